```python
import math
import jax, jax.numpy as jnp
from jax import lax
import numpy as np

D_MODEL = 2048
BATCH = 1
SEQ = 16384
DEPTH = 2
DEC_BATCH = 16
DEC_SEQ = 16
PAST_LEN = 4096

CHUNK = 64
N_MIXERS = 2
S5_GROUP = 16
S5_GROUPS = D_MODEL // S5_GROUP
S5_STATE = 64
DT_MIN = 1e-3
DT_MAX = 1e-1
N_HEADS = 16
HEAD_DIM = D_MODEL // N_HEADS
PAST_CHUNKS = 8
BAND_PAST = PAST_CHUNKS * CHUNK
BAND = BAND_PAST + CHUNK
REL_CLIP = 128
ATTN_SCALE = HEAD_DIM ** -0.5
NEG_INF = -1e30
D_FF = 5632
PLE_DIM = 256
N_NORMS = 4
DN_ALPHA = (2 * DEPTH) ** 0.25
DN_BETA = (8 * DEPTH) ** -0.25
LN_EPS = 1e-5

kernel_name = "hybrid_s5_chunkattn_streaming_step"


def layer_norm(x, g, b):
    xf = x.astype(jnp.float32)
    mu = jnp.mean(xf, axis=-1, keepdims=True)
    var = jnp.mean(jnp.square(xf - mu), axis=-1, keepdims=True)
    y = (xf - mu) * lax.rsqrt(var + LN_EPS) * g.astype(jnp.float32) + b.astype(jnp.float32)
    return y.astype(x.dtype)


def post_norm(x, sub, g, b):
    return layer_norm(DN_ALPHA * x + sub, g, b)


def swiglu(x, w_in, w_out):
    gate, up = jnp.split(x @ w_in, 2, axis=-1)
    return (jax.nn.silu(gate) * up) @ w_out


def per_layer_embed(x, p, w_proj, w_gate):
    return (p.astype(x.dtype) @ w_proj) * jax.nn.sigmoid(x @ w_gate)


def s5_discretize(a_re, a_im, log_dt):
    f32 = jnp.float32
    ar, ai = a_re.astype(f32), a_im.astype(f32)
    dt = jnp.exp(log_dt.astype(f32))[:, None]
    mag = jnp.exp(ar * dt)
    ab_re, ab_im = mag * jnp.cos(ai * dt), mag * jnp.sin(ai * dt)
    den = ar * ar + ai * ai
    nr, ni = ab_re - 1.0, ab_im
    coef_re = (nr * ar + ni * ai) / den
    coef_im = (ni * ar - nr * ai) / den
    return ab_re, ab_im, coef_re, coef_im


def _ssm_combine(e1, e2):
    a1r, a1i, b1r, b1i = e1
    a2r, a2i, b2r, b2i = e2
    return (a2r * a1r - a2i * a1i,
            a2r * a1i + a2i * a1r,
            a2r * b1r - a2i * b1i + b2r,
            a2r * b1i + a2i * b1r + b2i)


def s5_mixer(x, h0_re, h0_im, a_re, a_im, log_dt, b_re, b_im, c_re, c_im, d_skip, w_glu):
    f32 = jnp.float32
    bsz, L, _ = x.shape
    T = min(CHUNK, L)
    nblk = L // T
    ab_re, ab_im, coef_re, coef_im = s5_discretize(a_re, a_im, log_dt)
    br, bi = b_re.astype(f32), b_im.astype(f32)
    cr, ci = c_re.astype(f32), c_im.astype(f32)
    dd = d_skip.astype(f32)
    u = x.astype(f32).reshape(bsz, nblk, T, S5_GROUPS, S5_GROUP).transpose(1, 0, 2, 3, 4)
    a_shape = (bsz, T, S5_GROUPS, S5_STATE)
    a_r = jnp.broadcast_to(ab_re, a_shape)
    a_i = jnp.broadcast_to(ab_im, a_shape)

    def block(h, ub):
        hr, hi = h
        pr = jnp.einsum('btgc,gpc->btgp', ub, br)
        pi = jnp.einsum('btgc,gpc->btgp', ub, bi)
        bu_r = coef_re * pr - coef_im * pi
        bu_i = coef_re * pi + coef_im * pr
        acr, aci, sr, si = lax.associative_scan(_ssm_combine, (a_r, a_i, bu_r, bu_i), axis=1)
        sr = acr * hr[:, None] - aci * hi[:, None] + sr
        si = acr * hi[:, None] + aci * hr[:, None] + si
        y = (jnp.einsum('btgp,gcp->btgc', sr, cr) - jnp.einsum('btgp,gcp->btgc', si, ci)
             + dd * ub)
        return (sr[:, -1], si[:, -1]), y

    (hr, hi), y = lax.scan(block, (h0_re.astype(f32), h0_im.astype(f32)), u)
    y = y.transpose(1, 0, 2, 3, 4).reshape(bsz, L, D_MODEL).astype(x.dtype)
    z = jax.nn.gelu(y)
    za, zb = jnp.split(z @ w_glu, 2, axis=-1)
    return za * jax.nn.sigmoid(zb), hr, hi


def rel_bias(table, q_off, nq, nk):
    dist = q_off + jnp.arange(nq)[:, None] - jnp.arange(nk)[None, :]
    idx = jnp.clip(dist, -REL_CLIP, REL_CLIP) + REL_CLIP
    return table[:, idx].astype(jnp.float32)


def band_attend(q, k, v, bias, valid):
    s = jnp.einsum('bqhd,bkhd->bhqk', q, k).astype(jnp.float32) * ATTN_SCALE + bias
    s = jnp.where(valid, s, NEG_INF)
    p = jax.nn.softmax(s, axis=-1).astype(v.dtype)
    return jnp.einsum('bhqk,bkhd->bqhd', p, v)


def split_qkv(x, w_qkv):
    bsz, L, _ = x.shape
    q, k, v = jnp.split(x @ w_qkv, 3, axis=-1)
    shp = (bsz, L, N_HEADS, HEAD_DIM)
    return q.reshape(shp), k.reshape(shp), v.reshape(shp)


def attn_prompt(x, w_qkv, w_o, table):
    bsz, L, _ = x.shape
    q, k, v = split_qkv(x, w_qkv)
    pad = ((0, 0), (BAND_PAST, 0), (0, 0), (0, 0))
    kp, vp = jnp.pad(k, pad), jnp.pad(v, pad)
    bias = rel_bias(table, BAND_PAST, CHUNK, BAND)

    def one_chunk(c):
        start = c * CHUNK
        qc = lax.dynamic_slice_in_dim(q, start, CHUNK, axis=1)
        kc = lax.dynamic_slice_in_dim(kp, start, BAND, axis=1)
        vc = lax.dynamic_slice_in_dim(vp, start, BAND, axis=1)
        valid = start - BAND_PAST + jnp.arange(BAND) >= 0
        return band_attend(qc, kc, vc, bias, valid)

    o = lax.map(one_chunk, jnp.arange(L // CHUNK))
    o = o.transpose(1, 0, 2, 3, 4).reshape(bsz, L, D_MODEL)
    rows = min(BAND_PAST, L)
    return o @ w_o, k[:, L - rows:], v[:, L - rows:]


def attn_sample(x, cache_k, cache_v, w_qkv, w_o, table):
    bsz, S, _ = x.shape
    q, k, v = split_qkv(x, w_qkv)
    R = cache_k.shape[1]
    kk = jnp.concatenate([cache_k.astype(k.dtype), k], axis=1)
    vv = jnp.concatenate([cache_v.astype(v.dtype), v], axis=1)
    bias = rel_bias(table, R, S, R + S)
    valid = jnp.ones((R + S,), dtype=bool)
    o = band_attend(q, kk, vv, bias, valid).reshape(bsz, S, D_MODEL)
    return o @ w_o, k, v


def setup_inputs(seed: int = 0) -> dict:
    key = jax.random.key(seed)
    ks = jax.random.split(key, 32)
    f32 = jnp.float32
    nrm = lambda k, shp, s: jax.random.normal(k, shp, f32) * s
    cache_rows = min(BAND_PAST, PAST_LEN)
    n_idx = jnp.arange(S5_STATE, dtype=f32)
    return {
        "x_prompt": nrm(ks[0], (BATCH, SEQ, D_MODEL), 1.0),
        "x_sample": nrm(ks[1], (DEC_BATCH, DEC_SEQ, D_MODEL), 1.0),
        "state_s5_re": nrm(ks[2], (DEC_BATCH, S5_GROUPS, S5_STATE), 0.1),
        "state_s5_im": nrm(ks[3], (DEC_BATCH, S5_GROUPS, S5_STATE), 0.1),
        "cache_k": nrm(ks[4], (DEC_BATCH, cache_rows, N_HEADS, HEAD_DIM), 1.0),
        "cache_v": nrm(ks[5], (DEC_BATCH, cache_rows, N_HEADS, HEAD_DIM), 1.0),
        "p_prompt": nrm(ks[6], (DEPTH, BATCH, SEQ, PLE_DIM), 1.0),
        "p_sample": nrm(ks[7], (DEPTH, DEC_BATCH, DEC_SEQ, PLE_DIM), 1.0),
        "ffn1_w_in": nrm(ks[8], (DEPTH, D_MODEL, 2 * D_FF), D_MODEL ** -0.5),
        "ffn1_w_out": nrm(ks[9], (DEPTH, D_FF, D_MODEL), DN_BETA * D_FF ** -0.5),
        "ffn2_w_in": nrm(ks[10], (DEPTH, D_MODEL, 2 * D_FF), D_MODEL ** -0.5),
        "ffn2_w_out": nrm(ks[11], (DEPTH, D_FF, D_MODEL), DN_BETA * D_FF ** -0.5),
        "ln_g": 1.0 + nrm(ks[12], (DEPTH, N_NORMS, D_MODEL), 0.02),
        "ln_b": nrm(ks[13], (DEPTH, N_NORMS, D_MODEL), 0.02),
        "ple_w_proj": nrm(ks[14], (DEPTH, PLE_DIM, D_MODEL), DN_BETA * PLE_DIM ** -0.5),
        "ple_w_gate": nrm(ks[15], (DEPTH, D_MODEL, D_MODEL), D_MODEL ** -0.5),
        "s5_a_re": -0.5 + nrm(ks[16], (S5_GROUPS, S5_STATE), 0.01),
        "s5_a_im": math.pi * n_idx[None, :] + nrm(ks[17], (S5_GROUPS, S5_STATE), 0.01),
        "s5_log_dt": jax.random.uniform(ks[18], (S5_GROUPS,), f32, math.log(DT_MIN), math.log(DT_MAX)),
        "s5_b_re": nrm(ks[19], (S5_GROUPS, S5_STATE, S5_GROUP), (2 * S5_GROUP) ** -0.5),
        "s5_b_im": nrm(ks[20], (S5_GROUPS, S5_STATE, S5_GROUP), (2 * S5_GROUP) ** -0.5),
        "s5_c_re": nrm(ks[21], (S5_GROUPS, S5_GROUP, S5_STATE), (2 * S5_STATE) ** -0.5),
        "s5_c_im": nrm(ks[22], (S5_GROUPS, S5_GROUP, S5_STATE), (2 * S5_STATE) ** -0.5),
        "s5_d": nrm(ks[23], (S5_GROUPS, S5_GROUP), 1.0),
        "s5_w_glu": nrm(ks[24], (D_MODEL, 2 * D_MODEL), DN_BETA * D_MODEL ** -0.5),
        "attn_w_qkv": nrm(ks[25], (D_MODEL, 3 * D_MODEL), D_MODEL ** -0.5),
        "attn_w_o": nrm(ks[26], (D_MODEL, D_MODEL), DN_BETA * D_MODEL ** -0.5),
        "attn_rel_bias": nrm(ks[27], (N_HEADS, 2 * REL_CLIP + 1), 0.5),
    }


def reference(x_prompt, x_sample, state_s5_re, state_s5_im, cache_k, cache_v, p_prompt, p_sample,
              ffn1_w_in, ffn1_w_out, ffn2_w_in, ffn2_w_out, ln_g, ln_b, ple_w_proj, ple_w_gate,
              s5_a_re, s5_a_im, s5_log_dt, s5_b_re, s5_b_im, s5_c_re, s5_c_im, s5_d, s5_w_glu,
              attn_w_qkv, attn_w_o, attn_rel_bias):
    def macaron(x, w_in, w_out, i, slot):
        return post_norm(x, 0.5 * swiglu(x, w_in, w_out), ln_g[i, slot], ln_b[i, slot])

    s5_w = (s5_a_re, s5_a_im, s5_log_dt, s5_b_re, s5_b_im, s5_c_re, s5_c_im, s5_d, s5_w_glu)
    xp, xs = x_prompt, x_sample
    for i in range(DEPTH):
        xp = macaron(xp, ffn1_w_in[i], ffn1_w_out[i], i, 0)
        xs = macaron(xs, ffn1_w_in[i], ffn1_w_out[i], i, 0)
        if i % N_MIXERS == 0:
            h0 = jnp.zeros((xp.shape[0], S5_GROUPS, S5_STATE), jnp.float32)
            mp, s5_re_p, s5_im_p = s5_mixer(xp, h0, h0, *s5_w)
            ms, s5_re_s, s5_im_s = s5_mixer(xs, state_s5_re, state_s5_im, *s5_w)
        else:
            mp, k_p, v_p = attn_prompt(xp, attn_w_qkv, attn_w_o, attn_rel_bias)
            ms, k_s, v_s = attn_sample(xs, cache_k, cache_v, attn_w_qkv, attn_w_o, attn_rel_bias)
        xp = post_norm(xp, mp, ln_g[i, 1], ln_b[i, 1])
        xs = post_norm(xs, ms, ln_g[i, 1], ln_b[i, 1])
        xp = macaron(xp, ffn2_w_in[i], ffn2_w_out[i], i, 2)
        xs = macaron(xs, ffn2_w_in[i], ffn2_w_out[i], i, 2)
        xp = post_norm(xp, per_layer_embed(xp, p_prompt[i], ple_w_proj[i], ple_w_gate[i]),
                       ln_g[i, 3], ln_b[i, 3])
        xs = post_norm(xs, per_layer_embed(xs, p_sample[i], ple_w_proj[i], ple_w_gate[i]),
                       ln_g[i, 3], ln_b[i, 3])
    return (xp, xs, s5_re_p, s5_im_p, k_p, v_p, s5_re_s, s5_im_s, k_s, v_s)
```

```python
import functools

import jax
import jax.numpy as jnp
from jax import lax
from jax.experimental import pallas as pl
from jax.experimental.pallas import tpu as pltpu

F32 = jnp.float32
BF16 = jnp.bfloat16

D_MODEL = 2048
DEPTH = 2
N_MIXERS = 2
CHUNK = 64
S5_GROUP = 16
S5_GROUPS = D_MODEL // S5_GROUP
S5_STATE = 64
N_HEADS = 16
HEAD_DIM = D_MODEL // N_HEADS
PAST_CHUNKS = 8
BAND_PAST = PAST_CHUNKS * CHUNK
BAND = BAND_PAST + CHUNK
REL_CLIP = 128
ATTN_SCALE = HEAD_DIM ** -0.5
NEG_INF = -1e30
D_FF = 5632
PLE_DIM = 256
DN_ALPHA = (2 * DEPTH) ** 0.25
LN_EPS = 1e-5

LANES = 128
VMEM_LIMIT = 56 * 1024 * 1024

TN = 512
TF = 512

S5_CB = D_MODEL // LANES
S5_GPB = LANES // S5_GROUP
S5_SPB = S5_GPB * S5_STATE
S5_CHUNK = 8
S5_UW = S5_CHUNK * LANES


def _cparams(*sem):
    return pltpu.CompilerParams(dimension_semantics=sem, vmem_limit_bytes=VMEM_LIMIT)


def _layer_norm(v, g, b):
    mu = jnp.mean(v, axis=-1, keepdims=True)
    d = v - mu
    var = jnp.mean(d * d, axis=-1, keepdims=True)
    return d * lax.rsqrt(var + LN_EPS) * g + b


def _post_norm_slabs(res, m_ref, g_ref, b_ref, nj):
    m = jnp.concatenate([m_ref[jj] for jj in range(nj)], axis=1)
    return _layer_norm(DN_ALPHA * res + m, g_ref[...], b_ref[...])


def _ffn_kernel(x_ref, wg_ref, wu_ref, wo_ref, g_ref, b_ref, o_ref, xb_ref, acc_ref, *, nj):
    j = pl.program_id(1)

    @pl.when(j == 0)
    def _():
        xb_ref[...] = x_ref[...].astype(BF16)
        acc_ref[...] = jnp.zeros_like(acc_ref)

    xb = xb_ref[...]
    gate = jnp.dot(xb, wg_ref[...], preferred_element_type=F32)
    up = jnp.dot(xb, wu_ref[...], preferred_element_type=F32)
    act = (gate * jax.nn.sigmoid(gate) * up).astype(BF16)
    acc_ref[...] += jnp.dot(act, wo_ref[...], preferred_element_type=F32)

    @pl.when(j == nj - 1)
    def _():
        v = DN_ALPHA * x_ref[...] + 0.5 * acc_ref[...]
        o_ref[...] = _layer_norm(v, g_ref[...], b_ref[...])


def _ffn_ln(x, w_in, w_out, g, b, tm):
    n = x.shape[0]
    nj = D_FF // TF
    return pl.pallas_call(
        functools.partial(_ffn_kernel, nj=nj),
        grid=(n // tm, nj),
        in_specs=[
            pl.BlockSpec((tm, D_MODEL), lambda i, j: (i, 0)),
            pl.BlockSpec((D_MODEL, TF), lambda i, j: (0, j)),
            pl.BlockSpec((D_MODEL, TF), lambda i, j: (0, j + nj)),
            pl.BlockSpec((TF, D_MODEL), lambda i, j: (j, 0)),
            pl.BlockSpec((1, D_MODEL), lambda i, j: (0, 0)),
            pl.BlockSpec((1, D_MODEL), lambda i, j: (0, 0)),
        ],
        out_specs=pl.BlockSpec((tm, D_MODEL), lambda i, j: (i, 0)),
        out_shape=jax.ShapeDtypeStruct((n, D_MODEL), F32),
        scratch_shapes=[pltpu.VMEM((tm, D_MODEL), BF16), pltpu.VMEM((tm, D_MODEL), F32)],
        compiler_params=_cparams("parallel", "arbitrary"),
        name="ffn_ln",
    )(x, w_in, w_in, w_out, g, b)


def _ple_kernel(x_ref, p_ref, wp_ref, wg_ref, g_ref, b_ref, o_ref, xb_ref, pb_ref, m_ref, *, nj):
    j = pl.program_id(1)

    @pl.when(j == 0)
    def _():
        xb_ref[...] = x_ref[...].astype(BF16)
        pb_ref[...] = p_ref[...].astype(BF16)

    proj = jnp.dot(pb_ref[...], wp_ref[...], preferred_element_type=F32)
    gate = jnp.dot(xb_ref[...], wg_ref[...], preferred_element_type=F32)
    m_ref[j] = proj * jax.nn.sigmoid(gate)

    @pl.when(j == nj - 1)
    def _():
        o_ref[...] = _post_norm_slabs(x_ref[...], m_ref, g_ref, b_ref, nj)


def _ple_ln(x, p, w_proj, w_gate, g, b, tm):
    n = x.shape[0]
    nj = D_MODEL // TN
    return pl.pallas_call(
        functools.partial(_ple_kernel, nj=nj),
        grid=(n // tm, nj),
        in_specs=[
            pl.BlockSpec((tm, D_MODEL), lambda i, j: (i, 0)),
            pl.BlockSpec((tm, PLE_DIM), lambda i, j: (i, 0)),
            pl.BlockSpec((PLE_DIM, TN), lambda i, j: (0, j)),
            pl.BlockSpec((D_MODEL, TN), lambda i, j: (0, j)),
            pl.BlockSpec((1, D_MODEL), lambda i, j: (0, 0)),
            pl.BlockSpec((1, D_MODEL), lambda i, j: (0, 0)),
        ],
        out_specs=pl.BlockSpec((tm, D_MODEL), lambda i, j: (i, 0)),
        out_shape=jax.ShapeDtypeStruct((n, D_MODEL), F32),
        scratch_shapes=[pltpu.VMEM((tm, D_MODEL), BF16), pltpu.VMEM((tm, PLE_DIM), BF16),
                        pltpu.VMEM((nj, tm, TN), F32)],
        compiler_params=_cparams("parallel", "arbitrary"),
        name="ple_ln",
    )(x, p, w_proj, w_gate, g, b)


def _s5_discretize(ar, ai, log_dt):
    dt = jnp.exp(log_dt)
    mag = jnp.exp(ar * dt)
    ab_re, ab_im = mag * jnp.cos(ai * dt), mag * jnp.sin(ai * dt)
    den = ar * ar + ai * ai
    nr, ni = ab_re - 1.0, ab_im
    return ab_re, ab_im, (nr * ar + ni * ai) / den, (ni * ar - nr * ai) / den


def _s5_prep_kernel(ar_r, ai_r, ldt_r, ar_c, ai_c, ldt_c, bre_ref, bim_ref, cre_ref, cim_ref,
                    mw_ref, v_ref, at_ref):
    ab_re, ab_im, cf_re, cf_im = _s5_discretize(ar_r[...], ai_r[...], ldt_r[...])
    b_re, b_im = bre_ref[...], bim_ref[...]
    c_re, c_im = cre_ref[...], cim_ref[...]
    p_re = cf_re * b_re - cf_im * b_im
    p_im = cf_re * b_im + cf_im * b_re
    ak_re, ak_im = jnp.ones_like(ab_re), jnp.zeros_like(ab_im)
    taps, pows = [], []
    for _ in range(S5_CHUNK):
        pows.append((p_re, p_im))
        taps.append(jnp.dot(p_re, c_re, precision=lax.Precision.HIGHEST, preferred_element_type=F32)
                    - jnp.dot(p_im, c_im, precision=lax.Precision.HIGHEST, preferred_element_type=F32))
        p_re, p_im = ab_re * p_re - ab_im * p_im, ab_re * p_im + ab_im * p_re
        ak_re, ak_im = ab_re * ak_re - ab_im * ak_im, ab_re * ak_im + ab_im * ak_re
    at_ref[:, :S5_SPB] = ak_re
    at_ref[:, S5_SPB:] = ak_im

    zero = jnp.zeros((LANES, LANES), BF16)
    for s in range(S5_CHUNK):
        rows = slice(s * LANES, (s + 1) * LANES)
        for t in range(S5_CHUNK):
            mw_ref[rows, t * LANES:(t + 1) * LANES] = taps[t - s].astype(BF16) if t >= s else zero
        w_re, w_im = pows[S5_CHUNK - 1 - s]
        mw_ref[rows, S5_UW:S5_UW + S5_SPB] = w_re.astype(BF16)
        mw_ref[rows, S5_UW + S5_SPB:] = w_im.astype(BF16)

    ac_re, ac_im, _, _ = _s5_discretize(ar_c[...], ai_c[...], ldt_c[...])
    ck_re, ck_im = ac_re, ac_im
    for t in range(S5_CHUNK):
        cols = slice(t * LANES, (t + 1) * LANES)
        v_ref[:S5_SPB, cols] = (ck_re * c_re - ck_im * c_im).astype(BF16)
        v_ref[S5_SPB:, cols] = (-(ck_im * c_re + ck_re * c_im)).astype(BF16)
        ck_re, ck_im = ck_re * ac_re - ck_im * ac_im, ck_re * ac_im + ck_im * ac_re


def _s5_prep(a_re, a_im, log_dt, b_re, b_im, c_re, c_im):
    eye = jnp.eye(S5_GPB, dtype=F32)

    def rows(v):
        return v.reshape(S5_CB, 1, S5_SPB)

    def cols(v):
        return v.reshape(S5_CB, S5_SPB, 1)

    def b_blockdiag(w):
        w4 = w.reshape(S5_CB, S5_GPB, S5_STATE, S5_GROUP).transpose(0, 1, 3, 2)
        return (w4[:, :, :, None, :] * eye[None, :, None, :, None]).reshape(S5_CB, LANES, S5_SPB)

    def c_blockdiag(w):
        w4 = w.reshape(S5_CB, S5_GPB, S5_GROUP, S5_STATE).transpose(0, 1, 3, 2)
        return (w4[:, :, :, None, :] * eye[None, :, None, :, None]).reshape(S5_CB, S5_SPB, LANES)

    ldt = jnp.repeat(log_dt, S5_STATE)
    row_spec = pl.BlockSpec((None, 1, S5_SPB), lambda c: (c, 0, 0))
    col_spec = pl.BlockSpec((None, S5_SPB, 1), lambda c: (c, 0, 0))
    b_spec = pl.BlockSpec((None, LANES, S5_SPB), lambda c: (c, 0, 0))
    c_spec = pl.BlockSpec((None, S5_SPB, LANES), lambda c: (c, 0, 0))
    return pl.pallas_call(
        _s5_prep_kernel,
        grid=(S5_CB,),
        in_specs=[row_spec, row_spec, row_spec, col_spec, col_spec, col_spec, b_spec, b_spec, c_spec, c_spec],
        out_specs=[
            pl.BlockSpec((None, S5_UW, S5_UW + 2 * S5_SPB), lambda c: (c, 0, 0)),
            pl.BlockSpec((None, 2 * S5_SPB, S5_UW), lambda c: (c, 0, 0)),
            pl.BlockSpec((None, 1, 2 * S5_SPB), lambda c: (c, 0, 0)),
        ],
        out_shape=[
            jax.ShapeDtypeStruct((S5_CB, S5_UW, S5_UW + 2 * S5_SPB), BF16),
            jax.ShapeDtypeStruct((S5_CB, 2 * S5_SPB, S5_UW), BF16),
            jax.ShapeDtypeStruct((S5_CB, 1, 2 * S5_SPB), F32),
        ],
        compiler_params=_cparams("parallel"),
        name="s5_prep",
    )(rows(a_re), rows(a_im), rows(ldt), cols(a_re), cols(a_im), cols(ldt),
      b_blockdiag(b_re), b_blockdiag(b_im), c_blockdiag(c_re), c_blockdiag(c_im))


def _s5_step(at_ref, h_re, h_im, s_re, s_im):
    a_re, a_im = at_ref[:, :S5_SPB], at_ref[:, S5_SPB:]
    return a_re * h_re - a_im * h_im + s_re, a_re * h_im + a_im * h_re + s_im


def _s5_prompt_kernel(*refs, tb, nr):
    x_refs = refs[:S5_CHUNK]
    mw_ref, v_ref, at_ref, d_ref, y_ref, hfin_ref, hc_ref, s_ref, h_ref = refs[S5_CHUNK:]
    r = pl.program_id(1)

    @pl.when(r == 0)
    def _():
        hc_ref[...] = jnp.zeros_like(hc_ref)

    u = jnp.concatenate([x[...] for x in x_refs], axis=1)
    res = jnp.dot(u.astype(BF16), mw_ref[...], preferred_element_type=F32)
    s_ref[...] = res[:, S5_UW:]

    def body(i, h):
        h_re, h_im = h
        row = pl.ds(i, 1)
        h_ref[row, :S5_SPB] = h_re
        h_ref[row, S5_SPB:] = h_im
        return _s5_step(at_ref, h_re, h_im, s_ref[row, :S5_SPB], s_ref[row, S5_SPB:])

    h_re, h_im = lax.fori_loop(0, tb, body, (hc_ref[:, :S5_SPB], hc_ref[:, S5_SPB:]), unroll=8)
    hc_ref[:, :S5_SPB] = h_re
    hc_ref[:, S5_SPB:] = h_im

    d_row = jnp.concatenate([d_ref[...]] * S5_CHUNK, axis=1)
    y_ref[...] = (res[:, :S5_UW]
                  + jnp.dot(h_ref[...].astype(BF16), v_ref[...], preferred_element_type=F32)
                  + d_row * u)

    @pl.when(r == nr - 1)
    def _():
        hfin_ref[...] = hc_ref[...]


def _s5_prompt(xv, mw, v, at, d, tb):
    nrow = xv.shape[0]
    nr = nrow // tb
    x_specs = [pl.BlockSpec((tb, LANES), lambda c, r, s=s: (r, s * S5_CB + c)) for s in range(S5_CHUNK)]
    return pl.pallas_call(
        functools.partial(_s5_prompt_kernel, tb=tb, nr=nr),
        grid=(S5_CB, nr),
        in_specs=x_specs + [
            pl.BlockSpec((None, S5_UW, S5_UW + 2 * S5_SPB), lambda c, r: (c, 0, 0)),
            pl.BlockSpec((None, 2 * S5_SPB, S5_UW), lambda c, r: (c, 0, 0)),
            pl.BlockSpec((None, 1, 2 * S5_SPB), lambda c, r: (c, 0, 0)),
            pl.BlockSpec((None, 1, LANES), lambda c, r: (c, 0, 0)),
        ],
        out_specs=[
            pl.BlockSpec((None, tb, S5_UW), lambda c, r: (c, r, 0)),
            pl.BlockSpec((None, 1, 2 * S5_SPB), lambda c, r: (c, 0, 0)),
        ],
        out_shape=[
            jax.ShapeDtypeStruct((S5_CB, nrow, S5_UW), F32),
            jax.ShapeDtypeStruct((S5_CB, 1, 2 * S5_SPB), F32),
        ],
        scratch_shapes=[pltpu.VMEM((1, 2 * S5_SPB), F32), pltpu.VMEM((tb, 2 * S5_SPB), F32),
                        pltpu.VMEM((tb, 2 * S5_SPB), F32)],
        compiler_params=_cparams("parallel", "arbitrary"),
        name="s5_prompt",
    )(*([xv] * S5_CHUNK), mw, v, at, d)


def _s5_sample_kernel(u_ref, hre_ref, him_ref, mw_ref, v_ref, at_ref, d_ref, y_ref, ore_ref, oim_ref, *, nblk):
    d_row = jnp.concatenate([d_ref[...]] * S5_CHUNK, axis=1)
    h_re, h_im = hre_ref[...], him_ref[...]
    for k in range(nblk):
        u = u_ref[k]
        res = jnp.dot(u.astype(BF16), mw_ref[...], preferred_element_type=F32)
        h = jnp.concatenate([h_re, h_im], axis=1).astype(BF16)
        y_ref[k] = res[:, :S5_UW] + jnp.dot(h, v_ref[...], preferred_element_type=F32) + d_row * u
        h_re, h_im = _s5_step(at_ref, h_re, h_im, res[:, S5_UW:S5_UW + S5_SPB], res[:, S5_UW + S5_SPB:])
    ore_ref[...] = h_re
    oim_ref[...] = h_im


def _s5_sample(u, h_re, h_im, mw, v, at, d):
    _, nblk, bsz, _ = u.shape
    h_spec = pl.BlockSpec((bsz, S5_SPB), lambda c: (0, c))
    u_spec = pl.BlockSpec((None, nblk, bsz, S5_UW), lambda c: (c, 0, 0, 0))
    return pl.pallas_call(
        functools.partial(_s5_sample_kernel, nblk=nblk),
        grid=(S5_CB,),
        in_specs=[
            u_spec, h_spec, h_spec,
            pl.BlockSpec((None, S5_UW, S5_UW + 2 * S5_SPB), lambda c: (c, 0, 0)),
            pl.BlockSpec((None, 2 * S5_SPB, S5_UW), lambda c: (c, 0, 0)),
            pl.BlockSpec((None, 1, 2 * S5_SPB), lambda c: (c, 0, 0)),
            pl.BlockSpec((None, 1, LANES), lambda c: (c, 0, 0)),
        ],
        out_specs=[u_spec, h_spec, h_spec],
        out_shape=[
            jax.ShapeDtypeStruct(u.shape, F32),
            jax.ShapeDtypeStruct(h_re.shape, F32),
            jax.ShapeDtypeStruct(h_im.shape, F32),
        ],
        compiler_params=_cparams("parallel"),
        name="s5_sample",
    )(u, h_re, h_im, mw, v, at, d)


def _glu_kernel(x_ref, y_ref, wa_ref, wb_ref, g_ref, b_ref, o_ref, zb_ref, m_ref, *, nj, folded):
    j = pl.program_id(2)

    @pl.when(j == 0)
    def _():
        if folded:
            for c in range(S5_CB):
                zb_ref[:, c * LANES:(c + 1) * LANES] = jax.nn.gelu(y_ref[c]).astype(BF16)
        else:
            zb_ref[...] = jax.nn.gelu(y_ref[...]).astype(BF16)

    z = zb_ref[...]
    za = jnp.dot(z, wa_ref[...], preferred_element_type=F32)
    zb = jnp.dot(z, wb_ref[...], preferred_element_type=F32)
    m_ref[j] = za * jax.nn.sigmoid(zb)

    @pl.when(j == nj - 1)
    def _():
        o_ref[...] = _post_norm_slabs(x_ref[...], m_ref, g_ref, b_ref, nj)


def _glu_ln(xv, y, w_glu, g, b, tm, folded):
    nrow = xv.shape[0]
    nt = xv.shape[1] // D_MODEL
    nj = D_MODEL // TN
    if folded:
        y_spec = pl.BlockSpec((S5_CB, tm, LANES), lambda i, t, j: (0, i, t))
    else:
        y_spec = pl.BlockSpec((tm, D_MODEL), lambda i, t, j: (i, 0))
    return pl.pallas_call(
        functools.partial(_glu_kernel, nj=nj, folded=folded),
        grid=(nrow // tm, nt, nj),
        in_specs=[
            pl.BlockSpec((tm, D_MODEL), lambda i, t, j: (i, t)),
            y_spec,
            pl.BlockSpec((D_MODEL, TN), lambda i, t, j: (0, j)),
            pl.BlockSpec((D_MODEL, TN), lambda i, t, j: (0, j + nj)),
            pl.BlockSpec((1, D_MODEL), lambda i, t, j: (0, 0)),
            pl.BlockSpec((1, D_MODEL), lambda i, t, j: (0, 0)),
        ],
        out_specs=pl.BlockSpec((tm, D_MODEL), lambda i, t, j: (i, t)),
        out_shape=jax.ShapeDtypeStruct(xv.shape, F32),
        scratch_shapes=[pltpu.VMEM((tm, D_MODEL), BF16), pltpu.VMEM((nj, tm, TN), F32)],
        compiler_params=_cparams("parallel", "parallel", "arbitrary"),
        name="glu_ln",
    )(xv, y, w_glu, w_glu, g, b)


def _proj_kernel(x_ref, w_ref, o_ref, xb_ref):
    @pl.when(pl.program_id(1) == 0)
    def _():
        xb_ref[...] = x_ref[...].astype(BF16)

    o_ref[...] = jnp.dot(xb_ref[...], w_ref[...], preferred_element_type=F32).astype(o_ref.dtype)


def _proj(x, w, col0, ncols, out_dtype, tm, row0=0, nrows=None):
    nrows = x.shape[0] if nrows is None else nrows
    rb, cb = row0 // tm, col0 // TN
    return pl.pallas_call(
        _proj_kernel,
        grid=(nrows // tm, ncols // TN),
        in_specs=[
            pl.BlockSpec((tm, D_MODEL), lambda i, j: (i + rb, 0)),
            pl.BlockSpec((D_MODEL, TN), lambda i, j: (0, j + cb)),
        ],
        out_specs=pl.BlockSpec((tm, TN), lambda i, j: (i, j)),
        out_shape=jax.ShapeDtypeStruct((nrows, ncols), out_dtype),
        scratch_shapes=[pltpu.VMEM((tm, D_MODEL), BF16)],
        compiler_params=_cparams("parallel", "arbitrary"),
        name="proj",
    )(x, w)


def _wo_kernel(x_ref, a_ref, w_ref, g_ref, b_ref, o_ref, m_ref, *, nj):
    j = pl.program_id(1)
    m_ref[j] = jnp.dot(a_ref[...], w_ref[...], preferred_element_type=F32)

    @pl.when(j == nj - 1)
    def _():
        o_ref[...] = _post_norm_slabs(x_ref[...], m_ref, g_ref, b_ref, nj)


def _wo_ln(x, a, w_o, g, b, tm):
    n = x.shape[0]
    nj = D_MODEL // TN
    return pl.pallas_call(
        functools.partial(_wo_kernel, nj=nj),
        grid=(n // tm, nj),
        in_specs=[
            pl.BlockSpec((tm, D_MODEL), lambda i, j: (i, 0)),
            pl.BlockSpec((tm, D_MODEL), lambda i, j: (i, 0)),
            pl.BlockSpec((D_MODEL, TN), lambda i, j: (0, j)),
            pl.BlockSpec((1, D_MODEL), lambda i, j: (0, 0)),
            pl.BlockSpec((1, D_MODEL), lambda i, j: (0, 0)),
        ],
        out_specs=pl.BlockSpec((tm, D_MODEL), lambda i, j: (i, 0)),
        out_shape=jax.ShapeDtypeStruct((n, D_MODEL), F32),
        scratch_shapes=[pltpu.VMEM((nj, tm, TN), F32)],
        compiler_params=_cparams("parallel", "arbitrary"),
        name="wo_ln",
    )(x, a, w_o, g, b)


def _bias_kernel(tab_ref, o_ref, *, q_off, nq, nk):
    h = pl.program_id(0)
    dist = (q_off + lax.broadcasted_iota(jnp.int32, (nq, nk), 0)
            - lax.broadcasted_iota(jnp.int32, (nq, nk), 1))
    idx = jnp.clip(dist, -REL_CLIP, REL_CLIP) + REL_CLIP

    def body(r, acc):
        return jnp.where(idx == r, tab_ref[h, r], acc)

    o_ref[...] = lax.fori_loop(0, 2 * REL_CLIP + 1, body, jnp.zeros((nq, nk), F32))


def _rel_bias(table, q_off, nq, nk):
    return pl.pallas_call(
        functools.partial(_bias_kernel, q_off=q_off, nq=nq, nk=nk),
        grid=(N_HEADS,),
        in_specs=[pl.BlockSpec(memory_space=pltpu.SMEM)],
        out_specs=pl.BlockSpec((None, nq, nk), lambda h: (h, 0, 0)),
        out_shape=jax.ShapeDtypeStruct((N_HEADS, nq, nk), F32),
        compiler_params=_cparams("arbitrary"),
        name="rel_bias",
    )(table)


def _attn_prompt_kernel(q_ref, kp_ref, kc_ref, vp_ref, vc_ref, bias_ref, o_ref, *, tq):
    i = pl.program_id(0)
    k_win = jnp.concatenate([kp_ref[...], kc_ref[...]], axis=0)
    v_win = jnp.concatenate([vp_ref[...], vc_ref[...]], axis=0)
    bias = bias_ref[...]
    col = lax.broadcasted_iota(jnp.int32, (CHUNK, BAND), 1)
    for c in range(tq // CHUNK):
        lo = tq - BAND_PAST + c * CHUNK
        q = q_ref[c * CHUNK:(c + 1) * CHUNK, :]
        k = k_win[lo:lo + BAND, :]
        v = v_win[lo:lo + BAND, :]
        s = lax.dot_general(q, k, (((1,), (1,)), ((), ())), preferred_element_type=F32)
        s = s * ATTN_SCALE + bias
        key_pos = (i - 1) * tq + lo + col
        s = jnp.where(key_pos >= 0, s, NEG_INF)
        e = jnp.exp(s - jnp.max(s, axis=-1, keepdims=True))
        den = jnp.sum(e, axis=-1, keepdims=True)
        pv = jnp.dot(e.astype(BF16), v, preferred_element_type=F32)
        o_ref[c * CHUNK:(c + 1) * CHUNK, :] = (pv / den).astype(o_ref.dtype)


def _attn_prompt(qkv, bias, tq):
    n = qkv.shape[0]
    hb = D_MODEL // HEAD_DIM

    def prev(i):
        return jnp.maximum(i - 1, 0)

    return pl.pallas_call(
        functools.partial(_attn_prompt_kernel, tq=tq),
        grid=(n // tq, N_HEADS),
        in_specs=[
            pl.BlockSpec((tq, HEAD_DIM), lambda i, h: (i, h)),
            pl.BlockSpec((tq, HEAD_DIM), lambda i, h: (prev(i), hb + h)),
            pl.BlockSpec((tq, HEAD_DIM), lambda i, h: (i, hb + h)),
            pl.BlockSpec((tq, HEAD_DIM), lambda i, h: (prev(i), 2 * hb + h)),
            pl.BlockSpec((tq, HEAD_DIM), lambda i, h: (i, 2 * hb + h)),
            pl.BlockSpec((None, CHUNK, BAND), lambda i, h: (h, 0, 0)),
        ],
        out_specs=pl.BlockSpec((tq, HEAD_DIM), lambda i, h: (i, h)),
        out_shape=jax.ShapeDtypeStruct((n, D_MODEL), BF16),
        compiler_params=_cparams("parallel", "parallel"),
        name="attn_prompt",
    )(qkv, qkv, qkv, qkv, qkv, bias)


def _attn_sample_kernel(q_ref, kn_ref, vn_ref, ck_ref, cv_ref, bias_ref, o_ref, *, ncache):
    s_len = q_ref.shape[0]
    for h in range(N_HEADS):
        cols = slice(h * HEAD_DIM, (h + 1) * HEAD_DIM)
        q = q_ref[:, cols]
        k_old = ck_ref[:, cols].astype(BF16)
        v_old = cv_ref[:, cols].astype(BF16)
        k_new, v_new = kn_ref[:, cols], vn_ref[:, cols]
        dims = (((1,), (1,)), ((), ()))
        s_old = lax.dot_general(q, k_old, dims, preferred_element_type=F32) * ATTN_SCALE + bias_ref[h, :, :ncache]
        s_new = (lax.dot_general(q, k_new, dims, preferred_element_type=F32) * ATTN_SCALE
                 + bias_ref[h, :, ncache:ncache + s_len])
        m = jnp.maximum(jnp.max(s_old, axis=-1, keepdims=True), jnp.max(s_new, axis=-1, keepdims=True))
        e_old, e_new = jnp.exp(s_old - m), jnp.exp(s_new - m)
        den = jnp.sum(e_old, axis=-1, keepdims=True) + jnp.sum(e_new, axis=-1, keepdims=True)
        pv = (jnp.dot(e_old.astype(BF16), v_old, preferred_element_type=F32)
              + jnp.dot(e_new.astype(BF16), v_new, preferred_element_type=F32))
        o_ref[:, cols] = (pv / den).astype(o_ref.dtype)


def _attn_sample(qkv, cache_k, cache_v, bias, bsz, s_len):
    ncache = cache_k.shape[1]
    return pl.pallas_call(
        functools.partial(_attn_sample_kernel, ncache=ncache),
        grid=(bsz,),
        in_specs=[
            pl.BlockSpec((s_len, D_MODEL), lambda b: (b, 0)),
            pl.BlockSpec((s_len, D_MODEL), lambda b: (b, 1)),
            pl.BlockSpec((s_len, D_MODEL), lambda b: (b, 2)),
            pl.BlockSpec((None, ncache, D_MODEL), lambda b: (b, 0, 0)),
            pl.BlockSpec((None, ncache, D_MODEL), lambda b: (b, 0, 0)),
            pl.BlockSpec(bias.shape, lambda b: (0, 0, 0)),
        ],
        out_specs=pl.BlockSpec((s_len, D_MODEL), lambda b: (b, 0)),
        out_shape=jax.ShapeDtypeStruct((bsz * s_len, D_MODEL), BF16),
        compiler_params=_cparams("parallel"),
        name="attn_sample",
    )(qkv, qkv, qkv, cache_k, cache_v, bias)


def kernel(x_prompt, x_sample, state_s5_re, state_s5_im, cache_k, cache_v, p_prompt, p_sample, ffn1_w_in, ffn1_w_out, ffn2_w_in, ffn2_w_out, ln_g, ln_b, ple_w_proj, ple_w_gate, s5_a_re, s5_a_im, s5_log_dt, s5_b_re, s5_b_im, s5_c_re, s5_c_im, s5_d, s5_w_glu, attn_w_qkv, attn_w_o, attn_rel_bias):
    bsz_p, seq, _ = x_prompt.shape
    bsz_s, s_len, _ = x_sample.shape
    assert bsz_p == 1 and s_len % S5_CHUNK == 0 and seq % (S5_CHUNK * 512) == 0
    n_p, n_s = bsz_p * seq, bsz_s * s_len
    tm_p, tm_s = 512, n_s

    xp = x_prompt.reshape(n_p, D_MODEL)
    xs = x_sample.reshape(n_s, D_MODEL)
    w1_in, w1_out = ffn1_w_in.astype(BF16), ffn1_w_out.astype(BF16)
    w2_in, w2_out = ffn2_w_in.astype(BF16), ffn2_w_out.astype(BF16)
    w_proj, w_gate = ple_w_proj.astype(BF16), ple_w_gate.astype(BF16)
    w_glu, w_qkv, w_o = s5_w_glu.astype(BF16), attn_w_qkv.astype(BF16), attn_w_o.astype(BF16)

    def norm(i, slot):
        return ln_g[i, slot].reshape(1, D_MODEL), ln_b[i, slot].reshape(1, D_MODEL)

    outs = {}
    for i in range(DEPTH):
        xp = _ffn_ln(xp, w1_in[i], w1_out[i], *norm(i, 0), tm_p)
        xs = _ffn_ln(xs, w1_in[i], w1_out[i], *norm(i, 0), tm_s)
        if i % N_MIXERS == 0:
            mw, v, at = _s5_prep(s5_a_re, s5_a_im, s5_log_dt, s5_b_re, s5_b_im, s5_c_re, s5_c_im)
            d = s5_d.reshape(S5_CB, 1, LANES)
            xv = xp.reshape(n_p // S5_CHUNK, S5_CHUNK * D_MODEL)
            y, h_fin = _s5_prompt(xv, mw, v, at, d, tb=512)
            xp = _glu_ln(xv, y, w_glu, *norm(i, 1), tm=512, folded=True).reshape(n_p, D_MODEL)
            outs["s5_p"] = (h_fin[:, 0, :S5_SPB].reshape(bsz_p, S5_GROUPS, S5_STATE),
                            h_fin[:, 0, S5_SPB:].reshape(bsz_p, S5_GROUPS, S5_STATE))
            nblk = s_len // S5_CHUNK
            u = (xs.reshape(bsz_s, nblk, S5_CHUNK, S5_CB, LANES).transpose(3, 1, 0, 2, 4)
                 .reshape(S5_CB, nblk, bsz_s, S5_UW))
            ys, hs_re, hs_im = _s5_sample(u, state_s5_re.reshape(bsz_s, -1), state_s5_im.reshape(bsz_s, -1),
                                          mw, v, at, d)
            ys = (ys.reshape(S5_CB, nblk, bsz_s, S5_CHUNK, LANES).transpose(2, 1, 3, 0, 4)
                  .reshape(n_s, D_MODEL))
            xs = _glu_ln(xs, ys, w_glu, *norm(i, 1), tm=tm_s, folded=False)
            outs["s5_s"] = (hs_re.reshape(bsz_s, S5_GROUPS, S5_STATE), hs_im.reshape(bsz_s, S5_GROUPS, S5_STATE))
        else:
            rows = min(BAND_PAST, seq)
            qkv_p = _proj(xp, w_qkv, 0, 3 * D_MODEL, BF16, tm_p)
            kv_tail = _proj(xp, w_qkv, D_MODEL, 2 * D_MODEL, F32, rows, row0=n_p - rows, nrows=rows)
            bias_p = _rel_bias(attn_rel_bias, BAND_PAST, CHUNK, BAND)
            att_p = _attn_prompt(qkv_p, bias_p, tq=512)
            xp = _wo_ln(xp, att_p, w_o, *norm(i, 1), tm_p)
            outs["kv_p"] = (kv_tail[:, :D_MODEL].reshape(bsz_p, rows, N_HEADS, HEAD_DIM),
                            kv_tail[:, D_MODEL:].reshape(bsz_p, rows, N_HEADS, HEAD_DIM))

            ncache = cache_k.shape[1]
            qkv_s = _proj(xs, w_qkv, 0, 3 * D_MODEL, BF16, tm_s)
            kv_s = _proj(xs, w_qkv, D_MODEL, 2 * D_MODEL, F32, tm_s)
            nk_pad = -(-(ncache + s_len) // LANES) * LANES
            bias_s = _rel_bias(attn_rel_bias, ncache, s_len, nk_pad)
            att_s = _attn_sample(qkv_s, cache_k.reshape(bsz_s, ncache, D_MODEL),
                                 cache_v.reshape(bsz_s, ncache, D_MODEL), bias_s, bsz_s, s_len)
            xs = _wo_ln(xs, att_s, w_o, *norm(i, 1), tm_s)
            outs["kv_s"] = (kv_s[:, :D_MODEL].reshape(bsz_s, s_len, N_HEADS, HEAD_DIM),
                            kv_s[:, D_MODEL:].reshape(bsz_s, s_len, N_HEADS, HEAD_DIM))
        xp = _ffn_ln(xp, w2_in[i], w2_out[i], *norm(i, 2), tm_p)
        xs = _ffn_ln(xs, w2_in[i], w2_out[i], *norm(i, 2), tm_s)
        xp = _ple_ln(xp, p_prompt[i].reshape(n_p, PLE_DIM), w_proj[i], w_gate[i], *norm(i, 3), tm_p)
        xs = _ple_ln(xs, p_sample[i].reshape(n_s, PLE_DIM), w_proj[i], w_gate[i], *norm(i, 3), tm_s)

    return (xp.reshape(bsz_p, seq, D_MODEL), xs.reshape(bsz_s, s_len, D_MODEL),
            *outs["s5_p"], *outs["kv_p"], *outs["s5_s"], *outs["kv_s"])
```

```python
import functools

import jax
import jax.numpy as jnp
from jax import lax
from jax.experimental import pallas as pl
from jax.experimental.pallas import tpu as pltpu

F32 = jnp.float32
BF16 = jnp.bfloat16

D_MODEL = 2048
DEPTH = 2
N_MIXERS = 2
CHUNK = 64
S5_GROUP = 16
S5_GROUPS = D_MODEL // S5_GROUP
S5_STATE = 64
N_HEADS = 16
HEAD_DIM = D_MODEL // N_HEADS
PAST_CHUNKS = 8
BAND_PAST = PAST_CHUNKS * CHUNK
BAND = BAND_PAST + CHUNK
REL_CLIP = 128
ATTN_SCALE = HEAD_DIM ** -0.5
NEG_INF = -1e30
D_FF = 5632
PLE_DIM = 256
DN_ALPHA = (2 * DEPTH) ** 0.25
LN_EPS = 1e-5

LANES = 128
VMEM_LIMIT = 56 * 1024 * 1024

TF = 512

S5_CB = D_MODEL // LANES
S5_GPB = LANES // S5_GROUP
S5_SPB = S5_GPB * S5_STATE
S5_CHUNK = 8
S5_UW = S5_CHUNK * LANES


def _cparams(*sem):
    return pltpu.CompilerParams(dimension_semantics=sem, vmem_limit_bytes=VMEM_LIMIT)


def _layer_norm(v, g, b):
    mu = jnp.mean(v, axis=-1, keepdims=True)
    d = v - mu
    var = jnp.mean(d * d, axis=-1, keepdims=True)
    return d * lax.rsqrt(var + LN_EPS) * g + b


def _post_norm_slabs(res, m_ref, g_ref, b_ref, nj):
    m = jnp.concatenate([m_ref[jj] for jj in range(nj)], axis=1)
    return _layer_norm(DN_ALPHA * res + m, g_ref[...], b_ref[...])


def _ffn_kernel(x_ref, wg_ref, wu_ref, wo_ref, g_ref, b_ref, o_ref, xb_ref, acc_ref, *, nj):
    j = pl.program_id(1)

    @pl.when(j == 0)
    def _():
        xb_ref[...] = x_ref[...].astype(BF16)

    xb = xb_ref[...]
    gate = jnp.dot(xb, wg_ref[...], preferred_element_type=F32)
    up = jnp.dot(xb, wu_ref[...], preferred_element_type=F32)
    act = (gate * jax.nn.sigmoid(gate) * up).astype(BF16)
    part = jnp.dot(act, wo_ref[...], preferred_element_type=F32)

    @pl.when(j == 0)
    def _():
        acc_ref[...] = part

    @pl.when(j > 0)
    def _():
        acc_ref[...] += part

    @pl.when(j == nj - 1)
    def _():
        v = DN_ALPHA * x_ref[...] + 0.5 * acc_ref[...]
        o_ref[...] = _layer_norm(v, g_ref[...], b_ref[...])


def _ffn_ln(x, w_in, w_out, layer, g, b, tm):
    n = x.shape[0]
    nj = D_FF // TF
    return pl.pallas_call(
        functools.partial(_ffn_kernel, nj=nj),
        grid=(n // tm, nj),
        in_specs=[
            pl.BlockSpec((tm, D_MODEL), lambda i, j: (i, 0)),
            pl.BlockSpec((None, D_MODEL, TF), lambda i, j: (layer, 0, j)),
            pl.BlockSpec((None, D_MODEL, TF), lambda i, j: (layer, 0, j + nj)),
            pl.BlockSpec((None, TF, D_MODEL), lambda i, j: (layer, j, 0)),
            pl.BlockSpec((1, D_MODEL), lambda i, j: (0, 0)),
            pl.BlockSpec((1, D_MODEL), lambda i, j: (0, 0)),
        ],
        out_specs=pl.BlockSpec((tm, D_MODEL), lambda i, j: (i, 0)),
        out_shape=jax.ShapeDtypeStruct((n, D_MODEL), F32),
        scratch_shapes=[pltpu.VMEM((tm, D_MODEL), BF16), pltpu.VMEM((tm, D_MODEL), F32)],
        compiler_params=_cparams("parallel", "arbitrary"),
        name="ffn_ln",
    )(x, w_in, w_in, w_out, g, b)


def _ple_kernel(x_ref, p_ref, wp_ref, wg_ref, g_ref, b_ref, o_ref, xb_ref, pb_ref, m_ref, *, nj):
    j = pl.program_id(1)

    @pl.when(j == 0)
    def _():
        xb_ref[...] = x_ref[...].astype(BF16)
        pb_ref[...] = p_ref[...].astype(BF16)

    proj = jnp.dot(pb_ref[...], wp_ref[...], preferred_element_type=F32)
    gate = jnp.dot(xb_ref[...], wg_ref[...], preferred_element_type=F32)
    m_ref[j] = proj * jax.nn.sigmoid(gate)

    @pl.when(j == nj - 1)
    def _():
        o_ref[...] = _post_norm_slabs(x_ref[...], m_ref, g_ref, b_ref, nj)


def _ple_ln(x, p, w_proj, w_gate, layer, g, b, tm, tn):
    n = x.shape[0]
    nj = D_MODEL // tn
    return pl.pallas_call(
        functools.partial(_ple_kernel, nj=nj),
        grid=(n // tm, nj),
        in_specs=[
            pl.BlockSpec((tm, D_MODEL), lambda i, j: (i, 0)),
            pl.BlockSpec((None, tm, PLE_DIM), lambda i, j: (layer, i, 0)),
            pl.BlockSpec((None, PLE_DIM, tn), lambda i, j: (layer, 0, j)),
            pl.BlockSpec((None, D_MODEL, tn), lambda i, j: (layer, 0, j)),
            pl.BlockSpec((1, D_MODEL), lambda i, j: (0, 0)),
            pl.BlockSpec((1, D_MODEL), lambda i, j: (0, 0)),
        ],
        out_specs=pl.BlockSpec((tm, D_MODEL), lambda i, j: (i, 0)),
        out_shape=jax.ShapeDtypeStruct((n, D_MODEL), F32),
        scratch_shapes=[pltpu.VMEM((tm, D_MODEL), BF16), pltpu.VMEM((tm, PLE_DIM), BF16),
                        pltpu.VMEM((nj, tm, tn), F32)],
        compiler_params=_cparams("parallel", "arbitrary"),
        name="ple_ln",
    )(x, p, w_proj, w_gate, g, b)


def _s5_discretize(ar, ai, log_dt):
    dt = jnp.exp(log_dt)
    mag = jnp.exp(ar * dt)
    ab_re, ab_im = mag * jnp.cos(ai * dt), mag * jnp.sin(ai * dt)
    den = ar * ar + ai * ai
    nr, ni = ab_re - 1.0, ab_im
    return ab_re, ab_im, (nr * ar + ni * ai) / den, (ni * ar - nr * ai) / den


def _s5_prep_kernel(ar_r, ai_r, ldt_r, ar_c, ai_c, ldt_c, bre_ref, bim_ref, cre_ref, cim_ref,
                    mw_ref, v_ref, at_ref):
    ab_re, ab_im, cf_re, cf_im = _s5_discretize(ar_r[...], ai_r[...], ldt_r[...])
    b_re, b_im = bre_ref[...], bim_ref[...]
    c_re, c_im = cre_ref[...], cim_ref[...]
    p_re = cf_re * b_re - cf_im * b_im
    p_im = cf_re * b_im + cf_im * b_re
    ak_re, ak_im = jnp.ones_like(ab_re), jnp.zeros_like(ab_im)
    taps, pows = [], []
    for _ in range(S5_CHUNK):
        pows.append((p_re, p_im))
        taps.append(jnp.dot(p_re, c_re, precision=lax.Precision.HIGHEST, preferred_element_type=F32)
                    - jnp.dot(p_im, c_im, precision=lax.Precision.HIGHEST, preferred_element_type=F32))
        p_re, p_im = ab_re * p_re - ab_im * p_im, ab_re * p_im + ab_im * p_re
        ak_re, ak_im = ab_re * ak_re - ab_im * ak_im, ab_re * ak_im + ab_im * ak_re
    at_ref[:, :S5_SPB] = ak_re
    at_ref[:, S5_SPB:] = ak_im

    zero = jnp.zeros((LANES, LANES), BF16)
    for s in range(S5_CHUNK):
        rows = slice(s * LANES, (s + 1) * LANES)
        for t in range(S5_CHUNK):
            mw_ref[rows, t * LANES:(t + 1) * LANES] = taps[t - s].astype(BF16) if t >= s else zero
        w_re, w_im = pows[S5_CHUNK - 1 - s]
        mw_ref[rows, S5_UW:S5_UW + S5_SPB] = w_re.astype(BF16)
        mw_ref[rows, S5_UW + S5_SPB:] = w_im.astype(BF16)

    ac_re, ac_im, _, _ = _s5_discretize(ar_c[...], ai_c[...], ldt_c[...])
    ck_re, ck_im = ac_re, ac_im
    for t in range(S5_CHUNK):
        cols = slice(t * LANES, (t + 1) * LANES)
        v_ref[:S5_SPB, cols] = (ck_re * c_re - ck_im * c_im).astype(BF16)
        v_ref[S5_SPB:, cols] = (-(ck_im * c_re + ck_re * c_im)).astype(BF16)
        ck_re, ck_im = ck_re * ac_re - ck_im * ac_im, ck_re * ac_im + ck_im * ac_re


def _s5_prep(a_re, a_im, log_dt, b_re, b_im, c_re, c_im):
    eye = jnp.eye(S5_GPB, dtype=F32)

    def rows(v):
        return v.reshape(S5_CB, 1, S5_SPB)

    def cols(v):
        return v.reshape(S5_CB, S5_SPB, 1)

    def b_blockdiag(w):
        w4 = w.reshape(S5_CB, S5_GPB, S5_STATE, S5_GROUP).transpose(0, 1, 3, 2)
        return (w4[:, :, :, None, :] * eye[None, :, None, :, None]).reshape(S5_CB, LANES, S5_SPB)

    def c_blockdiag(w):
        w4 = w.reshape(S5_CB, S5_GPB, S5_GROUP, S5_STATE).transpose(0, 1, 3, 2)
        return (w4[:, :, :, None, :] * eye[None, :, None, :, None]).reshape(S5_CB, S5_SPB, LANES)

    ldt = jnp.repeat(log_dt, S5_STATE)
    row_spec = pl.BlockSpec((None, 1, S5_SPB), lambda c: (c, 0, 0))
    col_spec = pl.BlockSpec((None, S5_SPB, 1), lambda c: (c, 0, 0))
    b_spec = pl.BlockSpec((None, LANES, S5_SPB), lambda c: (c, 0, 0))
    c_spec = pl.BlockSpec((None, S5_SPB, LANES), lambda c: (c, 0, 0))
    return pl.pallas_call(
        _s5_prep_kernel,
        grid=(S5_CB,),
        in_specs=[row_spec, row_spec, row_spec, col_spec, col_spec, col_spec, b_spec, b_spec, c_spec, c_spec],
        out_specs=[
            pl.BlockSpec((None, S5_UW, S5_UW + 2 * S5_SPB), lambda c: (c, 0, 0)),
            pl.BlockSpec((None, 2 * S5_SPB, S5_UW), lambda c: (c, 0, 0)),
            pl.BlockSpec((None, 1, 2 * S5_SPB), lambda c: (c, 0, 0)),
        ],
        out_shape=[
            jax.ShapeDtypeStruct((S5_CB, S5_UW, S5_UW + 2 * S5_SPB), BF16),
            jax.ShapeDtypeStruct((S5_CB, 2 * S5_SPB, S5_UW), BF16),
            jax.ShapeDtypeStruct((S5_CB, 1, 2 * S5_SPB), F32),
        ],
        compiler_params=_cparams("parallel"),
        name="s5_prep",
    )(rows(a_re), rows(a_im), rows(ldt), cols(a_re), cols(a_im), cols(ldt),
      b_blockdiag(b_re), b_blockdiag(b_im), c_blockdiag(c_re), c_blockdiag(c_im))


def _s5_step(at_ref, h_re, h_im, s_re, s_im):
    a_re, a_im = at_ref[:, :S5_SPB], at_ref[:, S5_SPB:]
    return a_re * h_re - a_im * h_im + s_re, a_re * h_im + a_im * h_re + s_im


def _s5_prompt_kernel(x_ref, mw_ref, v_ref, at_ref, d_ref, y_ref, hfin_ref, hc_ref, s_ref, h_ref, *, tb, nr):
    r = pl.program_id(1)

    @pl.when(r == 0)
    def _():
        hc_ref[...] = jnp.zeros_like(hc_ref)

    u = jnp.concatenate([x_ref[pl.ds(s, tb, stride=S5_CHUNK), :] for s in range(S5_CHUNK)], axis=1)
    res = jnp.dot(u.astype(BF16), mw_ref[...], preferred_element_type=F32)
    s_ref[...] = res[:, S5_UW:]

    def body(i, h):
        h_re, h_im = h
        row = pl.ds(i, 1)
        h_ref[row, :S5_SPB] = h_re
        h_ref[row, S5_SPB:] = h_im
        return _s5_step(at_ref, h_re, h_im, s_ref[row, :S5_SPB], s_ref[row, S5_SPB:])

    h_re, h_im = lax.fori_loop(0, tb, body, (hc_ref[:, :S5_SPB], hc_ref[:, S5_SPB:]), unroll=8)
    hc_ref[:, :S5_SPB] = h_re
    hc_ref[:, S5_SPB:] = h_im

    d_row = jnp.concatenate([d_ref[...]] * S5_CHUNK, axis=1)
    y = (res[:, :S5_UW]
         + jnp.dot(h_ref[...].astype(BF16), v_ref[...], preferred_element_type=F32)
         + d_row * u)
    for t in range(S5_CHUNK):
        y_ref[pl.ds(t, tb, stride=S5_CHUNK), :] = y[:, t * LANES:(t + 1) * LANES]

    @pl.when(r == nr - 1)
    def _():
        hfin_ref[...] = hc_ref[...]


def _s5_prompt(x, mw, v, at, d, tb):
    n = x.shape[0]
    rows = tb * S5_CHUNK
    nr = n // rows
    return pl.pallas_call(
        functools.partial(_s5_prompt_kernel, tb=tb, nr=nr),
        grid=(S5_CB, nr),
        in_specs=[
            pl.BlockSpec((rows, LANES), lambda c, r: (r, c)),
            pl.BlockSpec((None, S5_UW, S5_UW + 2 * S5_SPB), lambda c, r: (c, 0, 0)),
            pl.BlockSpec((None, 2 * S5_SPB, S5_UW), lambda c, r: (c, 0, 0)),
            pl.BlockSpec((None, 1, 2 * S5_SPB), lambda c, r: (c, 0, 0)),
            pl.BlockSpec((None, 1, LANES), lambda c, r: (c, 0, 0)),
        ],
        out_specs=[
            pl.BlockSpec((rows, LANES), lambda c, r: (r, c)),
            pl.BlockSpec((None, 1, 2 * S5_SPB), lambda c, r: (c, 0, 0)),
        ],
        out_shape=[
            jax.ShapeDtypeStruct((n, D_MODEL), F32),
            jax.ShapeDtypeStruct((S5_CB, 1, 2 * S5_SPB), F32),
        ],
        scratch_shapes=[pltpu.VMEM((1, 2 * S5_SPB), F32), pltpu.VMEM((tb, 2 * S5_SPB), F32),
                        pltpu.VMEM((tb, 2 * S5_SPB), F32)],
        compiler_params=_cparams("parallel", "arbitrary"),
        name="s5_prompt",
    )(x, mw, v, at, d)


def _s5_sample_kernel(u_ref, hre_ref, him_ref, mw_ref, v_ref, at_ref, d_ref, y_ref, ore_ref, oim_ref, *, nblk):
    d_row = jnp.concatenate([d_ref[...]] * S5_CHUNK, axis=1)
    h_re, h_im = hre_ref[...], him_ref[...]
    for k in range(nblk):
        u = u_ref[k]
        res = jnp.dot(u.astype(BF16), mw_ref[...], preferred_element_type=F32)
        h = jnp.concatenate([h_re, h_im], axis=1).astype(BF16)
        y_ref[k] = res[:, :S5_UW] + jnp.dot(h, v_ref[...], preferred_element_type=F32) + d_row * u
        h_re, h_im = _s5_step(at_ref, h_re, h_im, res[:, S5_UW:S5_UW + S5_SPB], res[:, S5_UW + S5_SPB:])
    ore_ref[...] = h_re
    oim_ref[...] = h_im


def _s5_sample(u, h_re, h_im, mw, v, at, d):
    _, nblk, bsz, _ = u.shape
    h_spec = pl.BlockSpec((bsz, S5_SPB), lambda c: (0, c))
    u_spec = pl.BlockSpec((None, nblk, bsz, S5_UW), lambda c: (c, 0, 0, 0))
    return pl.pallas_call(
        functools.partial(_s5_sample_kernel, nblk=nblk),
        grid=(S5_CB,),
        in_specs=[
            u_spec, h_spec, h_spec,
            pl.BlockSpec((None, S5_UW, S5_UW + 2 * S5_SPB), lambda c: (c, 0, 0)),
            pl.BlockSpec((None, 2 * S5_SPB, S5_UW), lambda c: (c, 0, 0)),
            pl.BlockSpec((None, 1, 2 * S5_SPB), lambda c: (c, 0, 0)),
            pl.BlockSpec((None, 1, LANES), lambda c: (c, 0, 0)),
        ],
        out_specs=[u_spec, h_spec, h_spec],
        out_shape=[
            jax.ShapeDtypeStruct(u.shape, F32),
            jax.ShapeDtypeStruct(h_re.shape, F32),
            jax.ShapeDtypeStruct(h_im.shape, F32),
        ],
        compiler_params=_cparams("parallel"),
        name="s5_sample",
    )(u, h_re, h_im, mw, v, at, d)


def _glu_kernel(x_ref, y_ref, wa_ref, wb_ref, g_ref, b_ref, o_ref, zb_ref, m_ref, *, nj):
    j = pl.program_id(1)

    @pl.when(j == 0)
    def _():
        zb_ref[...] = jax.nn.gelu(y_ref[...]).astype(BF16)

    z = zb_ref[...]
    za = jnp.dot(z, wa_ref[...], preferred_element_type=F32)
    zb = jnp.dot(z, wb_ref[...], preferred_element_type=F32)
    m_ref[j] = za * jax.nn.sigmoid(zb)

    @pl.when(j == nj - 1)
    def _():
        o_ref[...] = _post_norm_slabs(x_ref[...], m_ref, g_ref, b_ref, nj)


def _glu_ln(x, y, w_glu, g, b, tm, tn):
    n = x.shape[0]
    nj = D_MODEL // tn
    return pl.pallas_call(
        functools.partial(_glu_kernel, nj=nj),
        grid=(n // tm, nj),
        in_specs=[
            pl.BlockSpec((tm, D_MODEL), lambda i, j: (i, 0)),
            pl.BlockSpec((tm, D_MODEL), lambda i, j: (i, 0)),
            pl.BlockSpec((D_MODEL, tn), lambda i, j: (0, j)),
            pl.BlockSpec((D_MODEL, tn), lambda i, j: (0, j + nj)),
            pl.BlockSpec((1, D_MODEL), lambda i, j: (0, 0)),
            pl.BlockSpec((1, D_MODEL), lambda i, j: (0, 0)),
        ],
        out_specs=pl.BlockSpec((tm, D_MODEL), lambda i, j: (i, 0)),
        out_shape=jax.ShapeDtypeStruct((n, D_MODEL), F32),
        scratch_shapes=[pltpu.VMEM((tm, D_MODEL), BF16), pltpu.VMEM((nj, tm, tn), F32)],
        compiler_params=_cparams("parallel", "arbitrary"),
        name="glu_ln",
    )(x, y, w_glu, w_glu, g, b)


def _proj_kernel(x_ref, w_ref, o_ref, xb_ref):
    @pl.when(pl.program_id(1) == 0)
    def _():
        xb_ref[...] = x_ref[...].astype(BF16)

    o_ref[...] = jnp.dot(xb_ref[...], w_ref[...], preferred_element_type=F32).astype(o_ref.dtype)


def _proj(x, w, col0, ncols, out_dtype, tm, tn, row0=0, nrows=None):
    nrows = x.shape[0] if nrows is None else nrows
    assert row0 % tm == 0 and col0 % tn == 0
    rb, cb = row0 // tm, col0 // tn
    return pl.pallas_call(
        _proj_kernel,
        grid=(nrows // tm, ncols // tn),
        in_specs=[
            pl.BlockSpec((tm, D_MODEL), lambda i, j: (i + rb, 0)),
            pl.BlockSpec((D_MODEL, tn), lambda i, j: (0, j + cb)),
        ],
        out_specs=pl.BlockSpec((tm, tn), lambda i, j: (i, j)),
        out_shape=jax.ShapeDtypeStruct((nrows, ncols), out_dtype),
        scratch_shapes=[pltpu.VMEM((tm, D_MODEL), BF16)],
        compiler_params=_cparams("parallel", "arbitrary"),
        name="proj",
    )(x, w)


def _wo_kernel(x_ref, a_ref, w_ref, g_ref, b_ref, o_ref, m_ref, *, nj):
    j = pl.program_id(1)
    m_ref[j] = jnp.dot(a_ref[...], w_ref[...], preferred_element_type=F32)

    @pl.when(j == nj - 1)
    def _():
        o_ref[...] = _post_norm_slabs(x_ref[...], m_ref, g_ref, b_ref, nj)


def _wo_ln(x, a, w_o, g, b, tm, tn):
    n = x.shape[0]
    nj = D_MODEL // tn
    return pl.pallas_call(
        functools.partial(_wo_kernel, nj=nj),
        grid=(n // tm, nj),
        in_specs=[
            pl.BlockSpec((tm, D_MODEL), lambda i, j: (i, 0)),
            pl.BlockSpec((tm, D_MODEL), lambda i, j: (i, 0)),
            pl.BlockSpec((D_MODEL, tn), lambda i, j: (0, j)),
            pl.BlockSpec((1, D_MODEL), lambda i, j: (0, 0)),
            pl.BlockSpec((1, D_MODEL), lambda i, j: (0, 0)),
        ],
        out_specs=pl.BlockSpec((tm, D_MODEL), lambda i, j: (i, 0)),
        out_shape=jax.ShapeDtypeStruct((n, D_MODEL), F32),
        scratch_shapes=[pltpu.VMEM((nj, tm, tn), F32)],
        compiler_params=_cparams("parallel", "arbitrary"),
        name="wo_ln",
    )(x, a, w_o, g, b)


def _bias_kernel(tab_ref, o_ref, *, q_off, nq, nk, scale, band):
    h = pl.program_id(0)
    row = lax.broadcasted_iota(jnp.int32, (nq, nk), 0)
    col = lax.broadcasted_iota(jnp.int32, (nq, nk), 1)
    idx = jnp.clip(q_off + row - col, -REL_CLIP, REL_CLIP) + REL_CLIP

    def body(r, acc):
        return jnp.where(idx == r, tab_ref[h, r] * scale, acc)

    bias = lax.fori_loop(0, 2 * REL_CLIP + 1, body, jnp.zeros((nq, nk), F32))
    if band:
        first = row & ~(CHUNK - 1)
        bias = jnp.where((col >= first) & (col < first + BAND), bias, NEG_INF)
    o_ref[...] = bias


def _rel_bias(table, q_off, nq, nk, scale=1.0, band=False):
    return pl.pallas_call(
        functools.partial(_bias_kernel, q_off=q_off, nq=nq, nk=nk, scale=scale, band=band),
        grid=(N_HEADS,),
        in_specs=[pl.BlockSpec(memory_space=pltpu.SMEM)],
        out_specs=pl.BlockSpec((None, nq, nk), lambda h: (h, 0, 0)),
        out_shape=jax.ShapeDtypeStruct((N_HEADS, nq, nk), F32),
        compiler_params=_cparams("arbitrary"),
        name="rel_bias",
    )(table)


ATT_TQ = BAND_PAST
ATT_GQ = 4 * CHUNK
ATT_GK = ATT_GQ + BAND_PAST
ATT_HB = 4
LOG2E = 1.4426950408889634


def _attn_prompt_kernel(q_ref, kp_ref, kc_ref, vp_ref, vc_ref, bias_ref, o_ref, s_ref, p_ref, l_ref):
    i = pl.program_id(0)
    ng = ATT_TQ // ATT_GQ
    dims = (((1,), (1,)), ((), ()))
    units = [(hh, g) for hh in range(ATT_HB) for g in range(ng)]

    def split(g):
        lo = g * ATT_GQ
        return lo, ATT_TQ - lo

    for u, (hh, g) in enumerate(units):
        cols = slice(hh * HEAD_DIM, (hh + 1) * HEAD_DIM)
        lo, n_prev = split(g)
        q = q_ref[lo:lo + ATT_GQ, cols]
        s_ref[u, :, :n_prev] = lax.dot_general(q, kp_ref[lo:, cols], dims, preferred_element_type=F32)
        s_ref[u, :, n_prev:] = lax.dot_general(q, kc_ref[:ATT_GK - n_prev, cols], dims, preferred_element_type=F32)

    @pl.when(i == 0)
    def _():
        for u, (hh, g) in enumerate(units):
            _, n_prev = split(g)
            s_ref[u, :, :n_prev] = jnp.full((ATT_GQ, n_prev), NEG_INF, F32)

    for u, (hh, g) in enumerate(units):
        x = s_ref[u] * (ATTN_SCALE * LOG2E) + bias_ref[hh]
        e = jnp.exp2(x - jnp.max(x, axis=-1, keepdims=True))
        l_ref[u] = jnp.sum(e, axis=-1, keepdims=True)
        p_ref[u] = e.astype(BF16)

    for u, (hh, g) in enumerate(units):
        cols = slice(hh * HEAD_DIM, (hh + 1) * HEAD_DIM)
        lo, n_prev = split(g)
        pv = (jnp.dot(p_ref[u, :, :n_prev], vp_ref[lo:, cols], preferred_element_type=F32)
              + jnp.dot(p_ref[u, :, n_prev:], vc_ref[:ATT_GK - n_prev, cols], preferred_element_type=F32))
        o_ref[lo:lo + ATT_GQ, cols] = (pv / l_ref[u]).astype(o_ref.dtype)


def _attn_prompt(qkv, bias):
    n = qkv.shape[0]
    width = ATT_HB * HEAD_DIM
    nhb = D_MODEL // width
    n_units = ATT_HB * (ATT_TQ // ATT_GQ)

    def prev(i):
        return jnp.maximum(i - 1, 0)

    return pl.pallas_call(
        _attn_prompt_kernel,
        grid=(n // ATT_TQ, nhb),
        in_specs=[
            pl.BlockSpec((ATT_TQ, width), lambda i, h: (i, h)),
            pl.BlockSpec((ATT_TQ, width), lambda i, h: (prev(i), nhb + h)),
            pl.BlockSpec((ATT_TQ, width), lambda i, h: (i, nhb + h)),
            pl.BlockSpec((ATT_TQ, width), lambda i, h: (prev(i), 2 * nhb + h)),
            pl.BlockSpec((ATT_TQ, width), lambda i, h: (i, 2 * nhb + h)),
            pl.BlockSpec((ATT_HB, ATT_GQ, ATT_GK), lambda i, h: (h, 0, 0)),
        ],
        out_specs=pl.BlockSpec((ATT_TQ, width), lambda i, h: (i, h)),
        out_shape=jax.ShapeDtypeStruct((n, D_MODEL), BF16),
        scratch_shapes=[pltpu.VMEM((n_units, ATT_GQ, ATT_GK), F32), pltpu.VMEM((n_units, ATT_GQ, ATT_GK), BF16),
                        pltpu.VMEM((n_units, ATT_GQ, 1), F32)],
        compiler_params=_cparams("parallel", "parallel"),
        name="attn_prompt",
    )(qkv, qkv, qkv, qkv, qkv, bias)


def _attn_sample_kernel(q_ref, kn_ref, vn_ref, ck_ref, cv_ref, bias_ref, o_ref, *, ncache):
    s_len = q_ref.shape[0]
    for h in range(N_HEADS):
        cols = slice(h * HEAD_DIM, (h + 1) * HEAD_DIM)
        q = q_ref[:, cols]
        k_old = ck_ref[:, cols].astype(BF16)
        v_old = cv_ref[:, cols].astype(BF16)
        k_new, v_new = kn_ref[:, cols], vn_ref[:, cols]
        dims = (((1,), (1,)), ((), ()))
        s_old = lax.dot_general(q, k_old, dims, preferred_element_type=F32) * ATTN_SCALE + bias_ref[h, :, :ncache]
        s_new = (lax.dot_general(q, k_new, dims, preferred_element_type=F32) * ATTN_SCALE
                 + bias_ref[h, :, ncache:ncache + s_len])
        m = jnp.maximum(jnp.max(s_old, axis=-1, keepdims=True), jnp.max(s_new, axis=-1, keepdims=True))
        e_old, e_new = jnp.exp(s_old - m), jnp.exp(s_new - m)
        den = jnp.sum(e_old, axis=-1, keepdims=True) + jnp.sum(e_new, axis=-1, keepdims=True)
        pv = (jnp.dot(e_old.astype(BF16), v_old, preferred_element_type=F32)
              + jnp.dot(e_new.astype(BF16), v_new, preferred_element_type=F32))
        o_ref[:, cols] = (pv / den).astype(o_ref.dtype)


def _attn_sample(qkv, cache_k, cache_v, bias, bsz, s_len):
    ncache = cache_k.shape[1]
    return pl.pallas_call(
        functools.partial(_attn_sample_kernel, ncache=ncache),
        grid=(bsz,),
        in_specs=[
            pl.BlockSpec((s_len, D_MODEL), lambda b: (b, 0)),
            pl.BlockSpec((s_len, D_MODEL), lambda b: (b, 1)),
            pl.BlockSpec((s_len, D_MODEL), lambda b: (b, 2)),
            pl.BlockSpec((None, ncache, D_MODEL), lambda b: (b, 0, 0)),
            pl.BlockSpec((None, ncache, D_MODEL), lambda b: (b, 0, 0)),
            pl.BlockSpec(bias.shape, lambda b: (0, 0, 0)),
        ],
        out_specs=pl.BlockSpec((s_len, D_MODEL), lambda b: (b, 0)),
        out_shape=jax.ShapeDtypeStruct((bsz * s_len, D_MODEL), BF16),
        compiler_params=_cparams("parallel"),
        name="attn_sample",
    )(qkv, qkv, qkv, cache_k, cache_v, bias)


def kernel(x_prompt, x_sample, state_s5_re, state_s5_im, cache_k, cache_v, p_prompt, p_sample, ffn1_w_in, ffn1_w_out, ffn2_w_in, ffn2_w_out, ln_g, ln_b, ple_w_proj, ple_w_gate, s5_a_re, s5_a_im, s5_log_dt, s5_b_re, s5_b_im, s5_c_re, s5_c_im, s5_d, s5_w_glu, attn_w_qkv, attn_w_o, attn_rel_bias):
    bsz_p, seq, _ = x_prompt.shape
    bsz_s, s_len, _ = x_sample.shape
    assert bsz_p == 1 and s_len % S5_CHUNK == 0 and seq % (S5_CHUNK * 512) == 0
    n_p, n_s = bsz_p * seq, bsz_s * s_len
    tm_p, tm_s = 512, n_s
    tm_proj, tn_wide = 1024, 1024

    xp = x_prompt.reshape(n_p, D_MODEL)
    xs = x_sample.reshape(n_s, D_MODEL)
    pp = p_prompt.reshape(DEPTH, n_p, PLE_DIM)
    ps = p_sample.reshape(DEPTH, n_s, PLE_DIM)
    w1_in, w1_out = ffn1_w_in.astype(BF16), ffn1_w_out.astype(BF16)
    w2_in, w2_out = ffn2_w_in.astype(BF16), ffn2_w_out.astype(BF16)
    w_proj, w_gate = ple_w_proj.astype(BF16), ple_w_gate.astype(BF16)
    w_glu, w_qkv, w_o = s5_w_glu.astype(BF16), attn_w_qkv.astype(BF16), attn_w_o.astype(BF16)

    def norm(i, slot):
        return ln_g[i, slot].reshape(1, D_MODEL), ln_b[i, slot].reshape(1, D_MODEL)

    outs = {}
    for i in range(DEPTH):
        xp = _ffn_ln(xp, w1_in, w1_out, i, *norm(i, 0), tm_p)
        xs = _ffn_ln(xs, w1_in, w1_out, i, *norm(i, 0), tm_s)
        if i % N_MIXERS == 0:
            mw, v, at = _s5_prep(s5_a_re, s5_a_im, s5_log_dt, s5_b_re, s5_b_im, s5_c_re, s5_c_im)
            d = s5_d.reshape(S5_CB, 1, LANES)
            y, h_fin = _s5_prompt(xp, mw, v, at, d, tb=512)
            xp = _glu_ln(xp, y, w_glu, *norm(i, 1), tm_p, tn_wide)
            outs["s5_p"] = (h_fin[:, 0, :S5_SPB].reshape(bsz_p, S5_GROUPS, S5_STATE),
                            h_fin[:, 0, S5_SPB:].reshape(bsz_p, S5_GROUPS, S5_STATE))
            nblk = s_len // S5_CHUNK
            u = (xs.reshape(bsz_s, nblk, S5_CHUNK, S5_CB, LANES).transpose(3, 1, 0, 2, 4)
                 .reshape(S5_CB, nblk, bsz_s, S5_UW))
            ys, hs_re, hs_im = _s5_sample(u, state_s5_re.reshape(bsz_s, -1), state_s5_im.reshape(bsz_s, -1),
                                          mw, v, at, d)
            ys = (ys.reshape(S5_CB, nblk, bsz_s, S5_CHUNK, LANES).transpose(2, 1, 3, 0, 4)
                  .reshape(n_s, D_MODEL))
            xs = _glu_ln(xs, ys, w_glu, *norm(i, 1), tm_s, tn_wide)
            outs["s5_s"] = (hs_re.reshape(bsz_s, S5_GROUPS, S5_STATE), hs_im.reshape(bsz_s, S5_GROUPS, S5_STATE))
        else:
            rows = min(BAND_PAST, seq)
            assert seq % ATT_TQ == 0
            qkv_p = _proj(xp, w_qkv, 0, 3 * D_MODEL, BF16, tm_proj, tn_wide)
            kv_tail = _proj(xp, w_qkv, D_MODEL, 2 * D_MODEL, F32, rows, tn_wide, row0=n_p - rows, nrows=rows)
            bias_p = _rel_bias(attn_rel_bias, BAND_PAST, ATT_GQ, ATT_GK, scale=LOG2E, band=True)
            att_p = _attn_prompt(qkv_p, bias_p)
            xp = _wo_ln(xp, att_p, w_o, *norm(i, 1), tm_p, tn_wide)
            outs["kv_p"] = (kv_tail[:, :D_MODEL].reshape(bsz_p, rows, N_HEADS, HEAD_DIM),
                            kv_tail[:, D_MODEL:].reshape(bsz_p, rows, N_HEADS, HEAD_DIM))

            ncache = cache_k.shape[1]
            qkv_s = _proj(xs, w_qkv, 0, 3 * D_MODEL, BF16, tm_s, tn_wide)
            kv_s = _proj(xs, w_qkv, D_MODEL, 2 * D_MODEL, F32, tm_s, tn_wide)
            nk_pad = -(-(ncache + s_len) // LANES) * LANES
            bias_s = _rel_bias(attn_rel_bias, ncache, s_len, nk_pad)
            att_s = _attn_sample(qkv_s, cache_k.reshape(bsz_s, ncache, D_MODEL),
                                 cache_v.reshape(bsz_s, ncache, D_MODEL), bias_s, bsz_s, s_len)
            xs = _wo_ln(xs, att_s, w_o, *norm(i, 1), tm_s, tn_wide)
            outs["kv_s"] = (kv_s[:, :D_MODEL].reshape(bsz_s, s_len, N_HEADS, HEAD_DIM),
                            kv_s[:, D_MODEL:].reshape(bsz_s, s_len, N_HEADS, HEAD_DIM))
        xp = _ffn_ln(xp, w2_in, w2_out, i, *norm(i, 2), tm_p)
        xs = _ffn_ln(xs, w2_in, w2_out, i, *norm(i, 2), tm_s)
        xp = _ple_ln(xp, pp, w_proj, w_gate, i, *norm(i, 3), tm_p, tn_wide)
        xs = _ple_ln(xs, ps, w_proj, w_gate, i, *norm(i, 3), tm_s, tn_wide)

    return (xp.reshape(bsz_p, seq, D_MODEL), xs.reshape(bsz_s, s_len, D_MODEL),
            *outs["s5_p"], *outs["kv_p"], *outs["s5_s"], *outs["kv_s"])
```

```python
import functools

import jax
import jax.numpy as jnp
from jax import lax
from jax.experimental import pallas as pl
from jax.experimental.pallas import tpu as pltpu

F32 = jnp.float32
BF16 = jnp.bfloat16

D_MODEL = 2048
DEPTH = 2
N_MIXERS = 2
CHUNK = 64
S5_GROUP = 16
S5_GROUPS = D_MODEL // S5_GROUP
S5_STATE = 64
N_HEADS = 16
HEAD_DIM = D_MODEL // N_HEADS
PAST_CHUNKS = 8
BAND_PAST = PAST_CHUNKS * CHUNK
BAND = BAND_PAST + CHUNK
REL_CLIP = 128
ATTN_SCALE = HEAD_DIM ** -0.5
NEG_INF = -1e30
D_FF = 5632
PLE_DIM = 256
DN_ALPHA = (2 * DEPTH) ** 0.25
LN_EPS = 1e-5

LANES = 128
VMEM_LIMIT = 60 * 1024 * 1024

TF = 512
FFN_SUB = 256
FFN_OUT = 512
LN_ROWS = 256

S5_CB = D_MODEL // LANES
S5_GPB = LANES // S5_GROUP
S5_SPB = S5_GPB * S5_STATE
S5_CHUNK = 8
S5_UW = S5_CHUNK * LANES


def _cparams(*sem):
    return pltpu.CompilerParams(dimension_semantics=sem, vmem_limit_bytes=VMEM_LIMIT)


def _layer_norm(v, g, b):
    mu = jnp.mean(v, axis=-1, keepdims=True)
    d = v - mu
    var = jnp.mean(d * d, axis=-1, keepdims=True)
    return d * lax.rsqrt(var + LN_EPS) * g + b


def _post_norm_slabs(res, m_ref, g_ref, b_ref, nj):
    m = jnp.concatenate([m_ref[jj] for jj in range(nj)], axis=1)
    return _layer_norm(DN_ALPHA * res + m, g_ref[...], b_ref[...])


def _ffn_kernel(x_ref, wg_ref, wu_ref, wo_ref, g_ref, b_ref, o_ref, xb_ref, *, nj):
    j = pl.program_id(1)

    @pl.when(j == 0)
    def _():
        x = x_ref[...]
        xb_ref[...] = x.astype(BF16)
        o_ref[...] = (2.0 * DN_ALPHA) * x

    xb = xb_ref[...]
    acts = []
    for h in range(TF // FFN_SUB):
        cs = slice(h * FFN_SUB, (h + 1) * FFN_SUB)
        gate = jnp.dot(xb, wg_ref[:, cs], preferred_element_type=F32)
        up = jnp.dot(xb, wu_ref[:, cs], preferred_element_type=F32)
        acts.append((gate * jax.nn.sigmoid(gate) * up).astype(BF16))
    act = jnp.concatenate(acts, axis=1)
    for n in range(D_MODEL // FFN_OUT):
        ns = slice(n * FFN_OUT, (n + 1) * FFN_OUT)
        o_ref[:, ns] += jnp.dot(act, wo_ref[:, ns], preferred_element_type=F32)

    @pl.when(j == nj - 1)
    def _():
        for r in range(o_ref.shape[0] // LN_ROWS):
            rs = slice(r * LN_ROWS, (r + 1) * LN_ROWS)
            o_ref[rs, :] = _layer_norm(0.5 * o_ref[rs, :], g_ref[...], b_ref[...])


def _ffn_ln(x, w_in, w_out, layer, g, b, tm):
    n = x.shape[0]
    nj = D_FF // TF
    return pl.pallas_call(
        functools.partial(_ffn_kernel, nj=nj),
        grid=(n // tm, nj),
        in_specs=[
            pl.BlockSpec((tm, D_MODEL), lambda i, j: (i, 0)),
            pl.BlockSpec((None, D_MODEL, TF), lambda i, j: (layer, 0, j)),
            pl.BlockSpec((None, D_MODEL, TF), lambda i, j: (layer, 0, j + nj)),
            pl.BlockSpec((None, TF, D_MODEL), lambda i, j: (layer, j, 0)),
            pl.BlockSpec((1, D_MODEL), lambda i, j: (0, 0)),
            pl.BlockSpec((1, D_MODEL), lambda i, j: (0, 0)),
        ],
        out_specs=pl.BlockSpec((tm, D_MODEL), lambda i, j: (i, 0)),
        out_shape=jax.ShapeDtypeStruct((n, D_MODEL), F32),
        scratch_shapes=[pltpu.VMEM((tm, D_MODEL), BF16)],
        compiler_params=_cparams("parallel", "arbitrary"),
        name="ffn_ln",
    )(x, w_in, w_in, w_out, g, b)


def _ple_kernel(x_ref, p_ref, wp_ref, wg_ref, g_ref, b_ref, o_ref, xb_ref, pb_ref, m_ref, *, nj):
    j = pl.program_id(1)

    @pl.when(j == 0)
    def _():
        xb_ref[...] = x_ref[...].astype(BF16)
        pb_ref[...] = p_ref[...].astype(BF16)

    proj = jnp.dot(pb_ref[...], wp_ref[...], preferred_element_type=F32)
    gate = jnp.dot(xb_ref[...], wg_ref[...], preferred_element_type=F32)
    m_ref[j] = proj * jax.nn.sigmoid(gate)

    @pl.when(j == nj - 1)
    def _():
        o_ref[...] = _post_norm_slabs(x_ref[...], m_ref, g_ref, b_ref, nj)


def _ple_ln(x, p, w_proj, w_gate, layer, g, b, tm, tn):
    n = x.shape[0]
    nj = D_MODEL // tn
    return pl.pallas_call(
        functools.partial(_ple_kernel, nj=nj),
        grid=(n // tm, nj),
        in_specs=[
            pl.BlockSpec((tm, D_MODEL), lambda i, j: (i, 0)),
            pl.BlockSpec((None, tm, PLE_DIM), lambda i, j: (layer, i, 0)),
            pl.BlockSpec((None, PLE_DIM, tn), lambda i, j: (layer, 0, j)),
            pl.BlockSpec((None, D_MODEL, tn), lambda i, j: (layer, 0, j)),
            pl.BlockSpec((1, D_MODEL), lambda i, j: (0, 0)),
            pl.BlockSpec((1, D_MODEL), lambda i, j: (0, 0)),
        ],
        out_specs=pl.BlockSpec((tm, D_MODEL), lambda i, j: (i, 0)),
        out_shape=jax.ShapeDtypeStruct((n, D_MODEL), F32),
        scratch_shapes=[pltpu.VMEM((tm, D_MODEL), BF16), pltpu.VMEM((tm, PLE_DIM), BF16),
                        pltpu.VMEM((nj, tm, tn), F32)],
        compiler_params=_cparams("parallel", "arbitrary"),
        name="ple_ln",
    )(x, p, w_proj, w_gate, g, b)


def _s5_discretize(ar, ai, log_dt):
    dt = jnp.exp(log_dt)
    mag = jnp.exp(ar * dt)
    ab_re, ab_im = mag * jnp.cos(ai * dt), mag * jnp.sin(ai * dt)
    den = ar * ar + ai * ai
    nr, ni = ab_re - 1.0, ab_im
    return ab_re, ab_im, (nr * ar + ni * ai) / den, (ni * ar - nr * ai) / den


def _s5_prep_kernel(ar_r, ai_r, ldt_r, ar_c, ai_c, ldt_c, bre_ref, bim_ref, cre_ref, cim_ref,
                    mw_ref, v_ref, at_ref):
    ab_re, ab_im, cf_re, cf_im = _s5_discretize(ar_r[...], ai_r[...], ldt_r[...])
    b_re, b_im = bre_ref[...], bim_ref[...]
    c_re, c_im = cre_ref[...], cim_ref[...]
    p_re = cf_re * b_re - cf_im * b_im
    p_im = cf_re * b_im + cf_im * b_re
    ak_re, ak_im = jnp.ones_like(ab_re), jnp.zeros_like(ab_im)
    taps, pows = [], []
    for _ in range(S5_CHUNK):
        pows.append((p_re, p_im))
        taps.append(jnp.dot(p_re, c_re, precision=lax.Precision.HIGHEST, preferred_element_type=F32)
                    - jnp.dot(p_im, c_im, precision=lax.Precision.HIGHEST, preferred_element_type=F32))
        p_re, p_im = ab_re * p_re - ab_im * p_im, ab_re * p_im + ab_im * p_re
        ak_re, ak_im = ab_re * ak_re - ab_im * ak_im, ab_re * ak_im + ab_im * ak_re
    at_ref[:, :S5_SPB] = ak_re
    at_ref[:, S5_SPB:] = ak_im

    zero = jnp.zeros((LANES, LANES), BF16)
    for s in range(S5_CHUNK):
        rows = slice(s * LANES, (s + 1) * LANES)
        for t in range(S5_CHUNK):
            mw_ref[rows, t * LANES:(t + 1) * LANES] = taps[t - s].astype(BF16) if t >= s else zero
        w_re, w_im = pows[S5_CHUNK - 1 - s]
        mw_ref[rows, S5_UW:S5_UW + S5_SPB] = w_re.astype(BF16)
        mw_ref[rows, S5_UW + S5_SPB:] = w_im.astype(BF16)

    ac_re, ac_im, _, _ = _s5_discretize(ar_c[...], ai_c[...], ldt_c[...])
    ck_re, ck_im = ac_re, ac_im
    for t in range(S5_CHUNK):
        cols = slice(t * LANES, (t + 1) * LANES)
        v_ref[:S5_SPB, cols] = (ck_re * c_re - ck_im * c_im).astype(BF16)
        v_ref[S5_SPB:, cols] = (-(ck_im * c_re + ck_re * c_im)).astype(BF16)
        ck_re, ck_im = ck_re * ac_re - ck_im * ac_im, ck_re * ac_im + ck_im * ac_re


def _s5_prep(a_re, a_im, log_dt, b_re, b_im, c_re, c_im):
    eye = jnp.eye(S5_GPB, dtype=F32)

    def rows(v):
        return v.reshape(S5_CB, 1, S5_SPB)

    def cols(v):
        return v.reshape(S5_CB, S5_SPB, 1)

    def b_blockdiag(w):
        w4 = w.reshape(S5_CB, S5_GPB, S5_STATE, S5_GROUP).transpose(0, 1, 3, 2)
        return (w4[:, :, :, None, :] * eye[None, :, None, :, None]).reshape(S5_CB, LANES, S5_SPB)

    def c_blockdiag(w):
        w4 = w.reshape(S5_CB, S5_GPB, S5_GROUP, S5_STATE).transpose(0, 1, 3, 2)
        return (w4[:, :, :, None, :] * eye[None, :, None, :, None]).reshape(S5_CB, S5_SPB, LANES)

    ldt = jnp.repeat(log_dt, S5_STATE)
    row_spec = pl.BlockSpec((None, 1, S5_SPB), lambda c: (c, 0, 0))
    col_spec = pl.BlockSpec((None, S5_SPB, 1), lambda c: (c, 0, 0))
    b_spec = pl.BlockSpec((None, LANES, S5_SPB), lambda c: (c, 0, 0))
    c_spec = pl.BlockSpec((None, S5_SPB, LANES), lambda c: (c, 0, 0))
    return pl.pallas_call(
        _s5_prep_kernel,
        grid=(S5_CB,),
        in_specs=[row_spec, row_spec, row_spec, col_spec, col_spec, col_spec, b_spec, b_spec, c_spec, c_spec],
        out_specs=[
            pl.BlockSpec((None, S5_UW, S5_UW + 2 * S5_SPB), lambda c: (c, 0, 0)),
            pl.BlockSpec((None, 2 * S5_SPB, S5_UW), lambda c: (c, 0, 0)),
            pl.BlockSpec((None, 1, 2 * S5_SPB), lambda c: (c, 0, 0)),
        ],
        out_shape=[
            jax.ShapeDtypeStruct((S5_CB, S5_UW, S5_UW + 2 * S5_SPB), BF16),
            jax.ShapeDtypeStruct((S5_CB, 2 * S5_SPB, S5_UW), BF16),
            jax.ShapeDtypeStruct((S5_CB, 1, 2 * S5_SPB), F32),
        ],
        compiler_params=_cparams("parallel"),
        name="s5_prep",
    )(rows(a_re), rows(a_im), rows(ldt), cols(a_re), cols(a_im), cols(ldt),
      b_blockdiag(b_re), b_blockdiag(b_im), c_blockdiag(c_re), c_blockdiag(c_im))


def _s5_step(at_ref, h_re, h_im, s_re, s_im):
    a_re, a_im = at_ref[:, :S5_SPB], at_ref[:, S5_SPB:]
    return a_re * h_re - a_im * h_im + s_re, a_re * h_im + a_im * h_re + s_im


def _s5_prompt_kernel(x_ref, mw_ref, v_ref, at_ref, d_ref, y_ref, hfin_ref, hc_ref, s_ref, h_ref, *, tb, nr):
    r = pl.program_id(1)

    @pl.when(r == 0)
    def _():
        hc_ref[...] = jnp.zeros_like(hc_ref)

    u = jnp.concatenate([x_ref[pl.ds(s, tb, stride=S5_CHUNK), :] for s in range(S5_CHUNK)], axis=1)
    res = jnp.dot(u.astype(BF16), mw_ref[...], preferred_element_type=F32)
    s_ref[...] = res[:, S5_UW:]

    def body(i, h):
        h_re, h_im = h
        row = pl.ds(i, 1)
        h_ref[row, :S5_SPB] = h_re
        h_ref[row, S5_SPB:] = h_im
        return _s5_step(at_ref, h_re, h_im, s_ref[row, :S5_SPB], s_ref[row, S5_SPB:])

    h_re, h_im = lax.fori_loop(0, tb, body, (hc_ref[:, :S5_SPB], hc_ref[:, S5_SPB:]), unroll=8)
    hc_ref[:, :S5_SPB] = h_re
    hc_ref[:, S5_SPB:] = h_im

    d_row = jnp.concatenate([d_ref[...]] * S5_CHUNK, axis=1)
    y = (res[:, :S5_UW]
         + jnp.dot(h_ref[...].astype(BF16), v_ref[...], preferred_element_type=F32)
         + d_row * u)
    for t in range(S5_CHUNK):
        y_ref[pl.ds(t, tb, stride=S5_CHUNK), :] = y[:, t * LANES:(t + 1) * LANES]

    @pl.when(r == nr - 1)
    def _():
        hfin_ref[...] = hc_ref[...]


def _s5_prompt(x, mw, v, at, d, tb):
    n = x.shape[0]
    rows = tb * S5_CHUNK
    nr = n // rows
    return pl.pallas_call(
        functools.partial(_s5_prompt_kernel, tb=tb, nr=nr),
        grid=(S5_CB, nr),
        in_specs=[
            pl.BlockSpec((rows, LANES), lambda c, r: (r, c)),
            pl.BlockSpec((None, S5_UW, S5_UW + 2 * S5_SPB), lambda c, r: (c, 0, 0)),
            pl.BlockSpec((None, 2 * S5_SPB, S5_UW), lambda c, r: (c, 0, 0)),
            pl.BlockSpec((None, 1, 2 * S5_SPB), lambda c, r: (c, 0, 0)),
            pl.BlockSpec((None, 1, LANES), lambda c, r: (c, 0, 0)),
        ],
        out_specs=[
            pl.BlockSpec((rows, LANES), lambda c, r: (r, c)),
            pl.BlockSpec((None, 1, 2 * S5_SPB), lambda c, r: (c, 0, 0)),
        ],
        out_shape=[
            jax.ShapeDtypeStruct((n, D_MODEL), F32),
            jax.ShapeDtypeStruct((S5_CB, 1, 2 * S5_SPB), F32),
        ],
        scratch_shapes=[pltpu.VMEM((1, 2 * S5_SPB), F32), pltpu.VMEM((tb, 2 * S5_SPB), F32),
                        pltpu.VMEM((tb, 2 * S5_SPB), F32)],
        compiler_params=_cparams("parallel", "arbitrary"),
        name="s5_prompt",
    )(x, mw, v, at, d)


def _s5_sample_kernel(u_ref, hre_ref, him_ref, mw_ref, v_ref, at_ref, d_ref, y_ref, ore_ref, oim_ref, *, nblk):
    d_row = jnp.concatenate([d_ref[...]] * S5_CHUNK, axis=1)
    h_re, h_im = hre_ref[...], him_ref[...]
    for k in range(nblk):
        u = u_ref[k]
        res = jnp.dot(u.astype(BF16), mw_ref[...], preferred_element_type=F32)
        h = jnp.concatenate([h_re, h_im], axis=1).astype(BF16)
        y_ref[k] = res[:, :S5_UW] + jnp.dot(h, v_ref[...], preferred_element_type=F32) + d_row * u
        h_re, h_im = _s5_step(at_ref, h_re, h_im, res[:, S5_UW:S5_UW + S5_SPB], res[:, S5_UW + S5_SPB:])
    ore_ref[...] = h_re
    oim_ref[...] = h_im


def _s5_sample(u, h_re, h_im, mw, v, at, d):
    _, nblk, bsz, _ = u.shape
    h_spec = pl.BlockSpec((bsz, S5_SPB), lambda c: (0, c))
    u_spec = pl.BlockSpec((None, nblk, bsz, S5_UW), lambda c: (c, 0, 0, 0))
    return pl.pallas_call(
        functools.partial(_s5_sample_kernel, nblk=nblk),
        grid=(S5_CB,),
        in_specs=[
            u_spec, h_spec, h_spec,
            pl.BlockSpec((None, S5_UW, S5_UW + 2 * S5_SPB), lambda c: (c, 0, 0)),
            pl.BlockSpec((None, 2 * S5_SPB, S5_UW), lambda c: (c, 0, 0)),
            pl.BlockSpec((None, 1, 2 * S5_SPB), lambda c: (c, 0, 0)),
            pl.BlockSpec((None, 1, LANES), lambda c: (c, 0, 0)),
        ],
        out_specs=[u_spec, h_spec, h_spec],
        out_shape=[
            jax.ShapeDtypeStruct(u.shape, F32),
            jax.ShapeDtypeStruct(h_re.shape, F32),
            jax.ShapeDtypeStruct(h_im.shape, F32),
        ],
        compiler_params=_cparams("parallel"),
        name="s5_sample",
    )(u, h_re, h_im, mw, v, at, d)


def _glu_kernel(x_ref, y_ref, wa_ref, wb_ref, g_ref, b_ref, o_ref, zb_ref, m_ref, *, nj):
    j = pl.program_id(1)

    @pl.when(j == 0)
    def _():
        zb_ref[...] = jax.nn.gelu(y_ref[...]).astype(BF16)

    z = zb_ref[...]
    za = jnp.dot(z, wa_ref[...], preferred_element_type=F32)
    zb = jnp.dot(z, wb_ref[...], preferred_element_type=F32)
    m_ref[j] = za * jax.nn.sigmoid(zb)

    @pl.when(j == nj - 1)
    def _():
        o_ref[...] = _post_norm_slabs(x_ref[...], m_ref, g_ref, b_ref, nj)


def _glu_ln(x, y, w_glu, g, b, tm, tn):
    n = x.shape[0]
    nj = D_MODEL // tn
    return pl.pallas_call(
        functools.partial(_glu_kernel, nj=nj),
        grid=(n // tm, nj),
        in_specs=[
            pl.BlockSpec((tm, D_MODEL), lambda i, j: (i, 0)),
            pl.BlockSpec((tm, D_MODEL), lambda i, j: (i, 0)),
            pl.BlockSpec((D_MODEL, tn), lambda i, j: (0, j)),
            pl.BlockSpec((D_MODEL, tn), lambda i, j: (0, j + nj)),
            pl.BlockSpec((1, D_MODEL), lambda i, j: (0, 0)),
            pl.BlockSpec((1, D_MODEL), lambda i, j: (0, 0)),
        ],
        out_specs=pl.BlockSpec((tm, D_MODEL), lambda i, j: (i, 0)),
        out_shape=jax.ShapeDtypeStruct((n, D_MODEL), F32),
        scratch_shapes=[pltpu.VMEM((tm, D_MODEL), BF16), pltpu.VMEM((nj, tm, tn), F32)],
        compiler_params=_cparams("parallel", "arbitrary"),
        name="glu_ln",
    )(x, y, w_glu, w_glu, g, b)


def _proj_kernel(x_ref, w_ref, o_ref, xb_ref):
    @pl.when(pl.program_id(1) == 0)
    def _():
        xb_ref[...] = x_ref[...].astype(BF16)

    o_ref[...] = jnp.dot(xb_ref[...], w_ref[...], preferred_element_type=F32).astype(o_ref.dtype)


def _proj(x, w, col0, ncols, out_dtype, tm, tn, row0=0, nrows=None):
    nrows = x.shape[0] if nrows is None else nrows
    assert row0 % tm == 0 and col0 % tn == 0
    rb, cb = row0 // tm, col0 // tn
    return pl.pallas_call(
        _proj_kernel,
        grid=(nrows // tm, ncols // tn),
        in_specs=[
            pl.BlockSpec((tm, D_MODEL), lambda i, j: (i + rb, 0)),
            pl.BlockSpec((D_MODEL, tn), lambda i, j: (0, j + cb)),
        ],
        out_specs=pl.BlockSpec((tm, tn), lambda i, j: (i, j)),
        out_shape=jax.ShapeDtypeStruct((nrows, ncols), out_dtype),
        scratch_shapes=[pltpu.VMEM((tm, D_MODEL), BF16)],
        compiler_params=_cparams("parallel", "arbitrary"),
        name="proj",
    )(x, w)


def _wo_kernel(x_ref, a_ref, w_ref, g_ref, b_ref, o_ref, m_ref, *, nj):
    j = pl.program_id(1)
    m_ref[j] = jnp.dot(a_ref[...], w_ref[...], preferred_element_type=F32)

    @pl.when(j == nj - 1)
    def _():
        o_ref[...] = _post_norm_slabs(x_ref[...], m_ref, g_ref, b_ref, nj)


def _wo_ln(x, a, w_o, g, b, tm, tn):
    n = x.shape[0]
    nj = D_MODEL // tn
    return pl.pallas_call(
        functools.partial(_wo_kernel, nj=nj),
        grid=(n // tm, nj),
        in_specs=[
            pl.BlockSpec((tm, D_MODEL), lambda i, j: (i, 0)),
            pl.BlockSpec((tm, D_MODEL), lambda i, j: (i, 0)),
            pl.BlockSpec((D_MODEL, tn), lambda i, j: (0, j)),
            pl.BlockSpec((1, D_MODEL), lambda i, j: (0, 0)),
            pl.BlockSpec((1, D_MODEL), lambda i, j: (0, 0)),
        ],
        out_specs=pl.BlockSpec((tm, D_MODEL), lambda i, j: (i, 0)),
        out_shape=jax.ShapeDtypeStruct((n, D_MODEL), F32),
        scratch_shapes=[pltpu.VMEM((nj, tm, tn), F32)],
        compiler_params=_cparams("parallel", "arbitrary"),
        name="wo_ln",
    )(x, a, w_o, g, b)


def _bias_kernel(tab_ref, o_ref, *, q_off, nq, nk, scale, band):
    h = pl.program_id(0)
    nw = -(-(nq + nk) // LANES) * LANES
    off = lax.broadcasted_iota(jnp.int32, (8, nw), 1)
    off = jnp.where(off < nk, off, off - nw)
    idx = jnp.clip(q_off - off, -REL_CLIP, REL_CLIP) + REL_CLIP

    def body(r, acc):
        return jnp.where(idx == r, tab_ref[h, r] * scale, acc)

    base = lax.fori_loop(0, 2 * REL_CLIP + 1, body, jnp.zeros((8, nw), F32))
    bias = pltpu.roll(jnp.broadcast_to(base[:1], (nq, nw)), 0, 1, stride=1, stride_axis=0)
    if band:
        row = lax.broadcasted_iota(jnp.int32, (nq, nw), 0)
        col = lax.broadcasted_iota(jnp.int32, (nq, nw), 1)
        first = row & ~(CHUNK - 1)
        bias = jnp.where((col >= first) & (col < first + BAND), bias, NEG_INF)
    o_ref[...] = bias[:, :nk]


def _rel_bias(table, q_off, nq, nk, scale=1.0, band=False):
    return pl.pallas_call(
        functools.partial(_bias_kernel, q_off=q_off, nq=nq, nk=nk, scale=scale, band=band),
        grid=(N_HEADS,),
        in_specs=[pl.BlockSpec(memory_space=pltpu.SMEM)],
        out_specs=pl.BlockSpec((None, nq, nk), lambda h: (h, 0, 0)),
        out_shape=jax.ShapeDtypeStruct((N_HEADS, nq, nk), F32),
        compiler_params=_cparams("arbitrary"),
        name="rel_bias",
    )(table)


ATT_TQ = BAND_PAST
ATT_GQ = 4 * CHUNK
ATT_GK = ATT_GQ + BAND_PAST
ATT_HB = 4
LOG2E = 1.4426950408889634


def _attn_prompt_kernel(q_ref, kp_ref, kc_ref, vp_ref, vc_ref, bias_ref, o_ref, s_ref, p_ref, l_ref):
    i = pl.program_id(0)
    ng = ATT_TQ // ATT_GQ
    dims = (((1,), (1,)), ((), ()))
    units = [(hh, g) for hh in range(ATT_HB) for g in range(ng)]

    def split(g):
        lo = g * ATT_GQ
        return lo, ATT_TQ - lo

    for u, (hh, g) in enumerate(units):
        cols = slice(hh * HEAD_DIM, (hh + 1) * HEAD_DIM)
        lo, n_prev = split(g)
        q = q_ref[lo:lo + ATT_GQ, cols]
        s_ref[u, :, :n_prev] = lax.dot_general(q, kp_ref[lo:, cols], dims, preferred_element_type=F32)
        s_ref[u, :, n_prev:] = lax.dot_general(q, kc_ref[:ATT_GK - n_prev, cols], dims, preferred_element_type=F32)

    @pl.when(i == 0)
    def _():
        for u, (hh, g) in enumerate(units):
            _, n_prev = split(g)
            s_ref[u, :, :n_prev] = jnp.full((ATT_GQ, n_prev), NEG_INF, F32)

    for u, (hh, g) in enumerate(units):
        x = s_ref[u] * (ATTN_SCALE * LOG2E) + bias_ref[hh]
        e = jnp.exp2(x - jnp.max(x, axis=-1, keepdims=True))
        l_ref[u] = jnp.sum(e, axis=-1, keepdims=True)
        p_ref[u] = e.astype(BF16)

    for u, (hh, g) in enumerate(units):
        cols = slice(hh * HEAD_DIM, (hh + 1) * HEAD_DIM)
        lo, n_prev = split(g)
        pv = (jnp.dot(p_ref[u, :, :n_prev], vp_ref[lo:, cols], preferred_element_type=F32)
              + jnp.dot(p_ref[u, :, n_prev:], vc_ref[:ATT_GK - n_prev, cols], preferred_element_type=F32))
        o_ref[lo:lo + ATT_GQ, cols] = (pv / l_ref[u]).astype(o_ref.dtype)


def _attn_prompt(qkv, bias):
    n = qkv.shape[0]
    width = ATT_HB * HEAD_DIM
    nhb = D_MODEL // width
    n_units = ATT_HB * (ATT_TQ // ATT_GQ)

    def prev(i):
        return jnp.maximum(i - 1, 0)

    return pl.pallas_call(
        _attn_prompt_kernel,
        grid=(n // ATT_TQ, nhb),
        in_specs=[
            pl.BlockSpec((ATT_TQ, width), lambda i, h: (i, h)),
            pl.BlockSpec((ATT_TQ, width), lambda i, h: (prev(i), nhb + h)),
            pl.BlockSpec((ATT_TQ, width), lambda i, h: (i, nhb + h)),
            pl.BlockSpec((ATT_TQ, width), lambda i, h: (prev(i), 2 * nhb + h)),
            pl.BlockSpec((ATT_TQ, width), lambda i, h: (i, 2 * nhb + h)),
            pl.BlockSpec((ATT_HB, ATT_GQ, ATT_GK), lambda i, h: (h, 0, 0)),
        ],
        out_specs=pl.BlockSpec((ATT_TQ, width), lambda i, h: (i, h)),
        out_shape=jax.ShapeDtypeStruct((n, D_MODEL), BF16),
        scratch_shapes=[pltpu.VMEM((n_units, ATT_GQ, ATT_GK), F32), pltpu.VMEM((n_units, ATT_GQ, ATT_GK), BF16),
                        pltpu.VMEM((n_units, ATT_GQ, 1), F32)],
        compiler_params=_cparams("parallel", "parallel"),
        name="attn_prompt",
    )(qkv, qkv, qkv, qkv, qkv, bias)


def _attn_sample_kernel(q_ref, kn_ref, vn_ref, ck_ref, cv_ref, bias_ref, o_ref, *, ncache):
    s_len = q_ref.shape[0]
    for h in range(N_HEADS):
        cols = slice(h * HEAD_DIM, (h + 1) * HEAD_DIM)
        q = q_ref[:, cols]
        k_old = ck_ref[pl.ds(h, ncache, stride=N_HEADS), :].astype(BF16)
        v_old = cv_ref[pl.ds(h, ncache, stride=N_HEADS), :].astype(BF16)
        k_new, v_new = kn_ref[:, cols], vn_ref[:, cols]
        dims = (((1,), (1,)), ((), ()))
        s_old = lax.dot_general(q, k_old, dims, preferred_element_type=F32) * ATTN_SCALE + bias_ref[h, :, :ncache]
        s_new = (lax.dot_general(q, k_new, dims, preferred_element_type=F32) * ATTN_SCALE
                 + bias_ref[h, :, ncache:ncache + s_len])
        m = jnp.maximum(jnp.max(s_old, axis=-1, keepdims=True), jnp.max(s_new, axis=-1, keepdims=True))
        e_old, e_new = jnp.exp(s_old - m), jnp.exp(s_new - m)
        den = jnp.sum(e_old, axis=-1, keepdims=True) + jnp.sum(e_new, axis=-1, keepdims=True)
        pv = (jnp.dot(e_old.astype(BF16), v_old, preferred_element_type=F32)
              + jnp.dot(e_new.astype(BF16), v_new, preferred_element_type=F32))
        o_ref[:, cols] = (pv / den).astype(o_ref.dtype)


def _attn_sample(qkv, cache_k, cache_v, bias, bsz, s_len):
    ncache = cache_k.shape[1] // N_HEADS
    return pl.pallas_call(
        functools.partial(_attn_sample_kernel, ncache=ncache),
        grid=(bsz,),
        in_specs=[
            pl.BlockSpec((s_len, D_MODEL), lambda b: (b, 0)),
            pl.BlockSpec((s_len, D_MODEL), lambda b: (b, 1)),
            pl.BlockSpec((s_len, D_MODEL), lambda b: (b, 2)),
            pl.BlockSpec((None, ncache * N_HEADS, HEAD_DIM), lambda b: (b, 0, 0)),
            pl.BlockSpec((None, ncache * N_HEADS, HEAD_DIM), lambda b: (b, 0, 0)),
            pl.BlockSpec(bias.shape, lambda b: (0, 0, 0)),
        ],
        out_specs=pl.BlockSpec((s_len, D_MODEL), lambda b: (b, 0)),
        out_shape=jax.ShapeDtypeStruct((bsz * s_len, D_MODEL), BF16),
        compiler_params=_cparams("parallel"),
        name="attn_sample",
    )(qkv, qkv, qkv, cache_k, cache_v, bias)


def kernel(x_prompt, x_sample, state_s5_re, state_s5_im, cache_k, cache_v, p_prompt, p_sample, ffn1_w_in, ffn1_w_out, ffn2_w_in, ffn2_w_out, ln_g, ln_b, ple_w_proj, ple_w_gate, s5_a_re, s5_a_im, s5_log_dt, s5_b_re, s5_b_im, s5_c_re, s5_c_im, s5_d, s5_w_glu, attn_w_qkv, attn_w_o, attn_rel_bias):
    bsz_p, seq, _ = x_prompt.shape
    bsz_s, s_len, _ = x_sample.shape
    assert bsz_p == 1 and s_len % S5_CHUNK == 0 and seq % (S5_CHUNK * 512) == 0
    n_p, n_s = bsz_p * seq, bsz_s * s_len
    tm_p, tm_s = 512, n_s
    tm_ffn, tm_proj, tn_wide = 1024, 1024, 1024

    xp = x_prompt.reshape(n_p, D_MODEL)
    xs = x_sample.reshape(n_s, D_MODEL)
    pp = p_prompt.reshape(DEPTH, n_p, PLE_DIM)
    ps = p_sample.reshape(DEPTH, n_s, PLE_DIM)
    w1_in, w1_out = ffn1_w_in.astype(BF16), ffn1_w_out.astype(BF16)
    w2_in, w2_out = ffn2_w_in.astype(BF16), ffn2_w_out.astype(BF16)
    w_proj, w_gate = ple_w_proj.astype(BF16), ple_w_gate.astype(BF16)
    w_glu, w_qkv, w_o = s5_w_glu.astype(BF16), attn_w_qkv.astype(BF16), attn_w_o.astype(BF16)

    def norm(i, slot):
        return ln_g[i, slot].reshape(1, D_MODEL), ln_b[i, slot].reshape(1, D_MODEL)

    outs = {}
    for i in range(DEPTH):
        xp = _ffn_ln(xp, w1_in, w1_out, i, *norm(i, 0), tm_ffn)
        xs = _ffn_ln(xs, w1_in, w1_out, i, *norm(i, 0), tm_s)
        if i % N_MIXERS == 0:
            mw, v, at = _s5_prep(s5_a_re, s5_a_im, s5_log_dt, s5_b_re, s5_b_im, s5_c_re, s5_c_im)
            d = s5_d.reshape(S5_CB, 1, LANES)
            y, h_fin = _s5_prompt(xp, mw, v, at, d, tb=512)
            xp = _glu_ln(xp, y, w_glu, *norm(i, 1), tm_p, tn_wide)
            outs["s5_p"] = (h_fin[:, 0, :S5_SPB].reshape(bsz_p, S5_GROUPS, S5_STATE),
                            h_fin[:, 0, S5_SPB:].reshape(bsz_p, S5_GROUPS, S5_STATE))
            nblk = s_len // S5_CHUNK
            u = (xs.reshape(bsz_s, nblk, S5_CHUNK, S5_CB, LANES).transpose(3, 1, 0, 2, 4)
                 .reshape(S5_CB, nblk, bsz_s, S5_UW))
            ys, hs_re, hs_im = _s5_sample(u, state_s5_re.reshape(bsz_s, -1), state_s5_im.reshape(bsz_s, -1),
                                          mw, v, at, d)
            ys = (ys.reshape(S5_CB, nblk, bsz_s, S5_CHUNK, LANES).transpose(2, 1, 3, 0, 4)
                  .reshape(n_s, D_MODEL))
            xs = _glu_ln(xs, ys, w_glu, *norm(i, 1), tm_s, tn_wide)
            outs["s5_s"] = (hs_re.reshape(bsz_s, S5_GROUPS, S5_STATE), hs_im.reshape(bsz_s, S5_GROUPS, S5_STATE))
        else:
            rows = min(BAND_PAST, seq)
            assert seq % ATT_TQ == 0
            qkv_p = _proj(xp, w_qkv, 0, 3 * D_MODEL, BF16, tm_proj, tn_wide)
            kv_tail = _proj(xp, w_qkv, D_MODEL, 2 * D_MODEL, F32, rows, tn_wide, row0=n_p - rows, nrows=rows)
            bias_p = _rel_bias(attn_rel_bias, BAND_PAST, ATT_GQ, ATT_GK, scale=LOG2E, band=True)
            att_p = _attn_prompt(qkv_p, bias_p)
            xp = _wo_ln(xp, att_p, w_o, *norm(i, 1), tm_p, tn_wide)
            outs["kv_p"] = (kv_tail[:, :D_MODEL].reshape(bsz_p, rows, N_HEADS, HEAD_DIM),
                            kv_tail[:, D_MODEL:].reshape(bsz_p, rows, N_HEADS, HEAD_DIM))

            ncache = cache_k.shape[1]
            qkv_s = _proj(xs, w_qkv, 0, 3 * D_MODEL, BF16, tm_s, tn_wide)
            kv_s = _proj(xs, w_qkv, D_MODEL, 2 * D_MODEL, F32, tm_s, tn_wide)
            nk_pad = -(-(ncache + s_len) // LANES) * LANES
            bias_s = _rel_bias(attn_rel_bias, ncache, s_len, nk_pad)
            att_s = _attn_sample(qkv_s, cache_k.reshape(bsz_s, ncache * N_HEADS, HEAD_DIM),
                                 cache_v.reshape(bsz_s, ncache * N_HEADS, HEAD_DIM), bias_s, bsz_s, s_len)
            xs = _wo_ln(xs, att_s, w_o, *norm(i, 1), tm_s, tn_wide)
            outs["kv_s"] = (kv_s[:, :D_MODEL].reshape(bsz_s, s_len, N_HEADS, HEAD_DIM),
                            kv_s[:, D_MODEL:].reshape(bsz_s, s_len, N_HEADS, HEAD_DIM))
        xp = _ffn_ln(xp, w2_in, w2_out, i, *norm(i, 2), tm_ffn)
        xs = _ffn_ln(xs, w2_in, w2_out, i, *norm(i, 2), tm_s)
        xp = _ple_ln(xp, pp, w_proj, w_gate, i, *norm(i, 3), tm_p, tn_wide)
        xs = _ple_ln(xs, ps, w_proj, w_gate, i, *norm(i, 3), tm_s, tn_wide)

    return (xp.reshape(bsz_p, seq, D_MODEL), xs.reshape(bsz_s, s_len, D_MODEL),
            *outs["s5_p"], *outs["kv_p"], *outs["s5_s"], *outs["kv_s"])
```

```python
import functools

import jax
import jax.numpy as jnp
from jax import lax
from jax.experimental import pallas as pl
from jax.experimental.pallas import tpu as pltpu

F32 = jnp.float32
BF16 = jnp.bfloat16

D_MODEL = 2048
DEPTH = 2
N_MIXERS = 2
CHUNK = 64
S5_GROUP = 16
S5_GROUPS = D_MODEL // S5_GROUP
S5_STATE = 64
N_HEADS = 16
HEAD_DIM = D_MODEL // N_HEADS
PAST_CHUNKS = 8
BAND_PAST = PAST_CHUNKS * CHUNK
BAND = BAND_PAST + CHUNK
REL_CLIP = 128
ATTN_SCALE = HEAD_DIM ** -0.5
NEG_INF = -1e30
D_FF = 5632
PLE_DIM = 256
DN_ALPHA = (2 * DEPTH) ** 0.25
LN_EPS = 1e-5

LANES = 128
VMEM_LIMIT = 60 * 1024 * 1024

TF = 512
FFN_SUB = 256
FFN_OUT = 512
LN_ROWS = 256
COL_TILE = 512

S5_CB = D_MODEL // LANES
S5_GPB = LANES // S5_GROUP
S5_SPB = S5_GPB * S5_STATE
S5_CHUNK = 8
S5_UW = S5_CHUNK * LANES


def _cparams(*sem):
    return pltpu.CompilerParams(dimension_semantics=sem, vmem_limit_bytes=VMEM_LIMIT)


def _layer_norm(v, g, b):
    mu = jnp.mean(v, axis=-1, keepdims=True)
    d = v - mu
    var = jnp.mean(d * d, axis=-1, keepdims=True)
    return d * lax.rsqrt(var + LN_EPS) * g + b


def _ffn_kernel(x_ref, wg_ref, wu_ref, wo_ref, g_ref, b_ref, o_ref, xb_ref, *, nj):
    j = pl.program_id(1)

    @pl.when(j == 0)
    def _():
        x = x_ref[...]
        xb_ref[...] = x.astype(BF16)
        o_ref[...] = (2.0 * DN_ALPHA) * x

    xb = xb_ref[...]
    acts = []
    for h in range(TF // FFN_SUB):
        cs = slice(h * FFN_SUB, (h + 1) * FFN_SUB)
        gate = jnp.dot(xb, wg_ref[:, cs], preferred_element_type=F32)
        up = jnp.dot(xb, wu_ref[:, cs], preferred_element_type=F32)
        acts.append((gate * jax.nn.sigmoid(gate) * up).astype(BF16))
    act = jnp.concatenate(acts, axis=1)
    for n in range(D_MODEL // FFN_OUT):
        ns = slice(n * FFN_OUT, (n + 1) * FFN_OUT)
        o_ref[:, ns] += jnp.dot(act, wo_ref[:, ns], preferred_element_type=F32)

    @pl.when(j == nj - 1)
    def _():
        for r in range(o_ref.shape[0] // LN_ROWS):
            rs = slice(r * LN_ROWS, (r + 1) * LN_ROWS)
            o_ref[rs, :] = _layer_norm(0.5 * o_ref[rs, :], g_ref[...], b_ref[...])


def _ffn_ln(x, w_in, w_out, layer, g, b, tm):
    n = x.shape[0]
    nj = D_FF // TF
    return pl.pallas_call(
        functools.partial(_ffn_kernel, nj=nj),
        grid=(n // tm, nj),
        in_specs=[
            pl.BlockSpec((tm, D_MODEL), lambda i, j: (i, 0)),
            pl.BlockSpec((None, D_MODEL, TF), lambda i, j: (layer, 0, j)),
            pl.BlockSpec((None, D_MODEL, TF), lambda i, j: (layer, 0, j + nj)),
            pl.BlockSpec((None, TF, D_MODEL), lambda i, j: (layer, j, 0)),
            pl.BlockSpec((1, D_MODEL), lambda i, j: (0, 0)),
            pl.BlockSpec((1, D_MODEL), lambda i, j: (0, 0)),
        ],
        out_specs=pl.BlockSpec((tm, D_MODEL), lambda i, j: (i, 0)),
        out_shape=jax.ShapeDtypeStruct((n, D_MODEL), F32),
        scratch_shapes=[pltpu.VMEM((tm, D_MODEL), BF16)],
        compiler_params=_cparams("parallel", "arbitrary"),
        name="ffn_ln",
    )(x, w_in, w_in, w_out, g, b)


def _norm_rows(o_ref, g_ref, b_ref):
    rows = min(LN_ROWS, o_ref.shape[0])
    for r in range(o_ref.shape[0] // rows):
        rs = slice(r * rows, (r + 1) * rows)
        o_ref[rs, :] = _layer_norm(o_ref[rs, :], g_ref[...], b_ref[...])


def _resident(block_shape, index_map):
    return pl.BlockSpec(block_shape, index_map, pipeline_mode=pl.Buffered(1))


def _ple_kernel(x_ref, p_ref, wp_ref, wg_ref, g_ref, b_ref, o_ref):
    xb = x_ref[...].astype(BF16)
    pb = p_ref[...].astype(BF16)
    for n in range(D_MODEL // COL_TILE):
        ns = slice(n * COL_TILE, (n + 1) * COL_TILE)
        proj = jnp.dot(pb, wp_ref[:, ns], preferred_element_type=F32)
        gate = jnp.dot(xb, wg_ref[:, ns], preferred_element_type=F32)
        o_ref[:, ns] = DN_ALPHA * x_ref[:, ns] + proj * jax.nn.sigmoid(gate)
    _norm_rows(o_ref, g_ref, b_ref)


def _ple_ln(x, p, w_proj, w_gate, layer, g, b, tm):
    n = x.shape[0]
    return pl.pallas_call(
        _ple_kernel,
        grid=(n // tm,),
        in_specs=[
            pl.BlockSpec((tm, D_MODEL), lambda i: (i, 0)),
            pl.BlockSpec((None, tm, PLE_DIM), lambda i: (layer, i, 0)),
            _resident((None, PLE_DIM, D_MODEL), lambda i: (layer, 0, 0)),
            _resident((None, D_MODEL, D_MODEL), lambda i: (layer, 0, 0)),
            _resident((1, D_MODEL), lambda i: (0, 0)),
            _resident((1, D_MODEL), lambda i: (0, 0)),
        ],
        out_specs=pl.BlockSpec((tm, D_MODEL), lambda i: (i, 0)),
        out_shape=jax.ShapeDtypeStruct((n, D_MODEL), F32),
        compiler_params=_cparams("parallel"),
        name="ple_ln",
    )(x, p, w_proj, w_gate, g, b)


def _s5_discretize(ar, ai, log_dt):
    dt = jnp.exp(log_dt)
    mag = jnp.exp(ar * dt)
    ab_re, ab_im = mag * jnp.cos(ai * dt), mag * jnp.sin(ai * dt)
    den = ar * ar + ai * ai
    nr, ni = ab_re - 1.0, ab_im
    return ab_re, ab_im, (nr * ar + ni * ai) / den, (ni * ar - nr * ai) / den


def _s5_prep_kernel(ar_r, ai_r, ldt_r, ar_c, ai_c, ldt_c, bre_ref, bim_ref, cre_ref, cim_ref,
                    mw_ref, v_ref, at_ref):
    ab_re, ab_im, cf_re, cf_im = _s5_discretize(ar_r[...], ai_r[...], ldt_r[...])
    b_re, b_im = bre_ref[...], bim_ref[...]
    c_re, c_im = cre_ref[...], cim_ref[...]
    p_re = cf_re * b_re - cf_im * b_im
    p_im = cf_re * b_im + cf_im * b_re
    ak_re, ak_im = jnp.ones_like(ab_re), jnp.zeros_like(ab_im)
    taps, pows = [], []
    for _ in range(S5_CHUNK):
        pows.append((p_re, p_im))
        taps.append(jnp.dot(p_re, c_re, precision=lax.Precision.HIGHEST, preferred_element_type=F32)
                    - jnp.dot(p_im, c_im, precision=lax.Precision.HIGHEST, preferred_element_type=F32))
        p_re, p_im = ab_re * p_re - ab_im * p_im, ab_re * p_im + ab_im * p_re
        ak_re, ak_im = ab_re * ak_re - ab_im * ak_im, ab_re * ak_im + ab_im * ak_re
    at_ref[:, :S5_SPB] = ak_re
    at_ref[:, S5_SPB:] = ak_im

    zero = jnp.zeros((LANES, LANES), BF16)
    for s in range(S5_CHUNK):
        rows = slice(s * LANES, (s + 1) * LANES)
        for t in range(S5_CHUNK):
            mw_ref[rows, t * LANES:(t + 1) * LANES] = taps[t - s].astype(BF16) if t >= s else zero
        w_re, w_im = pows[S5_CHUNK - 1 - s]
        mw_ref[rows, S5_UW:S5_UW + S5_SPB] = w_re.astype(BF16)
        mw_ref[rows, S5_UW + S5_SPB:] = w_im.astype(BF16)

    ac_re, ac_im, _, _ = _s5_discretize(ar_c[...], ai_c[...], ldt_c[...])
    ck_re, ck_im = ac_re, ac_im
    for t in range(S5_CHUNK):
        cols = slice(t * LANES, (t + 1) * LANES)
        v_ref[:S5_SPB, cols] = (ck_re * c_re - ck_im * c_im).astype(BF16)
        v_ref[S5_SPB:, cols] = (-(ck_im * c_re + ck_re * c_im)).astype(BF16)
        ck_re, ck_im = ck_re * ac_re - ck_im * ac_im, ck_re * ac_im + ck_im * ac_re


def _s5_prep(a_re, a_im, log_dt, b_re, b_im, c_re, c_im):
    eye = jnp.eye(S5_GPB, dtype=F32)

    def rows(v):
        return v.reshape(S5_CB, 1, S5_SPB)

    def cols(v):
        return v.reshape(S5_CB, S5_SPB, 1)

    def b_blockdiag(w):
        w4 = w.reshape(S5_CB, S5_GPB, S5_STATE, S5_GROUP).transpose(0, 1, 3, 2)
        return (w4[:, :, :, None, :] * eye[None, :, None, :, None]).reshape(S5_CB, LANES, S5_SPB)

    def c_blockdiag(w):
        w4 = w.reshape(S5_CB, S5_GPB, S5_GROUP, S5_STATE).transpose(0, 1, 3, 2)
        return (w4[:, :, :, None, :] * eye[None, :, None, :, None]).reshape(S5_CB, S5_SPB, LANES)

    ldt = jnp.repeat(log_dt, S5_STATE)
    row_spec = pl.BlockSpec((None, 1, S5_SPB), lambda c: (c, 0, 0))
    col_spec = pl.BlockSpec((None, S5_SPB, 1), lambda c: (c, 0, 0))
    b_spec = pl.BlockSpec((None, LANES, S5_SPB), lambda c: (c, 0, 0))
    c_spec = pl.BlockSpec((None, S5_SPB, LANES), lambda c: (c, 0, 0))
    return pl.pallas_call(
        _s5_prep_kernel,
        grid=(S5_CB,),
        in_specs=[row_spec, row_spec, row_spec, col_spec, col_spec, col_spec, b_spec, b_spec, c_spec, c_spec],
        out_specs=[
            pl.BlockSpec((None, S5_UW, S5_UW + 2 * S5_SPB), lambda c: (c, 0, 0)),
            pl.BlockSpec((None, 2 * S5_SPB, S5_UW), lambda c: (c, 0, 0)),
            pl.BlockSpec((None, 1, 2 * S5_SPB), lambda c: (c, 0, 0)),
        ],
        out_shape=[
            jax.ShapeDtypeStruct((S5_CB, S5_UW, S5_UW + 2 * S5_SPB), BF16),
            jax.ShapeDtypeStruct((S5_CB, 2 * S5_SPB, S5_UW), BF16),
            jax.ShapeDtypeStruct((S5_CB, 1, 2 * S5_SPB), F32),
        ],
        compiler_params=_cparams("parallel"),
        name="s5_prep",
    )(rows(a_re), rows(a_im), rows(ldt), cols(a_re), cols(a_im), cols(ldt),
      b_blockdiag(b_re), b_blockdiag(b_im), c_blockdiag(c_re), c_blockdiag(c_im))


def _s5_step(at_ref, h_re, h_im, s_re, s_im):
    a_re, a_im = at_ref[:, :S5_SPB], at_ref[:, S5_SPB:]
    return a_re * h_re - a_im * h_im + s_re, a_re * h_im + a_im * h_re + s_im


def _s5_prompt_kernel(x_ref, mw_ref, v_ref, at_ref, d_ref, y_ref, hfin_ref, hc_ref, s_ref, h_ref, *, tb, nr):
    r = pl.program_id(1)

    @pl.when(r == 0)
    def _():
        hc_ref[...] = jnp.zeros_like(hc_ref)

    u = jnp.concatenate([x_ref[pl.ds(s, tb, stride=S5_CHUNK), :] for s in range(S5_CHUNK)], axis=1)
    res = jnp.dot(u.astype(BF16), mw_ref[...], preferred_element_type=F32)
    s_ref[...] = res[:, S5_UW:]

    def body(i, h):
        h_re, h_im = h
        row = pl.ds(i, 1)
        h_ref[row, :S5_SPB] = h_re
        h_ref[row, S5_SPB:] = h_im
        return _s5_step(at_ref, h_re, h_im, s_ref[row, :S5_SPB], s_ref[row, S5_SPB:])

    h_re, h_im = lax.fori_loop(0, tb, body, (hc_ref[:, :S5_SPB], hc_ref[:, S5_SPB:]), unroll=8)
    hc_ref[:, :S5_SPB] = h_re
    hc_ref[:, S5_SPB:] = h_im

    d_row = jnp.concatenate([d_ref[...]] * S5_CHUNK, axis=1)
    y = (res[:, :S5_UW]
         + jnp.dot(h_ref[...].astype(BF16), v_ref[...], preferred_element_type=F32)
         + d_row * u)
    for t in range(S5_CHUNK):
        y_ref[pl.ds(t, tb, stride=S5_CHUNK), :] = y[:, t * LANES:(t + 1) * LANES]

    @pl.when(r == nr - 1)
    def _():
        hfin_ref[...] = hc_ref[...]


def _s5_prompt(x, mw, v, at, d, tb):
    n = x.shape[0]
    rows = tb * S5_CHUNK
    nr = n // rows
    return pl.pallas_call(
        functools.partial(_s5_prompt_kernel, tb=tb, nr=nr),
        grid=(S5_CB, nr),
        in_specs=[
            pl.BlockSpec((rows, LANES), lambda c, r: (r, c)),
            pl.BlockSpec((None, S5_UW, S5_UW + 2 * S5_SPB), lambda c, r: (c, 0, 0)),
            pl.BlockSpec((None, 2 * S5_SPB, S5_UW), lambda c, r: (c, 0, 0)),
            pl.BlockSpec((None, 1, 2 * S5_SPB), lambda c, r: (c, 0, 0)),
            pl.BlockSpec((None, 1, LANES), lambda c, r: (c, 0, 0)),
        ],
        out_specs=[
            pl.BlockSpec((rows, LANES), lambda c, r: (r, c)),
            pl.BlockSpec((None, 1, 2 * S5_SPB), lambda c, r: (c, 0, 0)),
        ],
        out_shape=[
            jax.ShapeDtypeStruct((n, D_MODEL), F32),
            jax.ShapeDtypeStruct((S5_CB, 1, 2 * S5_SPB), F32),
        ],
        scratch_shapes=[pltpu.VMEM((1, 2 * S5_SPB), F32), pltpu.VMEM((tb, 2 * S5_SPB), F32),
                        pltpu.VMEM((tb, 2 * S5_SPB), F32)],
        compiler_params=_cparams("parallel", "arbitrary"),
        name="s5_prompt",
    )(x, mw, v, at, d)


def _s5_sample_kernel(u_ref, hre_ref, him_ref, mw_ref, v_ref, at_ref, d_ref, y_ref, ore_ref, oim_ref, *, nblk):
    d_row = jnp.concatenate([d_ref[...]] * S5_CHUNK, axis=1)
    h_re, h_im = hre_ref[...], him_ref[...]
    for k in range(nblk):
        u = u_ref[k]
        res = jnp.dot(u.astype(BF16), mw_ref[...], preferred_element_type=F32)
        h = jnp.concatenate([h_re, h_im], axis=1).astype(BF16)
        y_ref[k] = res[:, :S5_UW] + jnp.dot(h, v_ref[...], preferred_element_type=F32) + d_row * u
        h_re, h_im = _s5_step(at_ref, h_re, h_im, res[:, S5_UW:S5_UW + S5_SPB], res[:, S5_UW + S5_SPB:])
    ore_ref[...] = h_re
    oim_ref[...] = h_im


def _s5_sample(u, h_re, h_im, mw, v, at, d):
    _, nblk, bsz, _ = u.shape
    h_spec = pl.BlockSpec((bsz, S5_SPB), lambda c: (0, c))
    u_spec = pl.BlockSpec((None, nblk, bsz, S5_UW), lambda c: (c, 0, 0, 0))
    return pl.pallas_call(
        functools.partial(_s5_sample_kernel, nblk=nblk),
        grid=(S5_CB,),
        in_specs=[
            u_spec, h_spec, h_spec,
            pl.BlockSpec((None, S5_UW, S5_UW + 2 * S5_SPB), lambda c: (c, 0, 0)),
            pl.BlockSpec((None, 2 * S5_SPB, S5_UW), lambda c: (c, 0, 0)),
            pl.BlockSpec((None, 1, 2 * S5_SPB), lambda c: (c, 0, 0)),
            pl.BlockSpec((None, 1, LANES), lambda c: (c, 0, 0)),
        ],
        out_specs=[u_spec, h_spec, h_spec],
        out_shape=[
            jax.ShapeDtypeStruct(u.shape, F32),
            jax.ShapeDtypeStruct(h_re.shape, F32),
            jax.ShapeDtypeStruct(h_im.shape, F32),
        ],
        compiler_params=_cparams("parallel"),
        name="s5_sample",
    )(u, h_re, h_im, mw, v, at, d)


def _glu_kernel(x_ref, y_ref, w_ref, g_ref, b_ref, o_ref):
    z = jax.nn.gelu(y_ref[...]).astype(BF16)
    for n in range(D_MODEL // COL_TILE):
        ns = slice(n * COL_TILE, (n + 1) * COL_TILE)
        gs = slice(D_MODEL + n * COL_TILE, D_MODEL + (n + 1) * COL_TILE)
        za = jnp.dot(z, w_ref[:, ns], preferred_element_type=F32)
        zb = jnp.dot(z, w_ref[:, gs], preferred_element_type=F32)
        o_ref[:, ns] = DN_ALPHA * x_ref[:, ns] + za * jax.nn.sigmoid(zb)
    _norm_rows(o_ref, g_ref, b_ref)


def _glu_ln(x, y, w_glu, g, b, tm):
    n = x.shape[0]
    return pl.pallas_call(
        _glu_kernel,
        grid=(n // tm,),
        in_specs=[
            pl.BlockSpec((tm, D_MODEL), lambda i: (i, 0)),
            pl.BlockSpec((tm, D_MODEL), lambda i: (i, 0)),
            _resident((D_MODEL, 2 * D_MODEL), lambda i: (0, 0)),
            _resident((1, D_MODEL), lambda i: (0, 0)),
            _resident((1, D_MODEL), lambda i: (0, 0)),
        ],
        out_specs=pl.BlockSpec((tm, D_MODEL), lambda i: (i, 0)),
        out_shape=jax.ShapeDtypeStruct((n, D_MODEL), F32),
        compiler_params=_cparams("parallel"),
        name="glu_ln",
    )(x, y, w_glu, g, b)


def _proj_kernel(x_ref, w_ref, o_ref, xb_ref):
    @pl.when(pl.program_id(1) == 0)
    def _():
        xb_ref[...] = x_ref[...].astype(BF16)

    o_ref[...] = jnp.dot(xb_ref[...], w_ref[...], preferred_element_type=F32).astype(o_ref.dtype)


def _proj(x, w, col0, ncols, out_dtype, tm, tn, row0=0, nrows=None):
    nrows = x.shape[0] if nrows is None else nrows
    assert row0 % tm == 0 and col0 % tn == 0
    rb, cb = row0 // tm, col0 // tn
    return pl.pallas_call(
        _proj_kernel,
        grid=(nrows // tm, ncols // tn),
        in_specs=[
            pl.BlockSpec((tm, D_MODEL), lambda i, j: (i + rb, 0)),
            pl.BlockSpec((D_MODEL, tn), lambda i, j: (0, j + cb)),
        ],
        out_specs=pl.BlockSpec((tm, tn), lambda i, j: (i, j)),
        out_shape=jax.ShapeDtypeStruct((nrows, ncols), out_dtype),
        scratch_shapes=[pltpu.VMEM((tm, D_MODEL), BF16)],
        compiler_params=_cparams("parallel", "arbitrary"),
        name="proj",
    )(x, w)


def _wo_kernel(x_ref, a_ref, w_ref, g_ref, b_ref, o_ref):
    a = a_ref[...]
    for n in range(D_MODEL // COL_TILE):
        ns = slice(n * COL_TILE, (n + 1) * COL_TILE)
        o_ref[:, ns] = DN_ALPHA * x_ref[:, ns] + jnp.dot(a, w_ref[:, ns], preferred_element_type=F32)
    _norm_rows(o_ref, g_ref, b_ref)


def _wo_ln(x, a, w_o, g, b, tm):
    n = x.shape[0]
    return pl.pallas_call(
        _wo_kernel,
        grid=(n // tm,),
        in_specs=[
            pl.BlockSpec((tm, D_MODEL), lambda i: (i, 0)),
            pl.BlockSpec((tm, D_MODEL), lambda i: (i, 0)),
            _resident((D_MODEL, D_MODEL), lambda i: (0, 0)),
            _resident((1, D_MODEL), lambda i: (0, 0)),
            _resident((1, D_MODEL), lambda i: (0, 0)),
        ],
        out_specs=pl.BlockSpec((tm, D_MODEL), lambda i: (i, 0)),
        out_shape=jax.ShapeDtypeStruct((n, D_MODEL), F32),
        compiler_params=_cparams("parallel"),
        name="wo_ln",
    )(x, a, w_o, g, b)


def _bias_kernel(tab_ref, o_ref, base_ref, *, q_off, nq, nk, scale, band):
    h = pl.program_id(0)
    ntab, nw = tab_ref.shape[1], base_ref.shape[1]

    @pl.when(h == 0)
    def _():
        off = lax.broadcasted_iota(jnp.int32, (ntab, nw), 1)
        off = jnp.where(off < nk, off, off - nw)
        idx = jnp.clip(q_off - off, -REL_CLIP, REL_CLIP) + REL_CLIP
        onehot = (idx == lax.broadcasted_iota(jnp.int32, (ntab, nw), 0)).astype(F32)
        base_ref[...] = scale * jnp.dot(tab_ref[...], onehot, precision=lax.Precision.HIGHEST,
                                        preferred_element_type=F32)

    bias = pltpu.roll(jnp.broadcast_to(base_ref[pl.ds(h, 1), :], (nq, nw)), 0, 1, stride=1, stride_axis=0)
    if band:
        row = lax.broadcasted_iota(jnp.int32, (nq, nw), 0)
        col = lax.broadcasted_iota(jnp.int32, (nq, nw), 1)
        first = row & ~(CHUNK - 1)
        bias = jnp.where((col >= first) & (col < first + BAND), bias, NEG_INF)
    o_ref[...] = bias[:, :nk]


def _rel_bias(table, q_off, nq, nk, scale=1.0, band=False):
    assert nk % LANES == 0
    ntab = table.shape[1]
    ntab_pad = -(-ntab // LANES) * LANES
    nw = -(-(nq + nk) // LANES) * LANES
    table = jnp.pad(table, ((0, 0), (0, ntab_pad - ntab)))
    return pl.pallas_call(
        functools.partial(_bias_kernel, q_off=q_off, nq=nq, nk=nk, scale=scale, band=band),
        grid=(N_HEADS,),
        in_specs=[_resident((N_HEADS, ntab_pad), lambda h: (0, 0))],
        out_specs=pl.BlockSpec((None, nq, nk), lambda h: (h, 0, 0)),
        out_shape=jax.ShapeDtypeStruct((N_HEADS, nq, nk), F32),
        scratch_shapes=[pltpu.VMEM((N_HEADS, nw), F32)],
        compiler_params=_cparams("arbitrary"),
        name="rel_bias",
    )(table)


ATT_TQ = BAND_PAST
ATT_GQ = 4 * CHUNK
ATT_GK = ATT_GQ + BAND_PAST
ATT_HB = 4
LOG2E = 1.4426950408889634


def _attn_prompt_kernel(q_ref, kp_ref, kc_ref, vp_ref, vc_ref, bias_ref, o_ref, s_ref, p_ref, l_ref):
    i = pl.program_id(0)
    ng = ATT_TQ // ATT_GQ
    dims = (((1,), (1,)), ((), ()))
    units = [(hh, g) for hh in range(ATT_HB) for g in range(ng)]

    def split(g):
        lo = g * ATT_GQ
        return lo, ATT_TQ - lo

    for u, (hh, g) in enumerate(units):
        cols = slice(hh * HEAD_DIM, (hh + 1) * HEAD_DIM)
        lo, n_prev = split(g)
        q = q_ref[lo:lo + ATT_GQ, cols]
        c = ATTN_SCALE * LOG2E
        s_prev = lax.dot_general(q, kp_ref[lo:, cols], dims, preferred_element_type=F32)
        s_ref[u, :, :n_prev] = s_prev * c + bias_ref[hh, :, :n_prev]
        s_cur = lax.dot_general(q, kc_ref[:ATT_GK - n_prev, cols], dims, preferred_element_type=F32)
        s_ref[u, :, n_prev:] = s_cur * c + bias_ref[hh, :, n_prev:]

    @pl.when(i == 0)
    def _():
        for u, (hh, g) in enumerate(units):
            _, n_prev = split(g)
            s_ref[u, :, :n_prev] = jnp.full((ATT_GQ, n_prev), NEG_INF, F32)

    for u, (hh, g) in enumerate(units):
        x = s_ref[u]
        e = jnp.exp2(x - jnp.max(x, axis=-1, keepdims=True))
        l_ref[u] = jnp.sum(e, axis=-1, keepdims=True)
        p_ref[u] = e.astype(BF16)

    for u, (hh, g) in enumerate(units):
        cols = slice(hh * HEAD_DIM, (hh + 1) * HEAD_DIM)
        lo, n_prev = split(g)
        pv = (jnp.dot(p_ref[u, :, :n_prev], vp_ref[lo:, cols], preferred_element_type=F32)
              + jnp.dot(p_ref[u, :, n_prev:], vc_ref[:ATT_GK - n_prev, cols], preferred_element_type=F32))
        o_ref[lo:lo + ATT_GQ, cols] = (pv / l_ref[u]).astype(o_ref.dtype)


def _attn_prompt(qkv, bias):
    n = qkv.shape[0]
    width = ATT_HB * HEAD_DIM
    nhb = D_MODEL // width
    n_units = ATT_HB * (ATT_TQ // ATT_GQ)

    def prev(i):
        return jnp.maximum(i - 1, 0)

    return pl.pallas_call(
        _attn_prompt_kernel,
        grid=(n // ATT_TQ, nhb),
        in_specs=[
            pl.BlockSpec((ATT_TQ, width), lambda i, h: (i, h)),
            pl.BlockSpec((ATT_TQ, width), lambda i, h: (prev(i), nhb + h)),
            pl.BlockSpec((ATT_TQ, width), lambda i, h: (i, nhb + h)),
            pl.BlockSpec((ATT_TQ, width), lambda i, h: (prev(i), 2 * nhb + h)),
            pl.BlockSpec((ATT_TQ, width), lambda i, h: (i, 2 * nhb + h)),
            pl.BlockSpec((ATT_HB, ATT_GQ, ATT_GK), lambda i, h: (h, 0, 0)),
        ],
        out_specs=pl.BlockSpec((ATT_TQ, width), lambda i, h: (i, h)),
        out_shape=jax.ShapeDtypeStruct((n, D_MODEL), BF16),
        scratch_shapes=[pltpu.VMEM((n_units, ATT_GQ, ATT_GK), F32), pltpu.VMEM((n_units, ATT_GQ, ATT_GK), BF16),
                        pltpu.VMEM((n_units, ATT_GQ, 1), F32)],
        compiler_params=_cparams("parallel", "parallel"),
        name="attn_prompt",
    )(qkv, qkv, qkv, qkv, qkv, bias)


def _attn_sample_kernel(q_ref, kn_ref, vn_ref, ck_ref, cv_ref, bias_ref, o_ref, *, ncache):
    s_len = q_ref.shape[0]
    for h in range(N_HEADS):
        cols = slice(h * HEAD_DIM, (h + 1) * HEAD_DIM)
        q = q_ref[:, cols]
        k_old = ck_ref[pl.ds(h, ncache, stride=N_HEADS), :].astype(BF16)
        v_old = cv_ref[pl.ds(h, ncache, stride=N_HEADS), :].astype(BF16)
        k_new, v_new = kn_ref[:, cols], vn_ref[:, cols]
        dims = (((1,), (1,)), ((), ()))
        s_old = lax.dot_general(q, k_old, dims, preferred_element_type=F32) * ATTN_SCALE + bias_ref[h, :, :ncache]
        s_new = (lax.dot_general(q, k_new, dims, preferred_element_type=F32) * ATTN_SCALE
                 + bias_ref[h, :, ncache:ncache + s_len])
        m = jnp.maximum(jnp.max(s_old, axis=-1, keepdims=True), jnp.max(s_new, axis=-1, keepdims=True))
        e_old, e_new = jnp.exp(s_old - m), jnp.exp(s_new - m)
        den = jnp.sum(e_old, axis=-1, keepdims=True) + jnp.sum(e_new, axis=-1, keepdims=True)
        pv = (jnp.dot(e_old.astype(BF16), v_old, preferred_element_type=F32)
              + jnp.dot(e_new.astype(BF16), v_new, preferred_element_type=F32))
        o_ref[:, cols] = (pv / den).astype(o_ref.dtype)


def _attn_sample(qkv, cache_k, cache_v, bias, bsz, s_len):
    ncache = cache_k.shape[1] // N_HEADS
    return pl.pallas_call(
        functools.partial(_attn_sample_kernel, ncache=ncache),
        grid=(bsz,),
        in_specs=[
            pl.BlockSpec((s_len, D_MODEL), lambda b: (b, 0)),
            pl.BlockSpec((s_len, D_MODEL), lambda b: (b, 1)),
            pl.BlockSpec((s_len, D_MODEL), lambda b: (b, 2)),
            pl.BlockSpec((None, ncache * N_HEADS, HEAD_DIM), lambda b: (b, 0, 0)),
            pl.BlockSpec((None, ncache * N_HEADS, HEAD_DIM), lambda b: (b, 0, 0)),
            pl.BlockSpec(bias.shape, lambda b: (0, 0, 0)),
        ],
        out_specs=pl.BlockSpec((s_len, D_MODEL), lambda b: (b, 0)),
        out_shape=jax.ShapeDtypeStruct((bsz * s_len, D_MODEL), BF16),
        compiler_params=_cparams("parallel"),
        name="attn_sample",
    )(qkv, qkv, qkv, cache_k, cache_v, bias)


def kernel(x_prompt, x_sample, state_s5_re, state_s5_im, cache_k, cache_v, p_prompt, p_sample, ffn1_w_in, ffn1_w_out, ffn2_w_in, ffn2_w_out, ln_g, ln_b, ple_w_proj, ple_w_gate, s5_a_re, s5_a_im, s5_log_dt, s5_b_re, s5_b_im, s5_c_re, s5_c_im, s5_d, s5_w_glu, attn_w_qkv, attn_w_o, attn_rel_bias):
    bsz_p, seq, _ = x_prompt.shape
    bsz_s, s_len, _ = x_sample.shape
    assert bsz_p == 1 and s_len % S5_CHUNK == 0 and seq % (S5_CHUNK * 512) == 0
    n_p, n_s = bsz_p * seq, bsz_s * s_len
    tm_glu, tm_s = 512, n_s
    tm_ffn, tm_proj, tn_wide = 1024, 1024, 1024

    xp = x_prompt.reshape(n_p, D_MODEL)
    xs = x_sample.reshape(n_s, D_MODEL)
    pp = p_prompt.reshape(DEPTH, n_p, PLE_DIM)
    ps = p_sample.reshape(DEPTH, n_s, PLE_DIM)
    w1_in, w1_out = ffn1_w_in.astype(BF16), ffn1_w_out.astype(BF16)
    w2_in, w2_out = ffn2_w_in.astype(BF16), ffn2_w_out.astype(BF16)
    w_proj, w_gate = ple_w_proj.astype(BF16), ple_w_gate.astype(BF16)
    w_glu, w_qkv, w_o = s5_w_glu.astype(BF16), attn_w_qkv.astype(BF16), attn_w_o.astype(BF16)

    def norm(i, slot):
        return ln_g[i, slot].reshape(1, D_MODEL), ln_b[i, slot].reshape(1, D_MODEL)

    outs = {}
    for i in range(DEPTH):
        xp = _ffn_ln(xp, w1_in, w1_out, i, *norm(i, 0), tm_ffn)
        xs = _ffn_ln(xs, w1_in, w1_out, i, *norm(i, 0), tm_s)
        if i % N_MIXERS == 0:
            mw, v, at = _s5_prep(s5_a_re, s5_a_im, s5_log_dt, s5_b_re, s5_b_im, s5_c_re, s5_c_im)
            d = s5_d.reshape(S5_CB, 1, LANES)
            y, h_fin = _s5_prompt(xp, mw, v, at, d, tb=512)
            xp = _glu_ln(xp, y, w_glu, *norm(i, 1), tm_glu)
            outs["s5_p"] = (h_fin[:, 0, :S5_SPB].reshape(bsz_p, S5_GROUPS, S5_STATE),
                            h_fin[:, 0, S5_SPB:].reshape(bsz_p, S5_GROUPS, S5_STATE))
            nblk = s_len // S5_CHUNK
            u = (xs.reshape(bsz_s, nblk, S5_CHUNK, S5_CB, LANES).transpose(3, 1, 0, 2, 4)
                 .reshape(S5_CB, nblk, bsz_s, S5_UW))
            ys, hs_re, hs_im = _s5_sample(u, state_s5_re.reshape(bsz_s, -1), state_s5_im.reshape(bsz_s, -1),
                                          mw, v, at, d)
            ys = (ys.reshape(S5_CB, nblk, bsz_s, S5_CHUNK, LANES).transpose(2, 1, 3, 0, 4)
                  .reshape(n_s, D_MODEL))
            xs = _glu_ln(xs, ys, w_glu, *norm(i, 1), tm_s)
            outs["s5_s"] = (hs_re.reshape(bsz_s, S5_GROUPS, S5_STATE), hs_im.reshape(bsz_s, S5_GROUPS, S5_STATE))
        else:
            rows = min(BAND_PAST, seq)
            assert seq % ATT_TQ == 0
            qkv_p = _proj(xp, w_qkv, 0, 3 * D_MODEL, BF16, tm_proj, tn_wide)
            kv_tail = _proj(xp, w_qkv, D_MODEL, 2 * D_MODEL, F32, rows, tn_wide, row0=n_p - rows, nrows=rows)
            bias_p = _rel_bias(attn_rel_bias, BAND_PAST, ATT_GQ, ATT_GK, scale=LOG2E, band=True)
            att_p = _attn_prompt(qkv_p, bias_p)
            xp = _wo_ln(xp, att_p, w_o, *norm(i, 1), tm_ffn)
            outs["kv_p"] = (kv_tail[:, :D_MODEL].reshape(bsz_p, rows, N_HEADS, HEAD_DIM),
                            kv_tail[:, D_MODEL:].reshape(bsz_p, rows, N_HEADS, HEAD_DIM))

            ncache = cache_k.shape[1]
            qkv_s = _proj(xs, w_qkv, 0, 3 * D_MODEL, BF16, tm_s, tn_wide)
            kv_s = _proj(xs, w_qkv, D_MODEL, 2 * D_MODEL, F32, tm_s, tn_wide)
            nk_pad = -(-(ncache + s_len) // LANES) * LANES
            bias_s = _rel_bias(attn_rel_bias, ncache, s_len, nk_pad)
            att_s = _attn_sample(qkv_s, cache_k.reshape(bsz_s, ncache * N_HEADS, HEAD_DIM),
                                 cache_v.reshape(bsz_s, ncache * N_HEADS, HEAD_DIM), bias_s, bsz_s, s_len)
            xs = _wo_ln(xs, att_s, w_o, *norm(i, 1), tm_s)
            outs["kv_s"] = (kv_s[:, :D_MODEL].reshape(bsz_s, s_len, N_HEADS, HEAD_DIM),
                            kv_s[:, D_MODEL:].reshape(bsz_s, s_len, N_HEADS, HEAD_DIM))
        xp = _ffn_ln(xp, w2_in, w2_out, i, *norm(i, 2), tm_ffn)
        xs = _ffn_ln(xs, w2_in, w2_out, i, *norm(i, 2), tm_s)
        xp = _ple_ln(xp, pp, w_proj, w_gate, i, *norm(i, 3), tm_ffn)
        xs = _ple_ln(xs, ps, w_proj, w_gate, i, *norm(i, 3), tm_s)

    return (xp.reshape(bsz_p, seq, D_MODEL), xs.reshape(bsz_s, s_len, D_MODEL),
            *outs["s5_p"], *outs["kv_p"], *outs["s5_s"], *outs["kv_s"])
```

```python
import functools

import jax
import jax.numpy as jnp
from jax import lax
from jax.experimental import pallas as pl
from jax.experimental.pallas import tpu as pltpu

F32 = jnp.float32
BF16 = jnp.bfloat16

D_MODEL = 2048
DEPTH = 2
N_MIXERS = 2
CHUNK = 64
S5_GROUP = 16
S5_GROUPS = D_MODEL // S5_GROUP
S5_STATE = 64
N_HEADS = 16
HEAD_DIM = D_MODEL // N_HEADS
PAST_CHUNKS = 8
BAND_PAST = PAST_CHUNKS * CHUNK
BAND = BAND_PAST + CHUNK
REL_CLIP = 128
ATTN_SCALE = HEAD_DIM ** -0.5
NEG_INF = -1e30
D_FF = 5632
PLE_DIM = 256
DN_ALPHA = (2 * DEPTH) ** 0.25
LN_EPS = 1e-5

LANES = 128
MXU_DIM = 256
VMEM_LIMIT = 60 * 1024 * 1024

TF = 512
FFN_SUB = MXU_DIM
FFN_OUT = 512
LN_ROWS = 128
COL_TILE = 512

S5_CB = D_MODEL // LANES
S5_GPB = LANES // S5_GROUP
S5_SPB = S5_GPB * S5_STATE
S5_CHUNK = 8
S5_UW = S5_CHUNK * LANES


def _cparams(*sem):
    return pltpu.CompilerParams(dimension_semantics=sem, vmem_limit_bytes=VMEM_LIMIT)


def _layer_norm(v, g, b, eps=LN_EPS):
    mu = jnp.mean(v, axis=-1, keepdims=True)
    d = v - mu
    var = jnp.mean(d * d, axis=-1, keepdims=True)
    return d * lax.rsqrt(var + eps) * g + b


def _norm_rows(o_ref, g_ref, b_ref, eps=LN_EPS):
    rows = min(LN_ROWS, o_ref.shape[0])
    for r in range(o_ref.shape[0] // rows):
        rs = slice(r * rows, (r + 1) * rows)
        o_ref[rs, :] = _layer_norm(o_ref[rs, :], g_ref[...], b_ref[...], eps)


def _resident(block_shape, index_map):
    return pl.BlockSpec(block_shape, index_map, pipeline_mode=pl.Buffered(1))


def _ffn_kernel(*refs, nj, cast_next):
    if cast_next:
        x_ref, wg_ref, wu_ref, wo_ref, g_ref, b_ref, nin_ref, nout_ref, o_ref, cin_ref, cout_ref, xb_ref = refs
        cin_ref[...] = nin_ref[...].astype(BF16)
        cout_ref[...] = nout_ref[...].astype(BF16)
    else:
        x_ref, wg_ref, wu_ref, wo_ref, g_ref, b_ref, o_ref, xb_ref = refs
    j = pl.program_id(1)

    @pl.when(j == 0)
    def _():
        x = x_ref[...]
        xb_ref[...] = x.astype(BF16)
        o_ref[...] = (2.0 * DN_ALPHA) * x

    xb = xb_ref[...]
    acts = []
    for h in range(TF // FFN_SUB):
        cs = slice(h * FFN_SUB, (h + 1) * FFN_SUB)
        gate = jnp.dot(xb, wg_ref[:, cs], preferred_element_type=F32)
        up = jnp.dot(xb, wu_ref[:, cs], preferred_element_type=F32)
        acts.append((gate * jax.nn.sigmoid(gate) * up).astype(BF16))
    act = jnp.concatenate(acts, axis=1)
    for n in range(D_MODEL // FFN_OUT):
        ns = slice(n * FFN_OUT, (n + 1) * FFN_OUT)
        o_ref[:, ns] += jnp.dot(act, wo_ref[:, ns], preferred_element_type=F32)

    @pl.when(j == nj - 1)
    def _():
        _norm_rows(o_ref, g_ref, b_ref, eps=4.0 * LN_EPS)


def _ffn_ln(x, w_in, w_out, g, b, tm, next_w=None):
    n = x.shape[0]
    ni, nj = n // tm, D_FF // TF
    in_specs = [
        pl.BlockSpec((tm, D_MODEL), lambda i, j: (i, 0)),
        pl.BlockSpec((D_MODEL, TF), lambda i, j: (0, j)),
        pl.BlockSpec((D_MODEL, TF), lambda i, j: (0, j + nj)),
        pl.BlockSpec((TF, D_MODEL), lambda i, j: (j, 0)),
        pl.BlockSpec((1, D_MODEL), lambda i, j: (0, 0)),
        pl.BlockSpec((1, D_MODEL), lambda i, j: (0, 0)),
    ]
    out_specs = [pl.BlockSpec((tm, D_MODEL), lambda i, j: (i, 0))]
    out_shape = [jax.ShapeDtypeStruct((n, D_MODEL), F32)]
    args = [x, w_in, w_in, w_out, g, b]
    if next_w is not None:
        nw_in, nw_out, layer = next_w
        assert D_MODEL % ni == 0 and (2 * D_FF) % nj == 0 and D_FF % nj == 0
        in_blk = (D_MODEL // ni, 2 * D_FF // nj)
        out_blk = (D_FF // nj, D_MODEL // ni)
        in_specs += [pl.BlockSpec((None,) + in_blk, lambda i, j: (layer, i, j)),
                     pl.BlockSpec((None,) + out_blk, lambda i, j: (layer, j, i))]
        out_specs += [pl.BlockSpec(in_blk, lambda i, j: (i, j)), pl.BlockSpec(out_blk, lambda i, j: (j, i))]
        out_shape += [jax.ShapeDtypeStruct(nw_in.shape[1:], BF16), jax.ShapeDtypeStruct(nw_out.shape[1:], BF16)]
        args += [nw_in, nw_out]
    res = pl.pallas_call(
        functools.partial(_ffn_kernel, nj=nj, cast_next=next_w is not None),
        grid=(ni, nj),
        in_specs=in_specs,
        out_specs=out_specs,
        out_shape=out_shape,
        scratch_shapes=[pltpu.VMEM((tm, D_MODEL), BF16)],
        compiler_params=_cparams("parallel", "arbitrary"),
        name="ffn_ln",
    )(*args)
    return res[0] if next_w is None else res


def _ple_kernel(x_ref, p_ref, wp_ref, wg_ref, g_ref, b_ref, o_ref):
    xb = x_ref[...].astype(BF16)
    pb = p_ref[...].astype(BF16)
    for n in range(D_MODEL // COL_TILE):
        ns = slice(n * COL_TILE, (n + 1) * COL_TILE)
        proj = jnp.dot(pb, wp_ref[:, ns], preferred_element_type=F32)
        gate = jnp.dot(xb, wg_ref[:, ns], preferred_element_type=F32)
        o_ref[:, ns] = DN_ALPHA * x_ref[:, ns] + proj * jax.nn.sigmoid(gate)
    _norm_rows(o_ref, g_ref, b_ref)


def _ple_ln(x, p, w_proj, w_gate, layer, g, b, tm):
    n = x.shape[0]
    return pl.pallas_call(
        _ple_kernel,
        grid=(n // tm,),
        in_specs=[
            pl.BlockSpec((tm, D_MODEL), lambda i: (i, 0)),
            pl.BlockSpec((None, tm, PLE_DIM), lambda i: (layer, i, 0)),
            _resident((None, PLE_DIM, D_MODEL), lambda i: (layer, 0, 0)),
            _resident((None, D_MODEL, D_MODEL), lambda i: (layer, 0, 0)),
            _resident((1, D_MODEL), lambda i: (0, 0)),
            _resident((1, D_MODEL), lambda i: (0, 0)),
        ],
        out_specs=pl.BlockSpec((tm, D_MODEL), lambda i: (i, 0)),
        out_shape=jax.ShapeDtypeStruct((n, D_MODEL), F32),
        compiler_params=_cparams("parallel"),
        name="ple_ln",
    )(x, p, w_proj, w_gate, g, b)


def _s5_discretize(ar, ai, log_dt):
    dt = jnp.exp(log_dt)
    mag = jnp.exp(ar * dt)
    ab_re, ab_im = mag * jnp.cos(ai * dt), mag * jnp.sin(ai * dt)
    den = ar * ar + ai * ai
    nr, ni = ab_re - 1.0, ab_im
    return ab_re, ab_im, (nr * ar + ni * ai) / den, (ni * ar - nr * ai) / den


def _s5_prep_kernel(ar_r, ai_r, ldt_r, ar_c, ai_c, ldt_c, bre_ref, bim_ref, cre_ref, cim_ref,
                    mw_ref, v_ref, at_ref):
    ab_re, ab_im, cf_re, cf_im = _s5_discretize(ar_r[...], ai_r[...], ldt_r[...])
    b_re, b_im = bre_ref[...], bim_ref[...]
    c_re, c_im = cre_ref[...], cim_ref[...]
    p_re = cf_re * b_re - cf_im * b_im
    p_im = cf_re * b_im + cf_im * b_re
    ak_re, ak_im = jnp.ones_like(ab_re), jnp.zeros_like(ab_im)
    taps, pows = [], []
    for _ in range(S5_CHUNK):
        pows.append((p_re, p_im))
        taps.append(jnp.dot(p_re, c_re, precision=lax.Precision.HIGHEST, preferred_element_type=F32)
                    - jnp.dot(p_im, c_im, precision=lax.Precision.HIGHEST, preferred_element_type=F32))
        p_re, p_im = ab_re * p_re - ab_im * p_im, ab_re * p_im + ab_im * p_re
        ak_re, ak_im = ab_re * ak_re - ab_im * ak_im, ab_re * ak_im + ab_im * ak_re
    at_ref[:, :S5_SPB] = ak_re
    at_ref[:, S5_SPB:] = ak_im

    zero = jnp.zeros((LANES, LANES), BF16)
    for s in range(S5_CHUNK):
        rows = slice(s * LANES, (s + 1) * LANES)
        for t in range(S5_CHUNK):
            mw_ref[rows, t * LANES:(t + 1) * LANES] = taps[t - s].astype(BF16) if t >= s else zero
        w_re, w_im = pows[S5_CHUNK - 1 - s]
        mw_ref[rows, S5_UW:S5_UW + S5_SPB] = w_re.astype(BF16)
        mw_ref[rows, S5_UW + S5_SPB:] = w_im.astype(BF16)

    ac_re, ac_im, _, _ = _s5_discretize(ar_c[...], ai_c[...], ldt_c[...])
    ck_re, ck_im = ac_re, ac_im
    for t in range(S5_CHUNK):
        cols = slice(t * LANES, (t + 1) * LANES)
        v_ref[:S5_SPB, cols] = (ck_re * c_re - ck_im * c_im).astype(BF16)
        v_ref[S5_SPB:, cols] = (-(ck_im * c_re + ck_re * c_im)).astype(BF16)
        ck_re, ck_im = ck_re * ac_re - ck_im * ac_im, ck_re * ac_im + ck_im * ac_re


def _s5_prep(a_re, a_im, log_dt, b_re, b_im, c_re, c_im):
    eye = jnp.eye(S5_GPB, dtype=F32)

    def rows(v):
        return v.reshape(S5_CB, 1, S5_SPB)

    def cols(v):
        return v.reshape(S5_CB, S5_SPB, 1)

    def b_blockdiag(w):
        w4 = w.reshape(S5_CB, S5_GPB, S5_STATE, S5_GROUP).transpose(0, 1, 3, 2)
        return (w4[:, :, :, None, :] * eye[None, :, None, :, None]).reshape(S5_CB, LANES, S5_SPB)

    def c_blockdiag(w):
        w4 = w.reshape(S5_CB, S5_GPB, S5_GROUP, S5_STATE).transpose(0, 1, 3, 2)
        return (w4[:, :, :, None, :] * eye[None, :, None, :, None]).reshape(S5_CB, S5_SPB, LANES)

    ldt = jnp.repeat(log_dt, S5_STATE)
    row_spec = pl.BlockSpec((None, 1, S5_SPB), lambda c: (c, 0, 0))
    col_spec = pl.BlockSpec((None, S5_SPB, 1), lambda c: (c, 0, 0))
    b_spec = pl.BlockSpec((None, LANES, S5_SPB), lambda c: (c, 0, 0))
    c_spec = pl.BlockSpec((None, S5_SPB, LANES), lambda c: (c, 0, 0))
    return pl.pallas_call(
        _s5_prep_kernel,
        grid=(S5_CB,),
        in_specs=[row_spec, row_spec, row_spec, col_spec, col_spec, col_spec, b_spec, b_spec, c_spec, c_spec],
        out_specs=[
            pl.BlockSpec((None, S5_UW, S5_UW + 2 * S5_SPB), lambda c: (c, 0, 0)),
            pl.BlockSpec((None, 2 * S5_SPB, S5_UW), lambda c: (c, 0, 0)),
            pl.BlockSpec((None, 1, 2 * S5_SPB), lambda c: (c, 0, 0)),
        ],
        out_shape=[
            jax.ShapeDtypeStruct((S5_CB, S5_UW, S5_UW + 2 * S5_SPB), BF16),
            jax.ShapeDtypeStruct((S5_CB, 2 * S5_SPB, S5_UW), BF16),
            jax.ShapeDtypeStruct((S5_CB, 1, 2 * S5_SPB), F32),
        ],
        compiler_params=_cparams("parallel"),
        name="s5_prep",
    )(rows(a_re), rows(a_im), rows(ldt), cols(a_re), cols(a_im), cols(ldt),
      b_blockdiag(b_re), b_blockdiag(b_im), c_blockdiag(c_re), c_blockdiag(c_im))


def _s5_step(at_ref, h_re, h_im, s_re, s_im):
    a_re, a_im = at_ref[:, :S5_SPB], at_ref[:, S5_SPB:]
    return a_re * h_re - a_im * h_im + s_re, a_re * h_im + a_im * h_re + s_im


def _s5_prompt_kernel(x_ref, mw_ref, v_ref, at_ref, d_ref, y_ref, hfin_ref, hc_ref, s_ref, h_ref, *, tb, nr):
    r = pl.program_id(1)

    @pl.when(r == 0)
    def _():
        hc_ref[...] = jnp.zeros_like(hc_ref)

    u = jnp.concatenate([x_ref[pl.ds(s, tb, stride=S5_CHUNK), :] for s in range(S5_CHUNK)], axis=1)
    ub = u.astype(BF16)
    s_ref[...] = jnp.dot(ub, mw_ref[:, S5_UW:], preferred_element_type=F32)

    def body(i, h):
        h_re, h_im = h
        row = pl.ds(i, 1)
        h_ref[row, :S5_SPB] = h_re
        h_ref[row, S5_SPB:] = h_im
        return _s5_step(at_ref, h_re, h_im, s_ref[row, :S5_SPB], s_ref[row, S5_SPB:])

    h_re, h_im = lax.fori_loop(0, tb, body, (hc_ref[:, :S5_SPB], hc_ref[:, S5_SPB:]), unroll=8)
    hc_ref[:, :S5_SPB] = h_re
    hc_ref[:, S5_SPB:] = h_im

    hb = h_ref[...].astype(BF16)
    steps = MXU_DIM // LANES
    for c in range(S5_UW // MXU_DIM):
        cols = slice(c * MXU_DIM, (c + 1) * MXU_DIM)
        k = (c + 1) * MXU_DIM
        y = (jnp.dot(ub[:, :k], mw_ref[:k, cols], preferred_element_type=F32)
             + jnp.dot(hb, v_ref[:, cols], preferred_element_type=F32)
             + jnp.concatenate([d_ref[...]] * steps, axis=1) * u[:, cols])
        for t in range(steps):
            y_ref[pl.ds(c * steps + t, tb, stride=S5_CHUNK), :] = y[:, t * LANES:(t + 1) * LANES]

    @pl.when(r == nr - 1)
    def _():
        hfin_ref[...] = hc_ref[...]


def _s5_prompt(x, mw, v, at, d, tb):
    n = x.shape[0]
    rows = tb * S5_CHUNK
    nr = n // rows
    return pl.pallas_call(
        functools.partial(_s5_prompt_kernel, tb=tb, nr=nr),
        grid=(S5_CB, nr),
        in_specs=[
            pl.BlockSpec((rows, LANES), lambda c, r: (r, c)),
            pl.BlockSpec((None, S5_UW, S5_UW + 2 * S5_SPB), lambda c, r: (c, 0, 0)),
            pl.BlockSpec((None, 2 * S5_SPB, S5_UW), lambda c, r: (c, 0, 0)),
            pl.BlockSpec((None, 1, 2 * S5_SPB), lambda c, r: (c, 0, 0)),
            pl.BlockSpec((None, 1, LANES), lambda c, r: (c, 0, 0)),
        ],
        out_specs=[
            pl.BlockSpec((rows, LANES), lambda c, r: (r, c)),
            pl.BlockSpec((None, 1, 2 * S5_SPB), lambda c, r: (c, 0, 0)),
        ],
        out_shape=[
            jax.ShapeDtypeStruct((n, D_MODEL), F32),
            jax.ShapeDtypeStruct((S5_CB, 1, 2 * S5_SPB), F32),
        ],
        scratch_shapes=[pltpu.VMEM((1, 2 * S5_SPB), F32), pltpu.VMEM((tb, 2 * S5_SPB), F32),
                        pltpu.VMEM((tb, 2 * S5_SPB), F32)],
        compiler_params=_cparams("parallel", "arbitrary"),
        name="s5_prompt",
    )(x, mw, v, at, d)


def _s5_sample_kernel(u_ref, hre_ref, him_ref, mw_ref, v_ref, at_ref, d_ref, y_ref, ore_ref, oim_ref, *, nblk):
    d_row = jnp.concatenate([d_ref[...]] * S5_CHUNK, axis=1)
    h_re, h_im = hre_ref[...], him_ref[...]
    for k in range(nblk):
        u = u_ref[k]
        res = jnp.dot(u.astype(BF16), mw_ref[...], preferred_element_type=F32)
        h = jnp.concatenate([h_re, h_im], axis=1).astype(BF16)
        y_ref[k] = res[:, :S5_UW] + jnp.dot(h, v_ref[...], preferred_element_type=F32) + d_row * u
        h_re, h_im = _s5_step(at_ref, h_re, h_im, res[:, S5_UW:S5_UW + S5_SPB], res[:, S5_UW + S5_SPB:])
    ore_ref[...] = h_re
    oim_ref[...] = h_im


def _s5_sample(u, h_re, h_im, mw, v, at, d):
    _, nblk, bsz, _ = u.shape
    h_spec = pl.BlockSpec((bsz, S5_SPB), lambda c: (0, c))
    u_spec = pl.BlockSpec((None, nblk, bsz, S5_UW), lambda c: (c, 0, 0, 0))
    return pl.pallas_call(
        functools.partial(_s5_sample_kernel, nblk=nblk),
        grid=(S5_CB,),
        in_specs=[
            u_spec, h_spec, h_spec,
            pl.BlockSpec((None, S5_UW, S5_UW + 2 * S5_SPB), lambda c: (c, 0, 0)),
            pl.BlockSpec((None, 2 * S5_SPB, S5_UW), lambda c: (c, 0, 0)),
            pl.BlockSpec((None, 1, 2 * S5_SPB), lambda c: (c, 0, 0)),
            pl.BlockSpec((None, 1, LANES), lambda c: (c, 0, 0)),
        ],
        out_specs=[u_spec, h_spec, h_spec],
        out_shape=[
            jax.ShapeDtypeStruct(u.shape, F32),
            jax.ShapeDtypeStruct(h_re.shape, F32),
            jax.ShapeDtypeStruct(h_im.shape, F32),
        ],
        compiler_params=_cparams("parallel"),
        name="s5_sample",
    )(u, h_re, h_im, mw, v, at, d)


def _glu_kernel(x_ref, y_ref, w_ref, g_ref, b_ref, o_ref):
    z = jax.nn.gelu(y_ref[...]).astype(BF16)
    for n in range(D_MODEL // COL_TILE):
        ns = slice(n * COL_TILE, (n + 1) * COL_TILE)
        gs = slice(D_MODEL + n * COL_TILE, D_MODEL + (n + 1) * COL_TILE)
        za = jnp.dot(z, w_ref[:, ns], preferred_element_type=F32)
        zb = jnp.dot(z, w_ref[:, gs], preferred_element_type=F32)
        o_ref[:, ns] = DN_ALPHA * x_ref[:, ns] + za * jax.nn.sigmoid(zb)
    _norm_rows(o_ref, g_ref, b_ref)


def _glu_ln(x, y, w_glu, g, b, tm):
    n = x.shape[0]
    return pl.pallas_call(
        _glu_kernel,
        grid=(n // tm,),
        in_specs=[
            pl.BlockSpec((tm, D_MODEL), lambda i: (i, 0)),
            pl.BlockSpec((tm, D_MODEL), lambda i: (i, 0)),
            _resident((D_MODEL, 2 * D_MODEL), lambda i: (0, 0)),
            _resident((1, D_MODEL), lambda i: (0, 0)),
            _resident((1, D_MODEL), lambda i: (0, 0)),
        ],
        out_specs=pl.BlockSpec((tm, D_MODEL), lambda i: (i, 0)),
        out_shape=jax.ShapeDtypeStruct((n, D_MODEL), F32),
        compiler_params=_cparams("parallel"),
        name="glu_ln",
    )(x, y, w_glu, g, b)


def _proj_kernel(x_ref, w_ref, o_ref, xb_ref):
    @pl.when(pl.program_id(1) == 0)
    def _():
        xb_ref[...] = x_ref[...].astype(BF16)

    o_ref[...] = jnp.dot(xb_ref[...], w_ref[...], preferred_element_type=F32).astype(o_ref.dtype)


def _proj(x, w, col0, ncols, out_dtype, tm, tn, row0=0, nrows=None):
    nrows = x.shape[0] if nrows is None else nrows
    assert row0 % tm == 0 and col0 % tn == 0
    rb, cb = row0 // tm, col0 // tn
    return pl.pallas_call(
        _proj_kernel,
        grid=(nrows // tm, ncols // tn),
        in_specs=[
            pl.BlockSpec((tm, D_MODEL), lambda i, j: (i + rb, 0)),
            pl.BlockSpec((D_MODEL, tn), lambda i, j: (0, j + cb)),
        ],
        out_specs=pl.BlockSpec((tm, tn), lambda i, j: (i, j)),
        out_shape=jax.ShapeDtypeStruct((nrows, ncols), out_dtype),
        scratch_shapes=[pltpu.VMEM((tm, D_MODEL), BF16)],
        compiler_params=_cparams("parallel", "arbitrary"),
        name="proj",
    )(x, w)


def _wo_kernel(x_ref, a_ref, w_ref, g_ref, b_ref, o_ref):
    a = a_ref[...]
    for n in range(D_MODEL // COL_TILE):
        ns = slice(n * COL_TILE, (n + 1) * COL_TILE)
        o_ref[:, ns] = DN_ALPHA * x_ref[:, ns] + jnp.dot(a, w_ref[:, ns], preferred_element_type=F32)
    _norm_rows(o_ref, g_ref, b_ref)


def _wo_ln(x, a, w_o, g, b, tm):
    n = x.shape[0]
    return pl.pallas_call(
        _wo_kernel,
        grid=(n // tm,),
        in_specs=[
            pl.BlockSpec((tm, D_MODEL), lambda i: (i, 0)),
            pl.BlockSpec((tm, D_MODEL), lambda i: (i, 0)),
            _resident((D_MODEL, D_MODEL), lambda i: (0, 0)),
            _resident((1, D_MODEL), lambda i: (0, 0)),
            _resident((1, D_MODEL), lambda i: (0, 0)),
        ],
        out_specs=pl.BlockSpec((tm, D_MODEL), lambda i: (i, 0)),
        out_shape=jax.ShapeDtypeStruct((n, D_MODEL), F32),
        compiler_params=_cparams("parallel"),
        name="wo_ln",
    )(x, a, w_o, g, b)


def _bias_kernel(tab_ref, o_ref, base_ref, *, q_off, nq, nk, scale, band):
    h = pl.program_id(0)
    ntab, nw = tab_ref.shape[1], base_ref.shape[1]

    @pl.when(h == 0)
    def _():
        off = lax.broadcasted_iota(jnp.int32, (ntab, nw), 1)
        off = jnp.where(off < nk, off, off - nw)
        idx = jnp.clip(q_off - off, -REL_CLIP, REL_CLIP) + REL_CLIP
        onehot = (idx == lax.broadcasted_iota(jnp.int32, (ntab, nw), 0)).astype(F32)
        base_ref[...] = scale * jnp.dot(tab_ref[...], onehot, precision=lax.Precision.HIGHEST,
                                        preferred_element_type=F32)

    bias = pltpu.roll(jnp.broadcast_to(base_ref[pl.ds(h, 1), :], (nq, nw)), 0, 1, stride=1, stride_axis=0)
    if band:
        row = lax.broadcasted_iota(jnp.int32, (nq, nw), 0)
        col = lax.broadcasted_iota(jnp.int32, (nq, nw), 1)
        first = row & ~(CHUNK - 1)
        bias = jnp.where((col >= first) & (col < first + BAND), bias, NEG_INF)
    o_ref[...] = bias[:, :nk]


def _rel_bias(table, q_off, nq, nk, scale=1.0, band=False):
    assert nk % LANES == 0
    ntab = table.shape[1]
    ntab_pad = -(-ntab // LANES) * LANES
    nw = -(-(nq + nk) // LANES) * LANES
    table = jnp.pad(table, ((0, 0), (0, ntab_pad - ntab)))
    return pl.pallas_call(
        functools.partial(_bias_kernel, q_off=q_off, nq=nq, nk=nk, scale=scale, band=band),
        grid=(N_HEADS,),
        in_specs=[_resident((N_HEADS, ntab_pad), lambda h: (0, 0))],
        out_specs=pl.BlockSpec((None, nq, nk), lambda h: (h, 0, 0)),
        out_shape=jax.ShapeDtypeStruct((N_HEADS, nq, nk), F32),
        scratch_shapes=[pltpu.VMEM((N_HEADS, nw), F32)],
        compiler_params=_cparams("arbitrary"),
        name="rel_bias",
    )(table)


ATT_TQ = BAND_PAST
ATT_GQ = 4 * CHUNK
ATT_GK = ATT_GQ + BAND_PAST
ATT_HB = 4
LOG2E = 1.4426950408889634


def _attn_prompt_kernel(q_ref, kp_ref, kc_ref, vp_ref, vc_ref, bias_ref, o_ref, s_ref, p_ref, l_ref):
    i = pl.program_id(0)
    ng = ATT_TQ // ATT_GQ
    dims = (((1,), (1,)), ((), ()))
    units = [(hh, g) for hh in range(ATT_HB) for g in range(ng)]

    def split(g):
        lo = g * ATT_GQ
        return lo, ATT_TQ - lo

    for u, (hh, g) in enumerate(units):
        cols = slice(hh * HEAD_DIM, (hh + 1) * HEAD_DIM)
        lo, n_prev = split(g)
        q = q_ref[lo:lo + ATT_GQ, cols]
        c = ATTN_SCALE * LOG2E
        s_prev = lax.dot_general(q, kp_ref[lo:, cols], dims, preferred_element_type=F32)
        s_ref[u, :, :n_prev] = s_prev * c + bias_ref[hh, :, :n_prev]
        s_cur = lax.dot_general(q, kc_ref[:ATT_GK - n_prev, cols], dims, preferred_element_type=F32)
        s_ref[u, :, n_prev:] = s_cur * c + bias_ref[hh, :, n_prev:]

    @pl.when(i == 0)
    def _():
        for u, (hh, g) in enumerate(units):
            _, n_prev = split(g)
            s_ref[u, :, :n_prev] = jnp.full((ATT_GQ, n_prev), NEG_INF, F32)

    for u, (hh, g) in enumerate(units):
        x = s_ref[u]
        e = jnp.exp2(x - jnp.max(x, axis=-1, keepdims=True))
        l_ref[u] = jnp.sum(e, axis=-1, keepdims=True)
        p_ref[u] = e.astype(BF16)

    for u, (hh, g) in enumerate(units):
        cols = slice(hh * HEAD_DIM, (hh + 1) * HEAD_DIM)
        lo, n_prev = split(g)
        pv = (jnp.dot(p_ref[u, :, :n_prev], vp_ref[lo:, cols], preferred_element_type=F32)
              + jnp.dot(p_ref[u, :, n_prev:], vc_ref[:ATT_GK - n_prev, cols], preferred_element_type=F32))
        o_ref[lo:lo + ATT_GQ, cols] = (pv / l_ref[u]).astype(o_ref.dtype)


def _attn_prompt(qkv, bias):
    n = qkv.shape[0]
    width = ATT_HB * HEAD_DIM
    nhb = D_MODEL // width
    n_units = ATT_HB * (ATT_TQ // ATT_GQ)

    def prev(i):
        return jnp.maximum(i - 1, 0)

    return pl.pallas_call(
        _attn_prompt_kernel,
        grid=(n // ATT_TQ, nhb),
        in_specs=[
            pl.BlockSpec((ATT_TQ, width), lambda i, h: (i, h)),
            pl.BlockSpec((ATT_TQ, width), lambda i, h: (prev(i), nhb + h)),
            pl.BlockSpec((ATT_TQ, width), lambda i, h: (i, nhb + h)),
            pl.BlockSpec((ATT_TQ, width), lambda i, h: (prev(i), 2 * nhb + h)),
            pl.BlockSpec((ATT_TQ, width), lambda i, h: (i, 2 * nhb + h)),
            pl.BlockSpec((ATT_HB, ATT_GQ, ATT_GK), lambda i, h: (h, 0, 0)),
        ],
        out_specs=pl.BlockSpec((ATT_TQ, width), lambda i, h: (i, h)),
        out_shape=jax.ShapeDtypeStruct((n, D_MODEL), BF16),
        scratch_shapes=[pltpu.VMEM((n_units, ATT_GQ, ATT_GK), F32), pltpu.VMEM((n_units, ATT_GQ, ATT_GK), BF16),
                        pltpu.VMEM((n_units, ATT_GQ, 1), F32)],
        compiler_params=_cparams("parallel", "parallel"),
        name="attn_prompt",
    )(qkv, qkv, qkv, qkv, qkv, bias)


def _attn_sample_kernel(q_ref, kn_ref, vn_ref, ck_ref, cv_ref, bias_ref, o_ref, *, ncache):
    s_len = q_ref.shape[0]
    for h in range(N_HEADS):
        cols = slice(h * HEAD_DIM, (h + 1) * HEAD_DIM)
        q = q_ref[:, cols]
        k_old = ck_ref[pl.ds(h, ncache, stride=N_HEADS), :].astype(BF16)
        v_old = cv_ref[pl.ds(h, ncache, stride=N_HEADS), :].astype(BF16)
        k_new, v_new = kn_ref[:, cols], vn_ref[:, cols]
        dims = (((1,), (1,)), ((), ()))
        s_old = lax.dot_general(q, k_old, dims, preferred_element_type=F32) * ATTN_SCALE + bias_ref[h, :, :ncache]
        s_new = (lax.dot_general(q, k_new, dims, preferred_element_type=F32) * ATTN_SCALE
                 + bias_ref[h, :, ncache:ncache + s_len])
        m = jnp.maximum(jnp.max(s_old, axis=-1, keepdims=True), jnp.max(s_new, axis=-1, keepdims=True))
        e_old, e_new = jnp.exp(s_old - m), jnp.exp(s_new - m)
        den = jnp.sum(e_old, axis=-1, keepdims=True) + jnp.sum(e_new, axis=-1, keepdims=True)
        pv = (jnp.dot(e_old.astype(BF16), v_old, preferred_element_type=F32)
              + jnp.dot(e_new.astype(BF16), v_new, preferred_element_type=F32))
        o_ref[:, cols] = (pv / den).astype(o_ref.dtype)


def _attn_sample(qkv, cache_k, cache_v, bias, bsz, s_len):
    ncache = cache_k.shape[1] // N_HEADS
    return pl.pallas_call(
        functools.partial(_attn_sample_kernel, ncache=ncache),
        grid=(bsz,),
        in_specs=[
            pl.BlockSpec((s_len, D_MODEL), lambda b: (b, 0)),
            pl.BlockSpec((s_len, D_MODEL), lambda b: (b, 1)),
            pl.BlockSpec((s_len, D_MODEL), lambda b: (b, 2)),
            pl.BlockSpec((None, ncache * N_HEADS, HEAD_DIM), lambda b: (b, 0, 0)),
            pl.BlockSpec((None, ncache * N_HEADS, HEAD_DIM), lambda b: (b, 0, 0)),
            pl.BlockSpec(bias.shape, lambda b: (0, 0, 0)),
        ],
        out_specs=pl.BlockSpec((s_len, D_MODEL), lambda b: (b, 0)),
        out_shape=jax.ShapeDtypeStruct((bsz * s_len, D_MODEL), BF16),
        compiler_params=_cparams("parallel"),
        name="attn_sample",
    )(qkv, qkv, qkv, cache_k, cache_v, bias)


def kernel(x_prompt, x_sample, state_s5_re, state_s5_im, cache_k, cache_v, p_prompt, p_sample, ffn1_w_in, ffn1_w_out, ffn2_w_in, ffn2_w_out, ln_g, ln_b, ple_w_proj, ple_w_gate, s5_a_re, s5_a_im, s5_log_dt, s5_b_re, s5_b_im, s5_c_re, s5_c_im, s5_d, s5_w_glu, attn_w_qkv, attn_w_o, attn_rel_bias):
    bsz_p, seq, _ = x_prompt.shape
    bsz_s, s_len, _ = x_sample.shape
    assert bsz_p == 1 and s_len % S5_CHUNK == 0 and seq % (S5_CHUNK * 512) == 0
    n_p, n_s = bsz_p * seq, bsz_s * s_len
    tm_glu, tm_s = 512, n_s
    tm_ffn, tm_proj, tn_wide = 1024, 1024, 1024

    xp = x_prompt.reshape(n_p, D_MODEL)
    xs = x_sample.reshape(n_s, D_MODEL)
    pp = p_prompt.reshape(DEPTH, n_p, PLE_DIM)
    ps = p_sample.reshape(DEPTH, n_s, PLE_DIM)
    w1 = (ffn1_w_in[0].astype(BF16), ffn1_w_out[0].astype(BF16))
    w_proj, w_gate = ple_w_proj.astype(BF16), ple_w_gate.astype(BF16)
    w_glu, w_qkv, w_o = s5_w_glu.astype(BF16), attn_w_qkv.astype(BF16), attn_w_o.astype(BF16)

    def norm(i, slot):
        return ln_g[i, slot].reshape(1, D_MODEL), ln_b[i, slot].reshape(1, D_MODEL)

    outs = {}
    for i in range(DEPTH):
        xp, *w2 = _ffn_ln(xp, *w1, *norm(i, 0), tm_ffn, next_w=(ffn2_w_in, ffn2_w_out, i))
        xs = _ffn_ln(xs, *w1, *norm(i, 0), tm_s)
        if i % N_MIXERS == 0:
            mw, v, at = _s5_prep(s5_a_re, s5_a_im, s5_log_dt, s5_b_re, s5_b_im, s5_c_re, s5_c_im)
            d = s5_d.reshape(S5_CB, 1, LANES)
            y, h_fin = _s5_prompt(xp, mw, v, at, d, tb=512)
            xp = _glu_ln(xp, y, w_glu, *norm(i, 1), tm_glu)
            outs["s5_p"] = (h_fin[:, 0, :S5_SPB].reshape(bsz_p, S5_GROUPS, S5_STATE),
                            h_fin[:, 0, S5_SPB:].reshape(bsz_p, S5_GROUPS, S5_STATE))
            nblk = s_len // S5_CHUNK
            u = (xs.reshape(bsz_s, nblk, S5_CHUNK, S5_CB, LANES).transpose(3, 1, 0, 2, 4)
                 .reshape(S5_CB, nblk, bsz_s, S5_UW))
            ys, hs_re, hs_im = _s5_sample(u, state_s5_re.reshape(bsz_s, -1), state_s5_im.reshape(bsz_s, -1),
                                          mw, v, at, d)
            ys = (ys.reshape(S5_CB, nblk, bsz_s, S5_CHUNK, LANES).transpose(2, 1, 3, 0, 4)
                  .reshape(n_s, D_MODEL))
            xs = _glu_ln(xs, ys, w_glu, *norm(i, 1), tm_s)
            outs["s5_s"] = (hs_re.reshape(bsz_s, S5_GROUPS, S5_STATE), hs_im.reshape(bsz_s, S5_GROUPS, S5_STATE))
        else:
            rows = min(BAND_PAST, seq)
            assert seq % ATT_TQ == 0
            qkv_p = _proj(xp, w_qkv, 0, 3 * D_MODEL, BF16, tm_proj, tn_wide)
            kv_tail = _proj(xp, w_qkv, D_MODEL, 2 * D_MODEL, F32, rows, tn_wide, row0=n_p - rows, nrows=rows)
            bias_p = _rel_bias(attn_rel_bias, BAND_PAST, ATT_GQ, ATT_GK, scale=LOG2E, band=True)
            att_p = _attn_prompt(qkv_p, bias_p)
            xp = _wo_ln(xp, att_p, w_o, *norm(i, 1), tm_ffn)
            outs["kv_p"] = (kv_tail[:, :D_MODEL].reshape(bsz_p, rows, N_HEADS, HEAD_DIM),
                            kv_tail[:, D_MODEL:].reshape(bsz_p, rows, N_HEADS, HEAD_DIM))

            ncache = cache_k.shape[1]
            qkv_s = _proj(xs, w_qkv, 0, 3 * D_MODEL, BF16, tm_s, tn_wide)
            kv_s = _proj(xs, w_qkv, D_MODEL, 2 * D_MODEL, F32, tm_s, tn_wide)
            nk_pad = -(-(ncache + s_len) // LANES) * LANES
            bias_s = _rel_bias(attn_rel_bias, ncache, s_len, nk_pad)
            att_s = _attn_sample(qkv_s, cache_k.reshape(bsz_s, ncache * N_HEADS, HEAD_DIM),
                                 cache_v.reshape(bsz_s, ncache * N_HEADS, HEAD_DIM), bias_s, bsz_s, s_len)
            xs = _wo_ln(xs, att_s, w_o, *norm(i, 1), tm_s)
            outs["kv_s"] = (kv_s[:, :D_MODEL].reshape(bsz_s, s_len, N_HEADS, HEAD_DIM),
                            kv_s[:, D_MODEL:].reshape(bsz_s, s_len, N_HEADS, HEAD_DIM))
        if i + 1 < DEPTH:
            xp, *w1 = _ffn_ln(xp, *w2, *norm(i, 2), tm_ffn, next_w=(ffn1_w_in, ffn1_w_out, i + 1))
        else:
            xp = _ffn_ln(xp, *w2, *norm(i, 2), tm_ffn)
        xs = _ffn_ln(xs, *w2, *norm(i, 2), tm_s)
        xp = _ple_ln(xp, pp, w_proj, w_gate, i, *norm(i, 3), tm_ffn)
        xs = _ple_ln(xs, ps, w_proj, w_gate, i, *norm(i, 3), tm_s)

    return (xp.reshape(bsz_p, seq, D_MODEL), xs.reshape(bsz_s, s_len, D_MODEL),
            *outs["s5_p"], *outs["kv_p"], *outs["s5_s"], *outs["kv_s"])
```

```python
import functools

import jax
import jax.numpy as jnp
from jax import lax
from jax.experimental import pallas as pl
from jax.experimental.pallas import tpu as pltpu

F32 = jnp.float32
BF16 = jnp.bfloat16

D_MODEL = 2048
DEPTH = 2
N_MIXERS = 2
CHUNK = 64
S5_GROUP = 16
S5_GROUPS = D_MODEL // S5_GROUP
S5_STATE = 64
N_HEADS = 16
HEAD_DIM = D_MODEL // N_HEADS
PAST_CHUNKS = 8
BAND_PAST = PAST_CHUNKS * CHUNK
BAND = BAND_PAST + CHUNK
REL_CLIP = 128
ATTN_SCALE = HEAD_DIM ** -0.5
NEG_INF = -1e30
D_FF = 5632
PLE_DIM = 256
DN_ALPHA = (2 * DEPTH) ** 0.25
LN_EPS = 1e-5

LANES = 128
MXU_DIM = 256
VMEM_LIMIT = 60 * 1024 * 1024

TF = 512
FFN_SUB = MXU_DIM
FFN_OUT = 512
LN_ROWS = 128
COL_TILE = 512

S5_CB = D_MODEL // LANES
S5_GPB = LANES // S5_GROUP
S5_SPB = S5_GPB * S5_STATE
S5_CHUNK = 8
S5_UW = S5_CHUNK * LANES


def _cparams(*sem):
    return pltpu.CompilerParams(dimension_semantics=sem, vmem_limit_bytes=VMEM_LIMIT)


def _layer_norm(v, g, b, eps=LN_EPS):
    mu = jnp.mean(v, axis=-1, keepdims=True)
    d = v - mu
    var = jnp.mean(d * d, axis=-1, keepdims=True)
    return d * lax.rsqrt(var + eps) * g + b


def _norm_rows(o_ref, g_ref, b_ref, eps=LN_EPS):
    rows = min(LN_ROWS, o_ref.shape[0])
    for r in range(o_ref.shape[0] // rows):
        rs = slice(r * rows, (r + 1) * rows)
        o_ref[rs, :] = _layer_norm(o_ref[rs, :], g_ref[...], b_ref[...], eps)


def _resident(block_shape, index_map):
    return pl.BlockSpec(block_shape, index_map, pipeline_mode=pl.Buffered(1))


def _ffn_kernel(*refs, nj, cast_next):
    if cast_next:
        x_ref, wg_ref, wu_ref, wo_ref, g_ref, b_ref, nin_ref, nout_ref, o_ref, cin_ref, cout_ref, xb_ref = refs
        cin_ref[...] = nin_ref[...].astype(BF16)
        cout_ref[...] = nout_ref[...].astype(BF16)
    else:
        x_ref, wg_ref, wu_ref, wo_ref, g_ref, b_ref, o_ref, xb_ref = refs
    j = pl.program_id(1)

    @pl.when(j == 0)
    def _():
        x = x_ref[...]
        xb_ref[...] = x.astype(BF16)
        o_ref[...] = (2.0 * DN_ALPHA) * x

    xb = xb_ref[...]
    acts = []
    for h in range(TF // FFN_SUB):
        cs = slice(h * FFN_SUB, (h + 1) * FFN_SUB)
        gate = jnp.dot(xb, wg_ref[:, cs], preferred_element_type=F32)
        up = jnp.dot(xb, wu_ref[:, cs], preferred_element_type=F32)
        acts.append((gate * jax.nn.sigmoid(gate) * up).astype(BF16))
    act = jnp.concatenate(acts, axis=1)
    for n in range(D_MODEL // FFN_OUT):
        ns = slice(n * FFN_OUT, (n + 1) * FFN_OUT)
        o_ref[:, ns] += jnp.dot(act, wo_ref[:, ns], preferred_element_type=F32)

    @pl.when(j == nj - 1)
    def _():
        _norm_rows(o_ref, g_ref, b_ref, eps=4.0 * LN_EPS)


def _ffn_ln(x, w_in, w_out, g, b, tm, next_w=None):
    n = x.shape[0]
    ni, nj = n // tm, D_FF // TF
    in_specs = [
        pl.BlockSpec((tm, D_MODEL), lambda i, j: (i, 0)),
        pl.BlockSpec((D_MODEL, TF), lambda i, j: (0, j)),
        pl.BlockSpec((D_MODEL, TF), lambda i, j: (0, j + nj)),
        pl.BlockSpec((TF, D_MODEL), lambda i, j: (j, 0)),
        pl.BlockSpec((1, D_MODEL), lambda i, j: (0, 0)),
        pl.BlockSpec((1, D_MODEL), lambda i, j: (0, 0)),
    ]
    out_specs = [pl.BlockSpec((tm, D_MODEL), lambda i, j: (i, 0))]
    out_shape = [jax.ShapeDtypeStruct((n, D_MODEL), F32)]
    args = [x, w_in, w_in, w_out, g, b]
    if next_w is not None:
        nw_in, nw_out, layer = next_w
        assert D_MODEL % ni == 0 and (2 * D_FF) % nj == 0 and D_FF % nj == 0
        in_blk = (D_MODEL // ni, 2 * D_FF // nj)
        out_blk = (D_FF // nj, D_MODEL // ni)
        in_specs += [pl.BlockSpec((None,) + in_blk, lambda i, j: (layer, i, j)),
                     pl.BlockSpec((None,) + out_blk, lambda i, j: (layer, j, i))]
        out_specs += [pl.BlockSpec(in_blk, lambda i, j: (i, j)), pl.BlockSpec(out_blk, lambda i, j: (j, i))]
        out_shape += [jax.ShapeDtypeStruct(nw_in.shape[1:], BF16), jax.ShapeDtypeStruct(nw_out.shape[1:], BF16)]
        args += [nw_in, nw_out]
    res = pl.pallas_call(
        functools.partial(_ffn_kernel, nj=nj, cast_next=next_w is not None),
        grid=(ni, nj),
        in_specs=in_specs,
        out_specs=out_specs,
        out_shape=out_shape,
        scratch_shapes=[pltpu.VMEM((tm, D_MODEL), BF16)],
        compiler_params=_cparams("parallel", "arbitrary"),
        name="ffn_ln",
    )(*args)
    return res[0] if next_w is None else res


def _ple_kernel(x_ref, p_ref, wp_ref, wg_ref, g_ref, b_ref, o_ref):
    xb = x_ref[...].astype(BF16)
    pb = p_ref[...].astype(BF16)
    for n in range(D_MODEL // COL_TILE):
        ns = slice(n * COL_TILE, (n + 1) * COL_TILE)
        proj = jnp.dot(pb, wp_ref[:, ns], preferred_element_type=F32)
        gate = jnp.dot(xb, wg_ref[:, ns], preferred_element_type=F32)
        o_ref[:, ns] = DN_ALPHA * x_ref[:, ns] + proj * jax.nn.sigmoid(gate)
    _norm_rows(o_ref, g_ref, b_ref)


def _ple_ln(x, p, w_proj, w_gate, layer, g, b, tm):
    n = x.shape[0]
    return pl.pallas_call(
        _ple_kernel,
        grid=(n // tm,),
        in_specs=[
            pl.BlockSpec((tm, D_MODEL), lambda i: (i, 0)),
            pl.BlockSpec((None, tm, PLE_DIM), lambda i: (layer, i, 0)),
            _resident((None, PLE_DIM, D_MODEL), lambda i: (layer, 0, 0)),
            _resident((None, D_MODEL, D_MODEL), lambda i: (layer, 0, 0)),
            _resident((1, D_MODEL), lambda i: (0, 0)),
            _resident((1, D_MODEL), lambda i: (0, 0)),
        ],
        out_specs=pl.BlockSpec((tm, D_MODEL), lambda i: (i, 0)),
        out_shape=jax.ShapeDtypeStruct((n, D_MODEL), F32),
        compiler_params=_cparams("parallel"),
        name="ple_ln",
    )(x, p, w_proj, w_gate, g, b)


def _s5_discretize(ar, ai, log_dt):
    dt = jnp.exp(log_dt)
    mag = jnp.exp(ar * dt)
    ab_re, ab_im = mag * jnp.cos(ai * dt), mag * jnp.sin(ai * dt)
    den = ar * ar + ai * ai
    nr, ni = ab_re - 1.0, ab_im
    return ab_re, ab_im, (nr * ar + ni * ai) / den, (ni * ar - nr * ai) / den


def _s5_prep_kernel(ar_r, ai_r, ldt_r, ar_c, ai_c, ldt_c, bre_ref, bim_ref, cre_ref, cim_ref,
                    mw_ref, v_ref, at_ref):
    ab_re, ab_im, cf_re, cf_im = _s5_discretize(ar_r[...], ai_r[...], ldt_r[...])
    b_re, b_im = bre_ref[...], bim_ref[...]
    c_re, c_im = cre_ref[...], cim_ref[...]
    p_re = cf_re * b_re - cf_im * b_im
    p_im = cf_re * b_im + cf_im * b_re
    ak_re, ak_im = jnp.ones_like(ab_re), jnp.zeros_like(ab_im)
    pows = []
    for _ in range(S5_CHUNK):
        pows.append((p_re, p_im))
        p_re, p_im = ab_re * p_re - ab_im * p_im, ab_re * p_im + ab_im * p_re
        ak_re, ak_im = ab_re * ak_re - ab_im * ak_im, ab_re * ak_im + ab_im * ak_re
    at_ref[:, :S5_SPB] = ak_re
    at_ref[:, S5_SPB:] = ak_im

    taps = (jnp.dot(jnp.concatenate([p[0] for p in pows], axis=0), c_re,
                    precision=lax.Precision.HIGHEST, preferred_element_type=F32)
            - jnp.dot(jnp.concatenate([p[1] for p in pows], axis=0), c_im,
                      precision=lax.Precision.HIGHEST, preferred_element_type=F32)).astype(BF16)
    zero = jnp.zeros((LANES, LANES), BF16)
    for s in range(S5_CHUNK):
        rows = slice(s * LANES, (s + 1) * LANES)
        for t in range(S5_CHUNK):
            lag = slice((t - s) * LANES, (t - s + 1) * LANES)
            mw_ref[rows, t * LANES:(t + 1) * LANES] = taps[lag, :] if t >= s else zero
        w_re, w_im = pows[S5_CHUNK - 1 - s]
        mw_ref[rows, S5_UW:S5_UW + S5_SPB] = w_re.astype(BF16)
        mw_ref[rows, S5_UW + S5_SPB:] = w_im.astype(BF16)

    ac_re, ac_im, _, _ = _s5_discretize(ar_c[...], ai_c[...], ldt_c[...])
    ck_re, ck_im = ac_re, ac_im
    for t in range(S5_CHUNK):
        cols = slice(t * LANES, (t + 1) * LANES)
        v_ref[:S5_SPB, cols] = (ck_re * c_re - ck_im * c_im).astype(BF16)
        v_ref[S5_SPB:, cols] = (-(ck_im * c_re + ck_re * c_im)).astype(BF16)
        ck_re, ck_im = ck_re * ac_re - ck_im * ac_im, ck_re * ac_im + ck_im * ac_re


def _s5_prep(a_re, a_im, log_dt, b_re, b_im, c_re, c_im):
    eye = jnp.eye(S5_GPB, dtype=F32)

    def rows(v):
        return v.reshape(S5_CB, 1, S5_SPB)

    def cols(v):
        return v.reshape(S5_CB, S5_SPB, 1)

    def b_blockdiag(w):
        w4 = w.reshape(S5_CB, S5_GPB, S5_STATE, S5_GROUP).transpose(0, 1, 3, 2)
        return (w4[:, :, :, None, :] * eye[None, :, None, :, None]).reshape(S5_CB, LANES, S5_SPB)

    def c_blockdiag(w):
        w4 = w.reshape(S5_CB, S5_GPB, S5_GROUP, S5_STATE).transpose(0, 1, 3, 2)
        return (w4[:, :, :, None, :] * eye[None, :, None, :, None]).reshape(S5_CB, S5_SPB, LANES)

    ldt = jnp.repeat(log_dt, S5_STATE)
    row_spec = pl.BlockSpec((None, 1, S5_SPB), lambda c: (c, 0, 0))
    col_spec = pl.BlockSpec((None, S5_SPB, 1), lambda c: (c, 0, 0))
    b_spec = pl.BlockSpec((None, LANES, S5_SPB), lambda c: (c, 0, 0))
    c_spec = pl.BlockSpec((None, S5_SPB, LANES), lambda c: (c, 0, 0))
    return pl.pallas_call(
        _s5_prep_kernel,
        grid=(S5_CB,),
        in_specs=[row_spec, row_spec, row_spec, col_spec, col_spec, col_spec, b_spec, b_spec, c_spec, c_spec],
        out_specs=[
            pl.BlockSpec((None, S5_UW, S5_UW + 2 * S5_SPB), lambda c: (c, 0, 0)),
            pl.BlockSpec((None, 2 * S5_SPB, S5_UW), lambda c: (c, 0, 0)),
            pl.BlockSpec((None, 1, 2 * S5_SPB), lambda c: (c, 0, 0)),
        ],
        out_shape=[
            jax.ShapeDtypeStruct((S5_CB, S5_UW, S5_UW + 2 * S5_SPB), BF16),
            jax.ShapeDtypeStruct((S5_CB, 2 * S5_SPB, S5_UW), BF16),
            jax.ShapeDtypeStruct((S5_CB, 1, 2 * S5_SPB), F32),
        ],
        compiler_params=_cparams("parallel"),
        name="s5_prep",
    )(rows(a_re), rows(a_im), rows(ldt), cols(a_re), cols(a_im), cols(ldt),
      b_blockdiag(b_re), b_blockdiag(b_im), c_blockdiag(c_re), c_blockdiag(c_im))


def _s5_step(at_ref, h_re, h_im, s_re, s_im):
    a_re, a_im = at_ref[:, :S5_SPB], at_ref[:, S5_SPB:]
    return a_re * h_re - a_im * h_im + s_re, a_re * h_im + a_im * h_re + s_im


def _s5_prompt_kernel(*refs, tb, nr, n_cast):
    x_ref, mw_ref, v_ref, at_ref, d_ref = refs[:5]
    w32_refs = refs[5:5 + n_cast]
    y_ref, hfin_ref = refs[5 + n_cast:7 + n_cast]
    w16_refs = refs[7 + n_cast:7 + 2 * n_cast]
    hc_ref, s_ref, h_ref = refs[7 + 2 * n_cast:]
    for w32, w16 in zip(w32_refs, w16_refs):
        w16[...] = w32[...].astype(BF16)
    r = pl.program_id(1)

    @pl.when(r == 0)
    def _():
        hc_ref[...] = jnp.zeros_like(hc_ref)

    u = jnp.concatenate([x_ref[pl.ds(s, tb, stride=S5_CHUNK), :] for s in range(S5_CHUNK)], axis=1)
    ub = u.astype(BF16)
    s_ref[...] = jnp.dot(ub, mw_ref[:, S5_UW:], preferred_element_type=F32)

    def body(i, h):
        h_re, h_im = h
        row = pl.ds(i, 1)
        h_ref[row, :S5_SPB] = h_re
        h_ref[row, S5_SPB:] = h_im
        return _s5_step(at_ref, h_re, h_im, s_ref[row, :S5_SPB], s_ref[row, S5_SPB:])

    h_re, h_im = lax.fori_loop(0, tb, body, (hc_ref[:, :S5_SPB], hc_ref[:, S5_SPB:]), unroll=8)
    hc_ref[:, :S5_SPB] = h_re
    hc_ref[:, S5_SPB:] = h_im

    hb = h_ref[...].astype(BF16)
    steps = MXU_DIM // LANES
    for c in range(S5_UW // MXU_DIM):
        cols = slice(c * MXU_DIM, (c + 1) * MXU_DIM)
        k = (c + 1) * MXU_DIM
        y = (jnp.dot(ub[:, :k], mw_ref[:k, cols], preferred_element_type=F32)
             + jnp.dot(hb, v_ref[:, cols], preferred_element_type=F32)
             + jnp.concatenate([d_ref[...]] * steps, axis=1) * u[:, cols])
        for t in range(steps):
            y_ref[pl.ds(c * steps + t, tb, stride=S5_CHUNK), :] = y[:, t * LANES:(t + 1) * LANES]

    @pl.when(r == nr - 1)
    def _():
        hfin_ref[...] = hc_ref[...]


def _s5_prompt(x, mw, v, at, d, tb, cast=()):
    n = x.shape[0]
    rows = tb * S5_CHUNK
    nr = n // rows
    cast_blocks = []
    for w in cast:
        assert w.shape[0] % (S5_CB * 16) == 0 and w.shape[1] % (nr * LANES) == 0
        cast_blocks.append(pl.BlockSpec((w.shape[0] // S5_CB, w.shape[1] // nr), lambda c, r: (c, r)))
    return pl.pallas_call(
        functools.partial(_s5_prompt_kernel, tb=tb, nr=nr, n_cast=len(cast)),
        grid=(S5_CB, nr),
        in_specs=[
            pl.BlockSpec((rows, LANES), lambda c, r: (r, c)),
            pl.BlockSpec((None, S5_UW, S5_UW + 2 * S5_SPB), lambda c, r: (c, 0, 0)),
            pl.BlockSpec((None, 2 * S5_SPB, S5_UW), lambda c, r: (c, 0, 0)),
            pl.BlockSpec((None, 1, 2 * S5_SPB), lambda c, r: (c, 0, 0)),
            pl.BlockSpec((None, 1, LANES), lambda c, r: (c, 0, 0)),
        ] + cast_blocks,
        out_specs=[
            pl.BlockSpec((rows, LANES), lambda c, r: (r, c)),
            pl.BlockSpec((None, 1, 2 * S5_SPB), lambda c, r: (c, 0, 0)),
        ] + cast_blocks,
        out_shape=[
            jax.ShapeDtypeStruct((n, D_MODEL), F32),
            jax.ShapeDtypeStruct((S5_CB, 1, 2 * S5_SPB), F32),
        ] + [jax.ShapeDtypeStruct(w.shape, BF16) for w in cast],
        scratch_shapes=[pltpu.VMEM((1, 2 * S5_SPB), F32), pltpu.VMEM((tb, 2 * S5_SPB), F32),
                        pltpu.VMEM((tb, 2 * S5_SPB), F32)],
        compiler_params=_cparams("parallel", "arbitrary"),
        name="s5_prompt",
    )(x, mw, v, at, d, *cast)


def _s5_sample_kernel(u_ref, hre_ref, him_ref, mw_ref, v_ref, at_ref, d_ref, y_ref, ore_ref, oim_ref, *, nblk):
    d_row = jnp.concatenate([d_ref[...]] * S5_CHUNK, axis=1)
    h_re, h_im = hre_ref[...], him_ref[...]
    for k in range(nblk):
        u = u_ref[k]
        res = jnp.dot(u.astype(BF16), mw_ref[...], preferred_element_type=F32)
        h = jnp.concatenate([h_re, h_im], axis=1).astype(BF16)
        y_ref[k] = res[:, :S5_UW] + jnp.dot(h, v_ref[...], preferred_element_type=F32) + d_row * u
        h_re, h_im = _s5_step(at_ref, h_re, h_im, res[:, S5_UW:S5_UW + S5_SPB], res[:, S5_UW + S5_SPB:])
    ore_ref[...] = h_re
    oim_ref[...] = h_im


def _s5_sample(u, h_re, h_im, mw, v, at, d):
    _, nblk, bsz, _ = u.shape
    h_spec = pl.BlockSpec((bsz, S5_SPB), lambda c: (0, c))
    u_spec = pl.BlockSpec((None, nblk, bsz, S5_UW), lambda c: (c, 0, 0, 0))
    return pl.pallas_call(
        functools.partial(_s5_sample_kernel, nblk=nblk),
        grid=(S5_CB,),
        in_specs=[
            u_spec, h_spec, h_spec,
            pl.BlockSpec((None, S5_UW, S5_UW + 2 * S5_SPB), lambda c: (c, 0, 0)),
            pl.BlockSpec((None, 2 * S5_SPB, S5_UW), lambda c: (c, 0, 0)),
            pl.BlockSpec((None, 1, 2 * S5_SPB), lambda c: (c, 0, 0)),
            pl.BlockSpec((None, 1, LANES), lambda c: (c, 0, 0)),
        ],
        out_specs=[u_spec, h_spec, h_spec],
        out_shape=[
            jax.ShapeDtypeStruct(u.shape, F32),
            jax.ShapeDtypeStruct(h_re.shape, F32),
            jax.ShapeDtypeStruct(h_im.shape, F32),
        ],
        compiler_params=_cparams("parallel"),
        name="s5_sample",
    )(u, h_re, h_im, mw, v, at, d)


def _glu_kernel(x_ref, y_ref, w_ref, g_ref, b_ref, o_ref):
    z = jax.nn.gelu(y_ref[...]).astype(BF16)
    for n in range(D_MODEL // COL_TILE):
        ns = slice(n * COL_TILE, (n + 1) * COL_TILE)
        gs = slice(D_MODEL + n * COL_TILE, D_MODEL + (n + 1) * COL_TILE)
        za = jnp.dot(z, w_ref[:, ns], preferred_element_type=F32)
        zb = jnp.dot(z, w_ref[:, gs], preferred_element_type=F32)
        o_ref[:, ns] = DN_ALPHA * x_ref[:, ns] + za * jax.nn.sigmoid(zb)
    _norm_rows(o_ref, g_ref, b_ref)


def _glu_ln(x, y, w_glu, g, b, tm):
    n = x.shape[0]
    return pl.pallas_call(
        _glu_kernel,
        grid=(n // tm,),
        in_specs=[
            pl.BlockSpec((tm, D_MODEL), lambda i: (i, 0)),
            pl.BlockSpec((tm, D_MODEL), lambda i: (i, 0)),
            _resident((D_MODEL, 2 * D_MODEL), lambda i: (0, 0)),
            _resident((1, D_MODEL), lambda i: (0, 0)),
            _resident((1, D_MODEL), lambda i: (0, 0)),
        ],
        out_specs=pl.BlockSpec((tm, D_MODEL), lambda i: (i, 0)),
        out_shape=jax.ShapeDtypeStruct((n, D_MODEL), F32),
        compiler_params=_cparams("parallel"),
        name="glu_ln",
    )(x, y, w_glu, g, b)


def _proj_kernel(x_ref, w_ref, o_ref, xb_ref):
    @pl.when(pl.program_id(1) == 0)
    def _():
        xb_ref[...] = x_ref[...].astype(BF16)

    o_ref[...] = jnp.dot(xb_ref[...], w_ref[...], preferred_element_type=F32).astype(o_ref.dtype)


def _proj(x, w, col0, ncols, out_dtype, tm, tn, row0=0, nrows=None):
    nrows = x.shape[0] if nrows is None else nrows
    assert row0 % tm == 0 and col0 % tn == 0
    rb, cb = row0 // tm, col0 // tn
    return pl.pallas_call(
        _proj_kernel,
        grid=(nrows // tm, ncols // tn),
        in_specs=[
            pl.BlockSpec((tm, D_MODEL), lambda i, j: (i + rb, 0)),
            pl.BlockSpec((D_MODEL, tn), lambda i, j: (0, j + cb)),
        ],
        out_specs=pl.BlockSpec((tm, tn), lambda i, j: (i, j)),
        out_shape=jax.ShapeDtypeStruct((nrows, ncols), out_dtype),
        scratch_shapes=[pltpu.VMEM((tm, D_MODEL), BF16)],
        compiler_params=_cparams("parallel", "arbitrary"),
        name="proj",
    )(x, w)


def _wo_kernel(x_ref, a_ref, w_ref, g_ref, b_ref, o_ref):
    a = a_ref[...]
    for n in range(D_MODEL // COL_TILE):
        ns = slice(n * COL_TILE, (n + 1) * COL_TILE)
        o_ref[:, ns] = DN_ALPHA * x_ref[:, ns] + jnp.dot(a, w_ref[:, ns], preferred_element_type=F32)
    _norm_rows(o_ref, g_ref, b_ref)


def _wo_ln(x, a, w_o, g, b, tm):
    n = x.shape[0]
    return pl.pallas_call(
        _wo_kernel,
        grid=(n // tm,),
        in_specs=[
            pl.BlockSpec((tm, D_MODEL), lambda i: (i, 0)),
            pl.BlockSpec((tm, D_MODEL), lambda i: (i, 0)),
            _resident((D_MODEL, D_MODEL), lambda i: (0, 0)),
            _resident((1, D_MODEL), lambda i: (0, 0)),
            _resident((1, D_MODEL), lambda i: (0, 0)),
        ],
        out_specs=pl.BlockSpec((tm, D_MODEL), lambda i: (i, 0)),
        out_shape=jax.ShapeDtypeStruct((n, D_MODEL), F32),
        compiler_params=_cparams("parallel"),
        name="wo_ln",
    )(x, a, w_o, g, b)


def _bias_kernel(tab_ref, o_ref, base_ref, *, q_off, nq, nk, scale, band, invalid):
    h = pl.program_id(0)
    ntab, nw = tab_ref.shape[1], base_ref.shape[1]

    @pl.when(h == 0)
    def _():
        off = lax.broadcasted_iota(jnp.int32, (ntab, nw), 1)
        off = jnp.where(off < nk, off, off - nw)
        idx = jnp.clip(q_off - off, -REL_CLIP, REL_CLIP) + REL_CLIP
        onehot = (idx == lax.broadcasted_iota(jnp.int32, (ntab, nw), 0)).astype(F32)
        base_ref[...] = scale * jnp.dot(tab_ref[...], onehot, precision=lax.Precision.HIGHEST,
                                        preferred_element_type=F32)

    bias = pltpu.roll(jnp.broadcast_to(base_ref[pl.ds(h, 1), :], (nq, nw)), 0, 1, stride=1, stride_axis=0)
    row = lax.broadcasted_iota(jnp.int32, (nq, nw), 0)
    col = lax.broadcasted_iota(jnp.int32, (nq, nw), 1)
    if band:
        first = row & ~(CHUNK - 1)
        bias = jnp.where((col >= first) & (col < first + BAND), bias, NEG_INF)
    for v, n_invalid in enumerate(invalid):
        o_ref[v] = jnp.where(col >= n_invalid, bias, NEG_INF)[:, :nk]


def _rel_bias(table, q_off, nq, nk, scale=1.0, band=False, invalid=(0,)):
    assert nk % LANES == 0
    ntab = table.shape[1]
    ntab_pad = -(-ntab // LANES) * LANES
    nw = -(-(nq + nk) // LANES) * LANES
    table = jnp.pad(table, ((0, 0), (0, ntab_pad - ntab)))
    return pl.pallas_call(
        functools.partial(_bias_kernel, q_off=q_off, nq=nq, nk=nk, scale=scale, band=band, invalid=invalid),
        grid=(N_HEADS,),
        in_specs=[_resident((N_HEADS, ntab_pad), lambda h: (0, 0))],
        out_specs=pl.BlockSpec((len(invalid), None, nq, nk), lambda h: (0, h, 0, 0)),
        out_shape=jax.ShapeDtypeStruct((len(invalid), N_HEADS, nq, nk), F32),
        scratch_shapes=[pltpu.VMEM((N_HEADS, nw), F32)],
        compiler_params=_cparams("arbitrary"),
        name="rel_bias",
    )(table)


ATT_TQ = BAND_PAST
ATT_GQ = 4 * CHUNK
ATT_GK = ATT_GQ + BAND_PAST
ATT_HB = 4
LOG2E = 1.4426950408889634


def _attn_prompt_kernel(q_ref, kp_ref, kc_ref, vp_ref, vc_ref, bias_ref, o_ref, s_ref, p_ref):
    first = pl.program_id(1) == 0
    ng = ATT_TQ // ATT_GQ
    dims = (((1,), (1,)), ((), ()))
    c = ATTN_SCALE * LOG2E
    for u, (hh, g) in enumerate((hh, g) for hh in range(ATT_HB) for g in range(ng)):
        cols = slice(hh * HEAD_DIM, (hh + 1) * HEAD_DIM)
        lo = g * ATT_GQ
        n_prev = ATT_TQ - lo
        variant = jnp.where(first, 1 + g, 0)
        q = q_ref[lo:lo + ATT_GQ, cols]
        s_prev = lax.dot_general(q, kp_ref[lo:, cols], dims, preferred_element_type=F32)
        s_ref[u, :, :n_prev] = s_prev * c + bias_ref[variant, hh, :, :n_prev]
        s_cur = lax.dot_general(q, kc_ref[:ATT_GK - n_prev, cols], dims, preferred_element_type=F32)
        s_ref[u, :, n_prev:] = s_cur * c + bias_ref[variant, hh, :, n_prev:]

        x = s_ref[u]
        e = jnp.exp2(x - jnp.max(x, axis=-1, keepdims=True))
        den = jnp.sum(e, axis=-1, keepdims=True)
        p_ref[u] = e.astype(BF16)

        pv = (jnp.dot(p_ref[u, :, :n_prev], vp_ref[lo:, cols], preferred_element_type=F32)
              + jnp.dot(p_ref[u, :, n_prev:], vc_ref[:ATT_GK - n_prev, cols], preferred_element_type=F32))
        o_ref[lo:lo + ATT_GQ, cols] = (pv / den).astype(o_ref.dtype)


def _attn_prompt(qkv, bias):
    n = qkv.shape[0]
    width = ATT_HB * HEAD_DIM
    nhb = D_MODEL // width
    n_units = ATT_HB * (ATT_TQ // ATT_GQ)
    assert bias.shape[0] == 1 + ATT_TQ // ATT_GQ

    def prev(i):
        return jnp.maximum(i - 1, 0)

    return pl.pallas_call(
        _attn_prompt_kernel,
        grid=(nhb, n // ATT_TQ),
        in_specs=[
            pl.BlockSpec((ATT_TQ, width), lambda h, i: (i, h)),
            pl.BlockSpec((ATT_TQ, width), lambda h, i: (prev(i), nhb + h)),
            pl.BlockSpec((ATT_TQ, width), lambda h, i: (i, nhb + h)),
            pl.BlockSpec((ATT_TQ, width), lambda h, i: (prev(i), 2 * nhb + h)),
            pl.BlockSpec((ATT_TQ, width), lambda h, i: (i, 2 * nhb + h)),
            pl.BlockSpec((bias.shape[0], ATT_HB, ATT_GQ, ATT_GK), lambda h, i: (0, h, 0, 0)),
        ],
        out_specs=pl.BlockSpec((ATT_TQ, width), lambda h, i: (i, h)),
        out_shape=jax.ShapeDtypeStruct((n, D_MODEL), BF16),
        scratch_shapes=[pltpu.VMEM((n_units, ATT_GQ, ATT_GK), F32), pltpu.VMEM((n_units, ATT_GQ, ATT_GK), BF16)],
        compiler_params=_cparams("parallel", "parallel"),
        name="attn_prompt",
    )(qkv, qkv, qkv, qkv, qkv, bias)


def _attn_sample_kernel(q_ref, kn_ref, vn_ref, ck_ref, cv_ref, bias_ref, o_ref, *, ncache):
    s_len = q_ref.shape[0]
    for h in range(N_HEADS):
        cols = slice(h * HEAD_DIM, (h + 1) * HEAD_DIM)
        q = q_ref[:, cols]
        k_old = ck_ref[pl.ds(h, ncache, stride=N_HEADS), :].astype(BF16)
        v_old = cv_ref[pl.ds(h, ncache, stride=N_HEADS), :].astype(BF16)
        k_new, v_new = kn_ref[:, cols], vn_ref[:, cols]
        dims = (((1,), (1,)), ((), ()))
        s_old = lax.dot_general(q, k_old, dims, preferred_element_type=F32) * ATTN_SCALE + bias_ref[h, :, :ncache]
        s_new = (lax.dot_general(q, k_new, dims, preferred_element_type=F32) * ATTN_SCALE
                 + bias_ref[h, :, ncache:ncache + s_len])
        m = jnp.maximum(jnp.max(s_old, axis=-1, keepdims=True), jnp.max(s_new, axis=-1, keepdims=True))
        e_old, e_new = jnp.exp(s_old - m), jnp.exp(s_new - m)
        den = jnp.sum(e_old, axis=-1, keepdims=True) + jnp.sum(e_new, axis=-1, keepdims=True)
        pv = (jnp.dot(e_old.astype(BF16), v_old, preferred_element_type=F32)
              + jnp.dot(e_new.astype(BF16), v_new, preferred_element_type=F32))
        o_ref[:, cols] = (pv / den).astype(o_ref.dtype)


def _attn_sample(qkv, cache_k, cache_v, bias, bsz, s_len):
    ncache = cache_k.shape[1] // N_HEADS
    return pl.pallas_call(
        functools.partial(_attn_sample_kernel, ncache=ncache),
        grid=(bsz,),
        in_specs=[
            pl.BlockSpec((s_len, D_MODEL), lambda b: (b, 0)),
            pl.BlockSpec((s_len, D_MODEL), lambda b: (b, 1)),
            pl.BlockSpec((s_len, D_MODEL), lambda b: (b, 2)),
            pl.BlockSpec((None, ncache * N_HEADS, HEAD_DIM), lambda b: (b, 0, 0)),
            pl.BlockSpec((None, ncache * N_HEADS, HEAD_DIM), lambda b: (b, 0, 0)),
            pl.BlockSpec((None,) + bias.shape[1:], lambda b: (0, 0, 0, 0)),
        ],
        out_specs=pl.BlockSpec((s_len, D_MODEL), lambda b: (b, 0)),
        out_shape=jax.ShapeDtypeStruct((bsz * s_len, D_MODEL), BF16),
        compiler_params=_cparams("parallel"),
        name="attn_sample",
    )(qkv, qkv, qkv, cache_k, cache_v, bias)


def kernel(x_prompt, x_sample, state_s5_re, state_s5_im, cache_k, cache_v, p_prompt, p_sample, ffn1_w_in, ffn1_w_out, ffn2_w_in, ffn2_w_out, ln_g, ln_b, ple_w_proj, ple_w_gate, s5_a_re, s5_a_im, s5_log_dt, s5_b_re, s5_b_im, s5_c_re, s5_c_im, s5_d, s5_w_glu, attn_w_qkv, attn_w_o, attn_rel_bias):
    bsz_p, seq, _ = x_prompt.shape
    bsz_s, s_len, _ = x_sample.shape
    assert bsz_p == 1 and s_len % S5_CHUNK == 0 and seq % (S5_CHUNK * 512) == 0
    n_p, n_s = bsz_p * seq, bsz_s * s_len
    tm_glu, tm_s = 512, n_s
    tm_ffn, tm_proj, tn_wide = 1024, 1024, 1024

    xp = x_prompt.reshape(n_p, D_MODEL)
    xs = x_sample.reshape(n_s, D_MODEL)
    pp = p_prompt.reshape(DEPTH, n_p, PLE_DIM)
    ps = p_sample.reshape(DEPTH, n_s, PLE_DIM)
    w1 =(ffn1_w_in[0].astype(BF16), ffn1_w_out[0].astype(BF16))

    def norm(i, slot):
        return ln_g[i, slot].reshape(1, D_MODEL), ln_b[i, slot].reshape(1, D_MODEL)

    outs = {}
    for i in range(DEPTH):
        xp, *w2 = _ffn_ln(xp, *w1, *norm(i, 0), tm_ffn, next_w=(ffn2_w_in, ffn2_w_out, i))
        xs = _ffn_ln(xs, *w1, *norm(i, 0), tm_s)
        if i % N_MIXERS == 0:
            mw, v, at = _s5_prep(s5_a_re, s5_a_im, s5_log_dt, s5_b_re, s5_b_im, s5_c_re, s5_c_im)
            d = s5_d.reshape(S5_CB, 1, LANES)
            later = (s5_w_glu, attn_w_qkv, attn_w_o, ple_w_gate.reshape(DEPTH * D_MODEL, D_MODEL),
                     ple_w_proj.reshape(DEPTH * PLE_DIM, D_MODEL))
            y, h_fin, w_glu, w_qkv, w_o, w_gate, w_proj = _s5_prompt(xp, mw, v, at, d, tb=512, cast=later)
            w_gate = w_gate.reshape(DEPTH, D_MODEL, D_MODEL)
            w_proj = w_proj.reshape(DEPTH, PLE_DIM, D_MODEL)
            xp = _glu_ln(xp, y, w_glu, *norm(i, 1), tm_glu)
            outs["s5_p"] = (h_fin[:, 0, :S5_SPB].reshape(bsz_p, S5_GROUPS, S5_STATE),
                            h_fin[:, 0, S5_SPB:].reshape(bsz_p, S5_GROUPS, S5_STATE))
            nblk = s_len // S5_CHUNK
            u = (xs.reshape(bsz_s, nblk, S5_CHUNK, S5_CB, LANES).transpose(3, 1, 0, 2, 4)
                 .reshape(S5_CB, nblk, bsz_s, S5_UW))
            ys, hs_re, hs_im = _s5_sample(u, state_s5_re.reshape(bsz_s, -1), state_s5_im.reshape(bsz_s, -1),
                                          mw, v, at, d)
            ys = (ys.reshape(S5_CB, nblk, bsz_s, S5_CHUNK, LANES).transpose(2, 1, 3, 0, 4)
                  .reshape(n_s, D_MODEL))
            xs = _glu_ln(xs, ys, w_glu, *norm(i, 1), tm_s)
            outs["s5_s"] = (hs_re.reshape(bsz_s, S5_GROUPS, S5_STATE), hs_im.reshape(bsz_s, S5_GROUPS, S5_STATE))
        else:
            rows = min(BAND_PAST, seq)
            assert seq % ATT_TQ == 0
            qkv_p = _proj(xp, w_qkv, 0, 3 * D_MODEL, BF16, tm_proj, tn_wide)
            kv_tail = _proj(xp, w_qkv, D_MODEL, 2 * D_MODEL, F32, rows, tn_wide, row0=n_p - rows, nrows=rows)
            first_tile_invalid = tuple(ATT_TQ - g * ATT_GQ for g in range(ATT_TQ // ATT_GQ))
            bias_p = _rel_bias(attn_rel_bias, BAND_PAST, ATT_GQ, ATT_GK, scale=LOG2E, band=True,
                               invalid=(0,) + first_tile_invalid)
            att_p = _attn_prompt(qkv_p, bias_p)
            xp = _wo_ln(xp, att_p, w_o, *norm(i, 1), tm_ffn)
            outs["kv_p"] = (kv_tail[:, :D_MODEL].reshape(bsz_p, rows, N_HEADS, HEAD_DIM),
                            kv_tail[:, D_MODEL:].reshape(bsz_p, rows, N_HEADS, HEAD_DIM))

            ncache = cache_k.shape[1]
            qkv_s = _proj(xs, w_qkv, 0, 3 * D_MODEL, BF16, tm_s, tn_wide)
            kv_s = _proj(xs, w_qkv, D_MODEL, 2 * D_MODEL, F32, tm_s, tn_wide)
            nk_pad = -(-(ncache + s_len) // LANES) * LANES
            bias_s = _rel_bias(attn_rel_bias, ncache, s_len, nk_pad)
            att_s = _attn_sample(qkv_s, cache_k.reshape(bsz_s, ncache * N_HEADS, HEAD_DIM),
                                 cache_v.reshape(bsz_s, ncache * N_HEADS, HEAD_DIM), bias_s, bsz_s, s_len)
            xs = _wo_ln(xs, att_s, w_o, *norm(i, 1), tm_s)
            outs["kv_s"] = (kv_s[:, :D_MODEL].reshape(bsz_s, s_len, N_HEADS, HEAD_DIM),
                            kv_s[:, D_MODEL:].reshape(bsz_s, s_len, N_HEADS, HEAD_DIM))
        if i + 1 < DEPTH:
            xp, *w1 = _ffn_ln(xp, *w2, *norm(i, 2), tm_ffn, next_w=(ffn1_w_in, ffn1_w_out, i + 1))
        else:
            xp = _ffn_ln(xp, *w2, *norm(i, 2), tm_ffn)
        xs = _ffn_ln(xs, *w2, *norm(i, 2), tm_s)
        xp = _ple_ln(xp, pp, w_proj, w_gate, i, *norm(i, 3), tm_ffn)
        xs = _ple_ln(xs, ps, w_proj, w_gate, i, *norm(i, 3), tm_s)

    return (xp.reshape(bsz_p, seq, D_MODEL), xs.reshape(bsz_s, s_len, D_MODEL),
            *outs["s5_p"], *outs["kv_p"], *outs["s5_s"], *outs["kv_s"])
```

```python
import functools

import jax
import jax.numpy as jnp
from jax import lax
from jax.experimental import pallas as pl
from jax.experimental.pallas import tpu as pltpu

F32 = jnp.float32
BF16 = jnp.bfloat16

D_MODEL = 2048
DEPTH = 2
N_MIXERS = 2
CHUNK = 64
S5_GROUP = 16
S5_GROUPS = D_MODEL // S5_GROUP
S5_STATE = 64
N_HEADS = 16
HEAD_DIM = D_MODEL // N_HEADS
PAST_CHUNKS = 8
BAND_PAST = PAST_CHUNKS * CHUNK
BAND = BAND_PAST + CHUNK
REL_CLIP = 128
ATTN_SCALE = HEAD_DIM ** -0.5
NEG_INF = -1e30
D_FF = 5632
PLE_DIM = 256
DN_ALPHA = (2 * DEPTH) ** 0.25
LN_EPS = 1e-5

LANES = 128
MXU_DIM = 256
VMEM_LIMIT = 60 * 1024 * 1024

TF = 512
FFN_SUB = MXU_DIM
FFN_OUT = 512
LN_ROWS = 128
COL_TILE = 512

S5_CB = D_MODEL // LANES
S5_GPB = LANES // S5_GROUP
S5_SPB = S5_GPB * S5_STATE
S5_CHUNK = 8
S5_UW = S5_CHUNK * LANES


def _cparams(*sem):
    return pltpu.CompilerParams(dimension_semantics=sem, vmem_limit_bytes=VMEM_LIMIT)


def _layer_norm(v, g, b, eps=LN_EPS):
    mu = jnp.mean(v, axis=-1, keepdims=True)
    d = v - mu
    var = jnp.mean(d * d, axis=-1, keepdims=True)
    return d * lax.rsqrt(var + eps) * g + b


def _norm_rows(o_ref, g_ref, b_ref, eps=LN_EPS):
    rows = min(LN_ROWS, o_ref.shape[0])
    for r in range(o_ref.shape[0] // rows):
        rs = slice(r * rows, (r + 1) * rows)
        o_ref[rs, :] = _layer_norm(o_ref[rs, :], g_ref[...], b_ref[...], eps)


def _resident(block_shape, index_map):
    return pl.BlockSpec(block_shape, index_map, pipeline_mode=pl.Buffered(1))


def _ffn_kernel(*refs, nj, cast_next):
    if cast_next:
        x_ref, wg_ref, wu_ref, wo_ref, g_ref, b_ref, nin_ref, nout_ref, o_ref, cin_ref, cout_ref, xb_ref = refs
        cin_ref[...] = nin_ref[...].astype(BF16)
        cout_ref[...] = nout_ref[...].astype(BF16)
    else:
        x_ref, wg_ref, wu_ref, wo_ref, g_ref, b_ref, o_ref, xb_ref = refs
    j = pl.program_id(1)

    @pl.when(j == 0)
    def _():
        x = x_ref[...]
        xb_ref[...] = x.astype(BF16)
        o_ref[...] = (2.0 * DN_ALPHA) * x

    xb = xb_ref[...]
    acts = []
    for h in range(TF // FFN_SUB):
        cs = slice(h * FFN_SUB, (h + 1) * FFN_SUB)
        gate = jnp.dot(xb, wg_ref[:, cs], preferred_element_type=F32)
        up = jnp.dot(xb, wu_ref[:, cs], preferred_element_type=F32)
        acts.append((gate * jax.nn.sigmoid(gate) * up).astype(BF16))
    act = jnp.concatenate(acts, axis=1)
    for n in range(D_MODEL // FFN_OUT):
        ns = slice(n * FFN_OUT, (n + 1) * FFN_OUT)
        o_ref[:, ns] += jnp.dot(act, wo_ref[:, ns], preferred_element_type=F32)

    @pl.when(j == nj - 1)
    def _():
        _norm_rows(o_ref, g_ref, b_ref, eps=4.0 * LN_EPS)


def _ffn_ln(x, w_in, w_out, g, b, tm, next_w=None):
    n = x.shape[0]
    ni, nj = n // tm, D_FF // TF
    in_specs = [
        pl.BlockSpec((tm, D_MODEL), lambda i, j: (i, 0)),
        pl.BlockSpec((D_MODEL, TF), lambda i, j: (0, j)),
        pl.BlockSpec((D_MODEL, TF), lambda i, j: (0, j + nj)),
        pl.BlockSpec((TF, D_MODEL), lambda i, j: (j, 0)),
        pl.BlockSpec((1, D_MODEL), lambda i, j: (0, 0)),
        pl.BlockSpec((1, D_MODEL), lambda i, j: (0, 0)),
    ]
    out_specs = [pl.BlockSpec((tm, D_MODEL), lambda i, j: (i, 0))]
    out_shape = [jax.ShapeDtypeStruct((n, D_MODEL), F32)]
    args = [x, w_in, w_in, w_out, g, b]
    if next_w is not None:
        nw_in, nw_out, layer = next_w
        assert D_MODEL % ni == 0 and (2 * D_FF) % nj == 0 and D_FF % nj == 0
        in_blk = (D_MODEL // ni, 2 * D_FF // nj)
        out_blk = (D_FF // nj, D_MODEL // ni)
        in_specs += [pl.BlockSpec((None,) + in_blk, lambda i, j: (layer, i, j)),
                     pl.BlockSpec((None,) + out_blk, lambda i, j: (layer, j, i))]
        out_specs += [pl.BlockSpec(in_blk, lambda i, j: (i, j)), pl.BlockSpec(out_blk, lambda i, j: (j, i))]
        out_shape += [jax.ShapeDtypeStruct(nw_in.shape[1:], BF16), jax.ShapeDtypeStruct(nw_out.shape[1:], BF16)]
        args += [nw_in, nw_out]
    res = pl.pallas_call(
        functools.partial(_ffn_kernel, nj=nj, cast_next=next_w is not None),
        grid=(ni, nj),
        in_specs=in_specs,
        out_specs=out_specs,
        out_shape=out_shape,
        scratch_shapes=[pltpu.VMEM((tm, D_MODEL), BF16)],
        compiler_params=_cparams("parallel", "arbitrary"),
        name="ffn_ln",
    )(*args)
    return res[0] if next_w is None else res


def _ple_kernel(x_ref, p_ref, wp_ref, wg_ref, g_ref, b_ref, o_ref):
    xb = x_ref[...].astype(BF16)
    pb = p_ref[...].astype(BF16)
    for n in range(D_MODEL // COL_TILE):
        ns = slice(n * COL_TILE, (n + 1) * COL_TILE)
        proj = jnp.dot(pb, wp_ref[:, ns], preferred_element_type=F32)
        gate = jnp.dot(xb, wg_ref[:, ns], preferred_element_type=F32)
        o_ref[:, ns] = DN_ALPHA * x_ref[:, ns] + proj * jax.nn.sigmoid(gate)
    _norm_rows(o_ref, g_ref, b_ref)


def _ple_ln(x, p, w_proj, w_gate, layer, g, b, tm):
    n = x.shape[0]
    return pl.pallas_call(
        _ple_kernel,
        grid=(n // tm,),
        in_specs=[
            pl.BlockSpec((tm, D_MODEL), lambda i: (i, 0)),
            pl.BlockSpec((None, tm, PLE_DIM), lambda i: (layer, i, 0)),
            _resident((None, PLE_DIM, D_MODEL), lambda i: (layer, 0, 0)),
            _resident((None, D_MODEL, D_MODEL), lambda i: (layer, 0, 0)),
            _resident((1, D_MODEL), lambda i: (0, 0)),
            _resident((1, D_MODEL), lambda i: (0, 0)),
        ],
        out_specs=pl.BlockSpec((tm, D_MODEL), lambda i: (i, 0)),
        out_shape=jax.ShapeDtypeStruct((n, D_MODEL), F32),
        compiler_params=_cparams("parallel"),
        name="ple_ln",
    )(x, p, w_proj, w_gate, g, b)


def _s5_discretize(ar, ai, log_dt):
    dt = jnp.exp(log_dt)
    mag = jnp.exp(ar * dt)
    ab_re, ab_im = mag * jnp.cos(ai * dt), mag * jnp.sin(ai * dt)
    den = ar * ar + ai * ai
    nr, ni = ab_re - 1.0, ab_im
    return ab_re, ab_im, (nr * ar + ni * ai) / den, (ni * ar - nr * ai) / den


def _s5_prep_kernel(ar_r, ai_r, ldt_r, ar_c, ai_c, ldt_c, bre_ref, bim_ref, cre_ref, cim_ref,
                    mw_ref, v_ref, at_ref):
    ab_re, ab_im, cf_re, cf_im = _s5_discretize(ar_r[...], ai_r[...], ldt_r[...])
    b_re, b_im = bre_ref[...], bim_ref[...]
    c_re, c_im = cre_ref[...], cim_ref[...]
    p_re = cf_re * b_re - cf_im * b_im
    p_im = cf_re * b_im + cf_im * b_re
    ak_re, ak_im = jnp.ones_like(ab_re), jnp.zeros_like(ab_im)
    pows = []
    for _ in range(S5_CHUNK):
        pows.append((p_re, p_im))
        p_re, p_im = ab_re * p_re - ab_im * p_im, ab_re * p_im + ab_im * p_re
        ak_re, ak_im = ab_re * ak_re - ab_im * ak_im, ab_re * ak_im + ab_im * ak_re
    at_ref[:, :S5_SPB] = ak_re
    at_ref[:, S5_SPB:] = ak_im

    taps = (jnp.dot(jnp.concatenate([p[0] for p in pows], axis=0), c_re,
                    precision=lax.Precision.HIGHEST, preferred_element_type=F32)
            - jnp.dot(jnp.concatenate([p[1] for p in pows], axis=0), c_im,
                      precision=lax.Precision.HIGHEST, preferred_element_type=F32)).astype(BF16)
    zero = jnp.zeros((LANES, LANES), BF16)
    for s in range(S5_CHUNK):
        rows = slice(s * LANES, (s + 1) * LANES)
        for t in range(S5_CHUNK):
            lag = slice((t - s) * LANES, (t - s + 1) * LANES)
            mw_ref[rows, t * LANES:(t + 1) * LANES] = taps[lag, :] if t >= s else zero
        w_re, w_im = pows[S5_CHUNK - 1 - s]
        mw_ref[rows, S5_UW:S5_UW + S5_SPB] = w_re.astype(BF16)
        mw_ref[rows, S5_UW + S5_SPB:] = w_im.astype(BF16)

    ac_re, ac_im, _, _ = _s5_discretize(ar_c[...], ai_c[...], ldt_c[...])
    ck_re, ck_im = ac_re, ac_im
    for t in range(S5_CHUNK):
        cols = slice(t * LANES, (t + 1) * LANES)
        v_ref[:S5_SPB, cols] = (ck_re * c_re - ck_im * c_im).astype(BF16)
        v_ref[S5_SPB:, cols] = (-(ck_im * c_re + ck_re * c_im)).astype(BF16)
        ck_re, ck_im = ck_re * ac_re - ck_im * ac_im, ck_re * ac_im + ck_im * ac_re


def _s5_prep(a_re, a_im, log_dt, b_re, b_im, c_re, c_im):
    eye = jnp.eye(S5_GPB, dtype=F32)

    def rows(v):
        return v.reshape(S5_CB, 1, S5_SPB)

    def cols(v):
        return v.reshape(S5_CB, S5_SPB, 1)

    def b_blockdiag(w):
        w4 = w.reshape(S5_CB, S5_GPB, S5_STATE, S5_GROUP).transpose(0, 1, 3, 2)
        return (w4[:, :, :, None, :] * eye[None, :, None, :, None]).reshape(S5_CB, LANES, S5_SPB)

    def c_blockdiag(w):
        w4 = w.reshape(S5_CB, S5_GPB, S5_GROUP, S5_STATE).transpose(0, 1, 3, 2)
        return (w4[:, :, :, None, :] * eye[None, :, None, :, None]).reshape(S5_CB, S5_SPB, LANES)

    ldt = jnp.repeat(log_dt, S5_STATE)
    row_spec = pl.BlockSpec((None, 1, S5_SPB), lambda c: (c, 0, 0))
    col_spec = pl.BlockSpec((None, S5_SPB, 1), lambda c: (c, 0, 0))
    b_spec = pl.BlockSpec((None, LANES, S5_SPB), lambda c: (c, 0, 0))
    c_spec = pl.BlockSpec((None, S5_SPB, LANES), lambda c: (c, 0, 0))
    return pl.pallas_call(
        _s5_prep_kernel,
        grid=(S5_CB,),
        in_specs=[row_spec, row_spec, row_spec, col_spec, col_spec, col_spec, b_spec, b_spec, c_spec, c_spec],
        out_specs=[
            pl.BlockSpec((None, S5_UW, S5_UW + 2 * S5_SPB), lambda c: (c, 0, 0)),
            pl.BlockSpec((None, 2 * S5_SPB, S5_UW), lambda c: (c, 0, 0)),
            pl.BlockSpec((None, 1, 2 * S5_SPB), lambda c: (c, 0, 0)),
        ],
        out_shape=[
            jax.ShapeDtypeStruct((S5_CB, S5_UW, S5_UW + 2 * S5_SPB), BF16),
            jax.ShapeDtypeStruct((S5_CB, 2 * S5_SPB, S5_UW), BF16),
            jax.ShapeDtypeStruct((S5_CB, 1, 2 * S5_SPB), F32),
        ],
        compiler_params=_cparams("parallel"),
        name="s5_prep",
    )(rows(a_re), rows(a_im), rows(ldt), cols(a_re), cols(a_im), cols(ldt),
      b_blockdiag(b_re), b_blockdiag(b_im), c_blockdiag(c_re), c_blockdiag(c_im))


def _s5_step(at_ref, h_re, h_im, s_re, s_im):
    a_re, a_im = at_ref[:, :S5_SPB], at_ref[:, S5_SPB:]
    return a_re * h_re - a_im * h_im + s_re, a_re * h_im + a_im * h_re + s_im


def _s5_prompt_kernel(*refs, tb, nr, n_cast):
    x_ref, mw_ref, v_ref, at_ref, d_ref = refs[:5]
    w32_refs = refs[5:5 + n_cast]
    y_ref, hfin_ref = refs[5 + n_cast:7 + n_cast]
    w16_refs = refs[7 + n_cast:7 + 2 * n_cast]
    hc_ref, s_ref, h_ref = refs[7 + 2 * n_cast:]
    for w32, w16 in zip(w32_refs, w16_refs):
        w16[...] = w32[...].astype(BF16)
    r = pl.program_id(1)

    @pl.when(r == 0)
    def _():
        hc_ref[...] = jnp.zeros_like(hc_ref)

    u = jnp.concatenate([x_ref[pl.ds(s, tb, stride=S5_CHUNK), :] for s in range(S5_CHUNK)], axis=1)
    ub = u.astype(BF16)
    s_ref[...] = jnp.dot(ub, mw_ref[:, S5_UW:], preferred_element_type=F32)

    def body(i, h):
        h_re, h_im = h
        row = pl.ds(i, 1)
        h_ref[row, :S5_SPB] = h_re
        h_ref[row, S5_SPB:] = h_im
        return _s5_step(at_ref, h_re, h_im, s_ref[row, :S5_SPB], s_ref[row, S5_SPB:])

    h_re, h_im = lax.fori_loop(0, tb, body, (hc_ref[:, :S5_SPB], hc_ref[:, S5_SPB:]), unroll=8)
    hc_ref[:, :S5_SPB] = h_re
    hc_ref[:, S5_SPB:] = h_im

    hb = h_ref[...].astype(BF16)
    steps = MXU_DIM // LANES
    for c in range(S5_UW // MXU_DIM):
        cols = slice(c * MXU_DIM, (c + 1) * MXU_DIM)
        k = (c + 1) * MXU_DIM
        y = (jnp.dot(ub[:, :k], mw_ref[:k, cols], preferred_element_type=F32)
             + jnp.dot(hb, v_ref[:, cols], preferred_element_type=F32)
             + jnp.concatenate([d_ref[...]] * steps, axis=1) * u[:, cols])
        for t in range(steps):
            y_ref[pl.ds(c * steps + t, tb, stride=S5_CHUNK), :] = y[:, t * LANES:(t + 1) * LANES]

    @pl.when(r == nr - 1)
    def _():
        hfin_ref[...] = hc_ref[...]


def _s5_prompt(x, mw, v, at, d, tb, cast=()):
    n = x.shape[0]
    rows = tb * S5_CHUNK
    nr = n // rows
    cast_blocks = []
    for w in cast:
        assert w.shape[0] % (S5_CB * 16) == 0 and w.shape[1] % (nr * LANES) == 0
        cast_blocks.append(pl.BlockSpec((w.shape[0] // S5_CB, w.shape[1] // nr), lambda c, r: (c, r)))
    return pl.pallas_call(
        functools.partial(_s5_prompt_kernel, tb=tb, nr=nr, n_cast=len(cast)),
        grid=(S5_CB, nr),
        in_specs=[
            pl.BlockSpec((rows, LANES), lambda c, r: (r, c)),
            pl.BlockSpec((None, S5_UW, S5_UW + 2 * S5_SPB), lambda c, r: (c, 0, 0)),
            pl.BlockSpec((None, 2 * S5_SPB, S5_UW), lambda c, r: (c, 0, 0)),
            pl.BlockSpec((None, 1, 2 * S5_SPB), lambda c, r: (c, 0, 0)),
            pl.BlockSpec((None, 1, LANES), lambda c, r: (c, 0, 0)),
        ] + cast_blocks,
        out_specs=[
            pl.BlockSpec((rows, LANES), lambda c, r: (r, c)),
            pl.BlockSpec((None, 1, 2 * S5_SPB), lambda c, r: (c, 0, 0)),
        ] + cast_blocks,
        out_shape=[
            jax.ShapeDtypeStruct((n, D_MODEL), F32),
            jax.ShapeDtypeStruct((S5_CB, 1, 2 * S5_SPB), F32),
        ] + [jax.ShapeDtypeStruct(w.shape, BF16) for w in cast],
        scratch_shapes=[pltpu.VMEM((1, 2 * S5_SPB), F32), pltpu.VMEM((tb, 2 * S5_SPB), F32),
                        pltpu.VMEM((tb, 2 * S5_SPB), F32)],
        compiler_params=_cparams("parallel", "arbitrary"),
        name="s5_prompt",
    )(x, mw, v, at, d, *cast)


def _s5_sample_kernel(u_ref, hre_ref, him_ref, mw_ref, v_ref, at_ref, d_ref, y_ref, ore_ref, oim_ref, *, nblk):
    d_row = jnp.concatenate([d_ref[...]] * S5_CHUNK, axis=1)
    h_re, h_im = hre_ref[...], him_ref[...]
    for k in range(nblk):
        u = u_ref[k]
        res = jnp.dot(u.astype(BF16), mw_ref[...], preferred_element_type=F32)
        h = jnp.concatenate([h_re, h_im], axis=1).astype(BF16)
        y_ref[k] = res[:, :S5_UW] + jnp.dot(h, v_ref[...], preferred_element_type=F32) + d_row * u
        h_re, h_im = _s5_step(at_ref, h_re, h_im, res[:, S5_UW:S5_UW + S5_SPB], res[:, S5_UW + S5_SPB:])
    ore_ref[...] = h_re
    oim_ref[...] = h_im


def _s5_sample(u, h_re, h_im, mw, v, at, d):
    _, nblk, bsz, _ = u.shape
    h_spec = pl.BlockSpec((bsz, S5_SPB), lambda c: (0, c))
    u_spec = pl.BlockSpec((None, nblk, bsz, S5_UW), lambda c: (c, 0, 0, 0))
    return pl.pallas_call(
        functools.partial(_s5_sample_kernel, nblk=nblk),
        grid=(S5_CB,),
        in_specs=[
            u_spec, h_spec, h_spec,
            pl.BlockSpec((None, S5_UW, S5_UW + 2 * S5_SPB), lambda c: (c, 0, 0)),
            pl.BlockSpec((None, 2 * S5_SPB, S5_UW), lambda c: (c, 0, 0)),
            pl.BlockSpec((None, 1, 2 * S5_SPB), lambda c: (c, 0, 0)),
            pl.BlockSpec((None, 1, LANES), lambda c: (c, 0, 0)),
        ],
        out_specs=[u_spec, h_spec, h_spec],
        out_shape=[
            jax.ShapeDtypeStruct(u.shape, F32),
            jax.ShapeDtypeStruct(h_re.shape, F32),
            jax.ShapeDtypeStruct(h_im.shape, F32),
        ],
        compiler_params=_cparams("parallel"),
        name="s5_sample",
    )(u, h_re, h_im, mw, v, at, d)


def _glu_kernel(x_ref, y_ref, w_ref, g_ref, b_ref, o_ref):
    z = jax.nn.gelu(y_ref[...]).astype(BF16)
    for n in range(D_MODEL // COL_TILE):
        ns = slice(n * COL_TILE, (n + 1) * COL_TILE)
        gs = slice(D_MODEL + n * COL_TILE, D_MODEL + (n + 1) * COL_TILE)
        za = jnp.dot(z, w_ref[:, ns], preferred_element_type=F32)
        zb = jnp.dot(z, w_ref[:, gs], preferred_element_type=F32)
        o_ref[:, ns] = DN_ALPHA * x_ref[:, ns] + za * jax.nn.sigmoid(zb)
    _norm_rows(o_ref, g_ref, b_ref)


def _glu_ln(x, y, w_glu, g, b, tm):
    n = x.shape[0]
    return pl.pallas_call(
        _glu_kernel,
        grid=(n // tm,),
        in_specs=[
            pl.BlockSpec((tm, D_MODEL), lambda i: (i, 0)),
            pl.BlockSpec((tm, D_MODEL), lambda i: (i, 0)),
            _resident((D_MODEL, 2 * D_MODEL), lambda i: (0, 0)),
            _resident((1, D_MODEL), lambda i: (0, 0)),
            _resident((1, D_MODEL), lambda i: (0, 0)),
        ],
        out_specs=pl.BlockSpec((tm, D_MODEL), lambda i: (i, 0)),
        out_shape=jax.ShapeDtypeStruct((n, D_MODEL), F32),
        compiler_params=_cparams("parallel"),
        name="glu_ln",
    )(x, y, w_glu, g, b)


def _proj_kernel(x_ref, w_ref, o_ref, xb_ref):
    @pl.when(pl.program_id(1) == 0)
    def _():
        xb_ref[...] = x_ref[...].astype(BF16)

    o_ref[...] = jnp.dot(xb_ref[...], w_ref[...], preferred_element_type=F32).astype(o_ref.dtype)


def _proj(x, w, col0, ncols, out_dtype, tm, tn, row0=0, nrows=None):
    nrows = x.shape[0] if nrows is None else nrows
    assert row0 % tm == 0 and col0 % tn == 0
    rb, cb = row0 // tm, col0 // tn
    return pl.pallas_call(
        _proj_kernel,
        grid=(nrows // tm, ncols // tn),
        in_specs=[
            pl.BlockSpec((tm, D_MODEL), lambda i, j: (i + rb, 0)),
            pl.BlockSpec((D_MODEL, tn), lambda i, j: (0, j + cb)),
        ],
        out_specs=pl.BlockSpec((tm, tn), lambda i, j: (i, j)),
        out_shape=jax.ShapeDtypeStruct((nrows, ncols), out_dtype),
        scratch_shapes=[pltpu.VMEM((tm, D_MODEL), BF16)],
        compiler_params=_cparams("parallel", "arbitrary"),
        name="proj",
    )(x, w)


def _wo_kernel(x_ref, a_ref, w_ref, g_ref, b_ref, o_ref):
    a = a_ref[...]
    for n in range(D_MODEL // COL_TILE):
        ns = slice(n * COL_TILE, (n + 1) * COL_TILE)
        o_ref[:, ns] = DN_ALPHA * x_ref[:, ns] + jnp.dot(a, w_ref[:, ns], preferred_element_type=F32)
    _norm_rows(o_ref, g_ref, b_ref)


def _wo_ln(x, a, w_o, g, b, tm):
    n = x.shape[0]
    return pl.pallas_call(
        _wo_kernel,
        grid=(n // tm,),
        in_specs=[
            pl.BlockSpec((tm, D_MODEL), lambda i: (i, 0)),
            pl.BlockSpec((tm, D_MODEL), lambda i: (i, 0)),
            _resident((D_MODEL, D_MODEL), lambda i: (0, 0)),
            _resident((1, D_MODEL), lambda i: (0, 0)),
            _resident((1, D_MODEL), lambda i: (0, 0)),
        ],
        out_specs=pl.BlockSpec((tm, D_MODEL), lambda i: (i, 0)),
        out_shape=jax.ShapeDtypeStruct((n, D_MODEL), F32),
        compiler_params=_cparams("parallel"),
        name="wo_ln",
    )(x, a, w_o, g, b)


def _bias_kernel(tab_ref, o_ref, base_ref, *, q_off, nq, nk, scale, band):
    h = pl.program_id(0)
    ntab, nw = tab_ref.shape[1], base_ref.shape[1]

    @pl.when(h == 0)
    def _():
        off = lax.broadcasted_iota(jnp.int32, (ntab, nw), 1)
        off = jnp.where(off < nk, off, off - nw)
        idx = jnp.clip(q_off - off, -REL_CLIP, REL_CLIP) + REL_CLIP
        onehot = (idx == lax.broadcasted_iota(jnp.int32, (ntab, nw), 0)).astype(F32)
        base_ref[...] = scale * jnp.dot(tab_ref[...], onehot, precision=lax.Precision.HIGHEST,
                                        preferred_element_type=F32)

    bias = pltpu.roll(jnp.broadcast_to(base_ref[pl.ds(h, 1), :], (nq, nw)), 0, 1, stride=1, stride_axis=0)
    if band:
        row = lax.broadcasted_iota(jnp.int32, (nq, nw), 0)
        col = lax.broadcasted_iota(jnp.int32, (nq, nw), 1)
        first = row & ~(CHUNK - 1)
        bias = jnp.where((col >= first) & (col < first + BAND), bias, NEG_INF)
    o_ref[...] = bias[:, :nk]


def _rel_bias(table, q_off, nq, nk, scale=1.0, band=False):
    assert nk % LANES == 0
    ntab = table.shape[1]
    ntab_pad = -(-ntab // LANES) * LANES
    nw = -(-(nq + nk) // LANES) * LANES
    table = jnp.pad(table, ((0, 0), (0, ntab_pad - ntab)))
    return pl.pallas_call(
        functools.partial(_bias_kernel, q_off=q_off, nq=nq, nk=nk, scale=scale, band=band),
        grid=(N_HEADS,),
        in_specs=[_resident((N_HEADS, ntab_pad), lambda h: (0, 0))],
        out_specs=pl.BlockSpec((None, nq, nk), lambda h: (h, 0, 0)),
        out_shape=jax.ShapeDtypeStruct((N_HEADS, nq, nk), F32),
        scratch_shapes=[pltpu.VMEM((N_HEADS, nw), F32)],
        compiler_params=_cparams("arbitrary"),
        name="rel_bias",
    )(table)


ATT_TQ = BAND_PAST
ATT_GQ = 4 * CHUNK
ATT_GK = ATT_GQ + BAND_PAST
ATT_HB = 4
LOG2E = 1.4426950408889634


def _attn_prompt_kernel(q_ref, kp_ref, kc_ref, vp_ref, vc_ref, bias_ref, o_ref, s_ref, p_ref, l_ref):
    i = pl.program_id(0)
    ng = ATT_TQ // ATT_GQ
    dims = (((1,), (1,)), ((), ()))
    units = [(hh, g) for hh in range(ATT_HB) for g in range(ng)]

    def split(g):
        lo = g * ATT_GQ
        return lo, ATT_TQ - lo

    for u, (hh, g) in enumerate(units):
        cols = slice(hh * HEAD_DIM, (hh + 1) * HEAD_DIM)
        lo, n_prev = split(g)
        q = q_ref[lo:lo + ATT_GQ, cols]
        c = ATTN_SCALE * LOG2E
        s_prev = lax.dot_general(q, kp_ref[lo:, cols], dims, preferred_element_type=F32)
        s_ref[u, :, :n_prev] = s_prev * c + bias_ref[hh, :, :n_prev]
        s_cur = lax.dot_general(q, kc_ref[:ATT_GK - n_prev, cols], dims, preferred_element_type=F32)
        s_ref[u, :, n_prev:] = s_cur * c + bias_ref[hh, :, n_prev:]

    @pl.when(i == 0)
    def _():
        for u, (hh, g) in enumerate(units):
            _, n_prev = split(g)
            s_ref[u, :, :n_prev] = jnp.full((ATT_GQ, n_prev), NEG_INF, F32)

    for u, (hh, g) in enumerate(units):
        x = s_ref[u]
        e = jnp.exp2(x - jnp.max(x, axis=-1, keepdims=True))
        l_ref[u] = jnp.sum(e, axis=-1, keepdims=True)
        p_ref[u] = e.astype(BF16)

    for u, (hh, g) in enumerate(units):
        cols = slice(hh * HEAD_DIM, (hh + 1) * HEAD_DIM)
        lo, n_prev = split(g)
        pv = (jnp.dot(p_ref[u, :, :n_prev], vp_ref[lo:, cols], preferred_element_type=F32)
              + jnp.dot(p_ref[u, :, n_prev:], vc_ref[:ATT_GK - n_prev, cols], preferred_element_type=F32))
        o_ref[lo:lo + ATT_GQ, cols] = (pv / l_ref[u]).astype(o_ref.dtype)


def _attn_prompt(qkv, bias):
    n = qkv.shape[0]
    width = ATT_HB * HEAD_DIM
    nhb = D_MODEL // width
    n_units = ATT_HB * (ATT_TQ // ATT_GQ)

    def prev(i):
        return jnp.maximum(i - 1, 0)

    return pl.pallas_call(
        _attn_prompt_kernel,
        grid=(n // ATT_TQ, nhb),
        in_specs=[
            pl.BlockSpec((ATT_TQ, width), lambda i, h: (i, h)),
            pl.BlockSpec((ATT_TQ, width), lambda i, h: (prev(i), nhb + h)),
            pl.BlockSpec((ATT_TQ, width), lambda i, h: (i, nhb + h)),
            pl.BlockSpec((ATT_TQ, width), lambda i, h: (prev(i), 2 * nhb + h)),
            pl.BlockSpec((ATT_TQ, width), lambda i, h: (i, 2 * nhb + h)),
            pl.BlockSpec((ATT_HB, ATT_GQ, ATT_GK), lambda i, h: (h, 0, 0)),
        ],
        out_specs=pl.BlockSpec((ATT_TQ, width), lambda i, h: (i, h)),
        out_shape=jax.ShapeDtypeStruct((n, D_MODEL), BF16),
        scratch_shapes=[pltpu.VMEM((n_units, ATT_GQ, ATT_GK), F32), pltpu.VMEM((n_units, ATT_GQ, ATT_GK), BF16),
                        pltpu.VMEM((n_units, ATT_GQ, 1), F32)],
        compiler_params=_cparams("parallel", "parallel"),
        name="attn_prompt",
    )(qkv, qkv, qkv, qkv, qkv, bias)


def _attn_sample_kernel(q_ref, kn_ref, vn_ref, ck_ref, cv_ref, bias_ref, o_ref, *, ncache):
    s_len = q_ref.shape[0]
    for h in range(N_HEADS):
        cols = slice(h * HEAD_DIM, (h + 1) * HEAD_DIM)
        q = q_ref[:, cols]
        k_old = ck_ref[pl.ds(h, ncache, stride=N_HEADS), :].astype(BF16)
        v_old = cv_ref[pl.ds(h, ncache, stride=N_HEADS), :].astype(BF16)
        k_new, v_new = kn_ref[:, cols], vn_ref[:, cols]
        dims = (((1,), (1,)), ((), ()))
        s_old = lax.dot_general(q, k_old, dims, preferred_element_type=F32) * ATTN_SCALE + bias_ref[h, :, :ncache]
        s_new = (lax.dot_general(q, k_new, dims, preferred_element_type=F32) * ATTN_SCALE
                 + bias_ref[h, :, ncache:ncache + s_len])
        m = jnp.maximum(jnp.max(s_old, axis=-1, keepdims=True), jnp.max(s_new, axis=-1, keepdims=True))
        e_old, e_new = jnp.exp(s_old - m), jnp.exp(s_new - m)
        den = jnp.sum(e_old, axis=-1, keepdims=True) + jnp.sum(e_new, axis=-1, keepdims=True)
        pv = (jnp.dot(e_old.astype(BF16), v_old, preferred_element_type=F32)
              + jnp.dot(e_new.astype(BF16), v_new, preferred_element_type=F32))
        o_ref[:, cols] = (pv / den).astype(o_ref.dtype)


def _attn_sample(qkv, cache_k, cache_v, bias, bsz, s_len):
    ncache = cache_k.shape[1] // N_HEADS
    return pl.pallas_call(
        functools.partial(_attn_sample_kernel, ncache=ncache),
        grid=(bsz,),
        in_specs=[
            pl.BlockSpec((s_len, D_MODEL), lambda b: (b, 0)),
            pl.BlockSpec((s_len, D_MODEL), lambda b: (b, 1)),
            pl.BlockSpec((s_len, D_MODEL), lambda b: (b, 2)),
            pl.BlockSpec((None, ncache * N_HEADS, HEAD_DIM), lambda b: (b, 0, 0)),
            pl.BlockSpec((None, ncache * N_HEADS, HEAD_DIM), lambda b: (b, 0, 0)),
            pl.BlockSpec(bias.shape, lambda b: (0, 0, 0)),
        ],
        out_specs=pl.BlockSpec((s_len, D_MODEL), lambda b: (b, 0)),
        out_shape=jax.ShapeDtypeStruct((bsz * s_len, D_MODEL), BF16),
        compiler_params=_cparams("parallel"),
        name="attn_sample",
    )(qkv, qkv, qkv, cache_k, cache_v, bias)


def kernel(x_prompt, x_sample, state_s5_re, state_s5_im, cache_k, cache_v, p_prompt, p_sample, ffn1_w_in, ffn1_w_out, ffn2_w_in, ffn2_w_out, ln_g, ln_b, ple_w_proj, ple_w_gate, s5_a_re, s5_a_im, s5_log_dt, s5_b_re, s5_b_im, s5_c_re, s5_c_im, s5_d, s5_w_glu, attn_w_qkv, attn_w_o, attn_rel_bias):
    bsz_p, seq, _ = x_prompt.shape
    bsz_s, s_len, _ = x_sample.shape
    assert bsz_p == 1 and s_len % S5_CHUNK == 0 and seq % (S5_CHUNK * 512) == 0
    n_p, n_s = bsz_p * seq, bsz_s * s_len
    tm_glu, tm_s = 512, n_s
    tm_ffn, tm_proj, tn_wide = 1024, 1024, 1024

    xp = x_prompt.reshape(n_p, D_MODEL)
    xs = x_sample.reshape(n_s, D_MODEL)
    pp = p_prompt.reshape(DEPTH, n_p, PLE_DIM)
    ps = p_sample.reshape(DEPTH, n_s, PLE_DIM)
    w1 =(ffn1_w_in[0].astype(BF16), ffn1_w_out[0].astype(BF16))

    def norm(i, slot):
        return ln_g[i, slot].reshape(1, D_MODEL), ln_b[i, slot].reshape(1, D_MODEL)

    outs = {}
    for i in range(DEPTH):
        xp, *w2 = _ffn_ln(xp, *w1, *norm(i, 0), tm_ffn, next_w=(ffn2_w_in, ffn2_w_out, i))
        xs = _ffn_ln(xs, *w1, *norm(i, 0), tm_s)
        if i % N_MIXERS == 0:
            mw, v, at = _s5_prep(s5_a_re, s5_a_im, s5_log_dt, s5_b_re, s5_b_im, s5_c_re, s5_c_im)
            d = s5_d.reshape(S5_CB, 1, LANES)
            later = (s5_w_glu, attn_w_qkv, attn_w_o, ple_w_gate.reshape(DEPTH * D_MODEL, D_MODEL),
                     ple_w_proj.reshape(DEPTH * PLE_DIM, D_MODEL))
            y, h_fin, w_glu, w_qkv, w_o, w_gate, w_proj = _s5_prompt(xp, mw, v, at, d, tb=512, cast=later)
            w_gate = w_gate.reshape(DEPTH, D_MODEL, D_MODEL)
            w_proj = w_proj.reshape(DEPTH, PLE_DIM, D_MODEL)
            xp = _glu_ln(xp, y, w_glu, *norm(i, 1), tm_glu)
            outs["s5_p"] = (h_fin[:, 0, :S5_SPB].reshape(bsz_p, S5_GROUPS, S5_STATE),
                            h_fin[:, 0, S5_SPB:].reshape(bsz_p, S5_GROUPS, S5_STATE))
            nblk = s_len // S5_CHUNK
            u = (xs.reshape(bsz_s, nblk, S5_CHUNK, S5_CB, LANES).transpose(3, 1, 0, 2, 4)
                 .reshape(S5_CB, nblk, bsz_s, S5_UW))
            ys, hs_re, hs_im = _s5_sample(u, state_s5_re.reshape(bsz_s, -1), state_s5_im.reshape(bsz_s, -1),
                                          mw, v, at, d)
            ys = (ys.reshape(S5_CB, nblk, bsz_s, S5_CHUNK, LANES).transpose(2, 1, 3, 0, 4)
                  .reshape(n_s, D_MODEL))
            xs = _glu_ln(xs, ys, w_glu, *norm(i, 1), tm_s)
            outs["s5_s"] = (hs_re.reshape(bsz_s, S5_GROUPS, S5_STATE), hs_im.reshape(bsz_s, S5_GROUPS, S5_STATE))
        else:
            rows = min(BAND_PAST, seq)
            assert seq % ATT_TQ == 0
            qkv_p = _proj(xp, w_qkv, 0, 3 * D_MODEL, BF16, tm_proj, 2 * tn_wide)
            kv_tail = _proj(xp, w_qkv, D_MODEL, 2 * D_MODEL, F32, rows, tn_wide, row0=n_p - rows, nrows=rows)
            bias_p = _rel_bias(attn_rel_bias, BAND_PAST, ATT_GQ, ATT_GK, scale=LOG2E, band=True)
            att_p = _attn_prompt(qkv_p, bias_p)
            xp = _wo_ln(xp, att_p, w_o, *norm(i, 1), tm_ffn)
            outs["kv_p"] = (kv_tail[:, :D_MODEL].reshape(bsz_p, rows, N_HEADS, HEAD_DIM),
                            kv_tail[:, D_MODEL:].reshape(bsz_p, rows, N_HEADS, HEAD_DIM))

            ncache = cache_k.shape[1]
            qkv_s = _proj(xs, w_qkv, 0, 3 * D_MODEL, BF16, tm_s, tn_wide)
            kv_s = _proj(xs, w_qkv, D_MODEL, 2 * D_MODEL, F32, tm_s, tn_wide)
            nk_pad = -(-(ncache + s_len) // LANES) * LANES
            bias_s = _rel_bias(attn_rel_bias, ncache, s_len, nk_pad)
            att_s = _attn_sample(qkv_s, cache_k.reshape(bsz_s, ncache * N_HEADS, HEAD_DIM),
                                 cache_v.reshape(bsz_s, ncache * N_HEADS, HEAD_DIM), bias_s, bsz_s, s_len)
            xs = _wo_ln(xs, att_s, w_o, *norm(i, 1), tm_s)
            outs["kv_s"] = (kv_s[:, :D_MODEL].reshape(bsz_s, s_len, N_HEADS, HEAD_DIM),
                            kv_s[:, D_MODEL:].reshape(bsz_s, s_len, N_HEADS, HEAD_DIM))
        if i + 1 < DEPTH:
            xp, *w1 = _ffn_ln(xp, *w2, *norm(i, 2), tm_ffn, next_w=(ffn1_w_in, ffn1_w_out, i + 1))
        else:
            xp = _ffn_ln(xp, *w2, *norm(i, 2), tm_ffn)
        xs = _ffn_ln(xs, *w2, *norm(i, 2), tm_s)
        xp = _ple_ln(xp, pp, w_proj, w_gate, i, *norm(i, 3), tm_ffn)
        xs = _ple_ln(xs, ps, w_proj, w_gate, i, *norm(i, 3), tm_s)

    return (xp.reshape(bsz_p, seq, D_MODEL), xs.reshape(bsz_s, s_len, D_MODEL),
            *outs["s5_p"], *outs["kv_p"], *outs["s5_s"], *outs["kv_s"])
```

```python
import functools

import jax
import jax.numpy as jnp
from jax import lax
from jax.experimental import pallas as pl
from jax.experimental.pallas import tpu as pltpu

F32 = jnp.float32
BF16 = jnp.bfloat16

D_MODEL = 2048
DEPTH = 2
N_MIXERS = 2
CHUNK = 64
S5_GROUP = 16
S5_GROUPS = D_MODEL // S5_GROUP
S5_STATE = 64
N_HEADS = 16
HEAD_DIM = D_MODEL // N_HEADS
PAST_CHUNKS = 8
BAND_PAST = PAST_CHUNKS * CHUNK
BAND = BAND_PAST + CHUNK
REL_CLIP = 128
ATTN_SCALE = HEAD_DIM ** -0.5
NEG_INF = -1e30
D_FF = 5632
PLE_DIM = 256
DN_ALPHA = (2 * DEPTH) ** 0.25
LN_EPS = 1e-5

LANES = 128
MXU_DIM = 256
VMEM_LIMIT = 60 * 1024 * 1024

TF = 512
FFN_SUB = MXU_DIM
FFN_OUT = 512
LN_ROWS = 128
COL_TILE = 512

S5_CB = D_MODEL // LANES
S5_GPB = LANES // S5_GROUP
S5_SPB = S5_GPB * S5_STATE
S5_CHUNK = 8
S5_UW = S5_CHUNK * LANES
S5_TB = 512


def _cparams(*sem):
    return pltpu.CompilerParams(dimension_semantics=sem, vmem_limit_bytes=VMEM_LIMIT)


def _layer_norm(v, g, b, eps=LN_EPS):
    mu = jnp.mean(v, axis=-1, keepdims=True)
    d = v - mu
    var = jnp.mean(d * d, axis=-1, keepdims=True)
    return d * lax.rsqrt(var + eps) * g + b


def _norm_rows(o_ref, g_ref, b_ref, eps=LN_EPS):
    rows = min(LN_ROWS, o_ref.shape[0])
    for r in range(o_ref.shape[0] // rows):
        rs = slice(r * rows, (r + 1) * rows)
        o_ref[rs, :] = _layer_norm(o_ref[rs, :], g_ref[...], b_ref[...], eps)


def _resident(block_shape, index_map):
    return pl.BlockSpec(block_shape, index_map, pipeline_mode=pl.Buffered(1))


def _ffn_kernel(*refs, nj, cast_next):
    if cast_next:
        x_ref, wg_ref, wu_ref, wo_ref, g_ref, b_ref, nin_ref, nout_ref, o_ref, cin_ref, cout_ref, xb_ref = refs
    else:
        x_ref, wg_ref, wu_ref, wo_ref, g_ref, b_ref, o_ref, xb_ref = refs
    j = pl.program_id(1)

    @pl.when(j == 0)
    def _():
        x = x_ref[...]
        xb_ref[...] = x.astype(BF16)
        o_ref[...] = (2.0 * DN_ALPHA) * x

    if cast_next:
        cin_ref[...] = nin_ref[...].astype(BF16)
        cout_ref[...] = nout_ref[...].astype(BF16)
    xb = xb_ref[...]
    acts = []
    for h in range(TF // FFN_SUB):
        cs = slice(h * FFN_SUB, (h + 1) * FFN_SUB)
        gate = jnp.dot(xb, wg_ref[:, cs], preferred_element_type=F32)
        up = jnp.dot(xb, wu_ref[:, cs], preferred_element_type=F32)
        acts.append((gate * jax.nn.sigmoid(gate) * up).astype(BF16))
    act = jnp.concatenate(acts, axis=1)
    for n in range(D_MODEL // FFN_OUT):
        ns = slice(n * FFN_OUT, (n + 1) * FFN_OUT)
        o_ref[:, ns] += jnp.dot(act, wo_ref[:, ns], preferred_element_type=F32)

    @pl.when(j == nj - 1)
    def _():
        _norm_rows(o_ref, g_ref, b_ref, eps=4.0 * LN_EPS)


def _ffn_ln(x, w_in, w_out, g, b, tm, next_w=None):
    n = x.shape[0]
    ni, nj = n // tm, D_FF // TF
    in_specs = [
        pl.BlockSpec((tm, D_MODEL), lambda i, j: (i, 0)),
        pl.BlockSpec((D_MODEL, TF), lambda i, j: (0, j)),
        pl.BlockSpec((D_MODEL, TF), lambda i, j: (0, j + nj)),
        pl.BlockSpec((TF, D_MODEL), lambda i, j: (j, 0)),
        pl.BlockSpec((1, D_MODEL), lambda i, j: (0, 0)),
        pl.BlockSpec((1, D_MODEL), lambda i, j: (0, 0)),
    ]
    out_specs = [pl.BlockSpec((tm, D_MODEL), lambda i, j: (i, 0))]
    out_shape = [jax.ShapeDtypeStruct((n, D_MODEL), F32)]
    args = [x, w_in, w_in, w_out, g, b]
    if next_w is not None:
        nw_in, nw_out, layer = next_w
        assert D_MODEL % ni == 0 and (2 * D_FF) % nj == 0 and D_FF % nj == 0
        in_blk = (D_MODEL // ni, 2 * D_FF // nj)
        out_blk = (D_FF // nj, D_MODEL // ni)
        in_specs += [pl.BlockSpec((None,) + in_blk, lambda i, j: (layer, i, j)),
                     pl.BlockSpec((None,) + out_blk, lambda i, j: (layer, j, i))]
        out_specs += [pl.BlockSpec(in_blk, lambda i, j: (i, j)), pl.BlockSpec(out_blk, lambda i, j: (j, i))]
        out_shape += [jax.ShapeDtypeStruct(nw_in.shape[1:], BF16), jax.ShapeDtypeStruct(nw_out.shape[1:], BF16)]
        args += [nw_in, nw_out]
    res = pl.pallas_call(
        functools.partial(_ffn_kernel, nj=nj, cast_next=next_w is not None),
        grid=(ni, nj),
        in_specs=in_specs,
        out_specs=out_specs,
        out_shape=out_shape,
        scratch_shapes=[pltpu.VMEM((tm, D_MODEL), BF16)],
        compiler_params=_cparams("parallel", "arbitrary"),
        name="ffn_ln",
    )(*args)
    return res[0] if next_w is None else res


def _ple_kernel(x_ref, p_ref, wp_ref, wg_ref, g_ref, b_ref, o_ref):
    xb = x_ref[...].astype(BF16)
    pb = p_ref[...].astype(BF16)
    for n in range(D_MODEL // COL_TILE):
        ns = slice(n * COL_TILE, (n + 1) * COL_TILE)
        proj = jnp.dot(pb, wp_ref[:, ns], preferred_element_type=F32)
        gate = jnp.dot(xb, wg_ref[:, ns], preferred_element_type=F32)
        o_ref[:, ns] = DN_ALPHA * x_ref[:, ns] + proj * jax.nn.sigmoid(gate)
    _norm_rows(o_ref, g_ref, b_ref)


def _ple_ln(x, p, w_proj, w_gate, layer, g, b, tm):
    n = x.shape[0]
    return pl.pallas_call(
        _ple_kernel,
        grid=(n // tm,),
        in_specs=[
            pl.BlockSpec((tm, D_MODEL), lambda i: (i, 0)),
            pl.BlockSpec((None, tm, PLE_DIM), lambda i: (layer, i, 0)),
            _resident((None, PLE_DIM, D_MODEL), lambda i: (layer, 0, 0)),
            _resident((None, D_MODEL, D_MODEL), lambda i: (layer, 0, 0)),
            _resident((1, D_MODEL), lambda i: (0, 0)),
            _resident((1, D_MODEL), lambda i: (0, 0)),
        ],
        out_specs=pl.BlockSpec((tm, D_MODEL), lambda i: (i, 0)),
        out_shape=jax.ShapeDtypeStruct((n, D_MODEL), F32),
        compiler_params=_cparams("parallel"),
        name="ple_ln",
    )(x, p, w_proj, w_gate, g, b)


def _s5_discretize(ar, ai, log_dt):
    dt = jnp.exp(log_dt)
    mag = jnp.exp(ar * dt)
    ab_re, ab_im = mag * jnp.cos(ai * dt), mag * jnp.sin(ai * dt)
    den = ar * ar + ai * ai
    nr, ni = ab_re - 1.0, ab_im
    return ab_re, ab_im, (nr * ar + ni * ai) / den, (ni * ar - nr * ai) / den


def _s5_prep_kernel(ar_r, ai_r, ldt_r, ar_c, ai_c, ldt_c, bre_ref, bim_ref, cre_ref, cim_ref, fin_ref, fout_ref,
                    mw_ref, v_ref, at_ref, cin_ref, cout_ref):
    cin_ref[...] = fin_ref[...].astype(BF16)
    cout_ref[...] = fout_ref[...].astype(BF16)
    ab_re, ab_im, cf_re, cf_im = _s5_discretize(ar_r[...], ai_r[...], ldt_r[...])
    b_re, b_im = bre_ref[...], bim_ref[...]
    c_re, c_im = cre_ref[...], cim_ref[...]
    p_re = cf_re * b_re - cf_im * b_im
    p_im = cf_re * b_im + cf_im * b_re
    ak_re, ak_im = jnp.ones_like(ab_re), jnp.zeros_like(ab_im)
    pows = []
    for _ in range(S5_CHUNK):
        pows.append((p_re, p_im))
        p_re, p_im = ab_re * p_re - ab_im * p_im, ab_re * p_im + ab_im * p_re
        ak_re, ak_im = ab_re * ak_re - ab_im * ak_im, ab_re * ak_im + ab_im * ak_re
    at_ref[:, :S5_SPB] = ak_re
    at_ref[:, S5_SPB:] = ak_im

    taps = (jnp.dot(jnp.concatenate([p[0] for p in pows], axis=0), c_re,
                    precision=lax.Precision.HIGHEST, preferred_element_type=F32)
            - jnp.dot(jnp.concatenate([p[1] for p in pows], axis=0), c_im,
                      precision=lax.Precision.HIGHEST, preferred_element_type=F32)).astype(BF16)
    zero = jnp.zeros((LANES, LANES), BF16)
    for s in range(S5_CHUNK):
        rows = slice(s * LANES, (s + 1) * LANES)
        for t in range(S5_CHUNK):
            lag = slice((t - s) * LANES, (t - s + 1) * LANES)
            mw_ref[rows, t * LANES:(t + 1) * LANES] = taps[lag, :] if t >= s else zero
        w_re, w_im = pows[S5_CHUNK - 1 - s]
        mw_ref[rows, S5_UW:S5_UW + S5_SPB] = w_re.astype(BF16)
        mw_ref[rows, S5_UW + S5_SPB:] = w_im.astype(BF16)

    ac_re, ac_im, _, _ = _s5_discretize(ar_c[...], ai_c[...], ldt_c[...])
    ck_re, ck_im = ac_re, ac_im
    for t in range(S5_CHUNK):
        cols = slice(t * LANES, (t + 1) * LANES)
        v_ref[:S5_SPB, cols] = (ck_re * c_re - ck_im * c_im).astype(BF16)
        v_ref[S5_SPB:, cols] = (-(ck_im * c_re + ck_re * c_im)).astype(BF16)
        ck_re, ck_im = ck_re * ac_re - ck_im * ac_im, ck_re * ac_im + ck_im * ac_re


def _s5_prep(a_re, a_im, log_dt, b_re, b_im, c_re, c_im, ffn_w_in, ffn_w_out):
    fin_blk = (ffn_w_in.shape[1] // S5_CB, ffn_w_in.shape[2])
    fout_blk = (ffn_w_out.shape[1] // S5_CB, ffn_w_out.shape[2])
    assert fin_blk[0] % 16 == 0 and fout_blk[0] % 16 == 0
    eye = jnp.eye(S5_GPB, dtype=F32)

    def rows(v):
        return v.reshape(S5_CB, 1, S5_SPB)

    def cols(v):
        return v.reshape(S5_CB, S5_SPB, 1)

    def b_blockdiag(w):
        w4 = w.reshape(S5_CB, S5_GPB, S5_STATE, S5_GROUP).transpose(0, 1, 3, 2)
        return (w4[:, :, :, None, :] * eye[None, :, None, :, None]).reshape(S5_CB, LANES, S5_SPB)

    def c_blockdiag(w):
        w4 = w.reshape(S5_CB, S5_GPB, S5_GROUP, S5_STATE).transpose(0, 1, 3, 2)
        return (w4[:, :, :, None, :] * eye[None, :, None, :, None]).reshape(S5_CB, S5_SPB, LANES)

    ldt = jnp.repeat(log_dt, S5_STATE)
    row_spec = pl.BlockSpec((None, 1, S5_SPB), lambda c: (c, 0, 0))
    col_spec = pl.BlockSpec((None, S5_SPB, 1), lambda c: (c, 0, 0))
    b_spec = pl.BlockSpec((None, LANES, S5_SPB), lambda c: (c, 0, 0))
    c_spec = pl.BlockSpec((None, S5_SPB, LANES), lambda c: (c, 0, 0))
    return pl.pallas_call(
        _s5_prep_kernel,
        grid=(S5_CB,),
        in_specs=[row_spec, row_spec, row_spec, col_spec, col_spec, col_spec, b_spec, b_spec, c_spec, c_spec,
                  pl.BlockSpec((None,) + fin_blk, lambda c: (0, c, 0)),
                  pl.BlockSpec((None,) + fout_blk, lambda c: (0, c, 0))],
        out_specs=[
            pl.BlockSpec((None, S5_UW, S5_UW + 2 * S5_SPB), lambda c: (c, 0, 0)),
            pl.BlockSpec((None, 2 * S5_SPB, S5_UW), lambda c: (c, 0, 0)),
            pl.BlockSpec((None, 1, 2 * S5_SPB), lambda c: (c, 0, 0)),
            pl.BlockSpec(fin_blk, lambda c: (c, 0)),
            pl.BlockSpec(fout_blk, lambda c: (c, 0)),
        ],
        out_shape=[
            jax.ShapeDtypeStruct((S5_CB, S5_UW, S5_UW + 2 * S5_SPB), BF16),
            jax.ShapeDtypeStruct((S5_CB, 2 * S5_SPB, S5_UW), BF16),
            jax.ShapeDtypeStruct((S5_CB, 1, 2 * S5_SPB), F32),
            jax.ShapeDtypeStruct(ffn_w_in.shape[1:], BF16),
            jax.ShapeDtypeStruct(ffn_w_out.shape[1:], BF16),
        ],
        compiler_params=_cparams("parallel"),
        name="s5_prep",
    )(rows(a_re), rows(a_im), rows(ldt), cols(a_re), cols(a_im), cols(ldt),
      b_blockdiag(b_re), b_blockdiag(b_im), c_blockdiag(c_re), c_blockdiag(c_im), ffn_w_in, ffn_w_out)


def _s5_step(at_ref, h_re, h_im, s_re, s_im):
    a_re, a_im = at_ref[:, :S5_SPB], at_ref[:, S5_SPB:]
    return a_re * h_re - a_im * h_im + s_re, a_re * h_im + a_im * h_re + s_im


def _s5_prompt_kernel(*refs, tb, nr, n_cast):
    x_ref, mw_ref, v_ref, at_ref, d_ref = refs[:5]
    w32_refs = refs[5:5 + n_cast]
    y_ref, hfin_ref = refs[5 + n_cast:7 + n_cast]
    w16_refs = refs[7 + n_cast:7 + 2 * n_cast]
    hc_ref, s_ref, h_ref = refs[7 + 2 * n_cast:]
    r = pl.program_id(1)

    @pl.when(r == 0)
    def _():
        hc_ref[...] = jnp.zeros_like(hc_ref)

    for w32, w16 in zip(w32_refs, w16_refs):
        w16[...] = w32[...].astype(BF16)
    u = jnp.concatenate([x_ref[pl.ds(s, tb, stride=S5_CHUNK), :] for s in range(S5_CHUNK)], axis=1)
    ub = u.astype(BF16)
    s_ref[...] = jnp.dot(ub, mw_ref[:, S5_UW:], preferred_element_type=F32)

    def body(i, h):
        h_re, h_im = h
        row = pl.ds(i, 1)
        h_ref[row, :S5_SPB] = h_re
        h_ref[row, S5_SPB:] = h_im
        return _s5_step(at_ref, h_re, h_im, s_ref[row, :S5_SPB], s_ref[row, S5_SPB:])

    h_re, h_im = lax.fori_loop(0, tb, body, (hc_ref[:, :S5_SPB], hc_ref[:, S5_SPB:]), unroll=8)
    hc_ref[:, :S5_SPB] = h_re
    hc_ref[:, S5_SPB:] = h_im

    hb = h_ref[...].astype(BF16)
    steps = MXU_DIM // LANES
    for c in range(S5_UW // MXU_DIM):
        cols = slice(c * MXU_DIM, (c + 1) * MXU_DIM)
        k = (c + 1) * MXU_DIM
        y = (jnp.dot(ub[:, :k], mw_ref[:k, cols], preferred_element_type=F32)
             + jnp.dot(hb, v_ref[:, cols], preferred_element_type=F32)
             + jnp.concatenate([d_ref[...]] * steps, axis=1) * u[:, cols])
        for t in range(steps):
            y_ref[pl.ds(c * steps + t, tb, stride=S5_CHUNK), :] = y[:, t * LANES:(t + 1) * LANES]

    @pl.when(r == nr - 1)
    def _():
        hfin_ref[...] = hc_ref[...]


def _s5_prompt(x, mw, v, at, d, cast=()):
    n = x.shape[0]
    tb = S5_TB
    rows = tb * S5_CHUNK
    nr = n // rows
    cast_blocks = []
    for w in cast:
        assert w.shape[0] % (S5_CB * 16) == 0 and w.shape[1] % (nr * LANES) == 0
        cast_blocks.append(pl.BlockSpec((w.shape[0] // S5_CB, w.shape[1] // nr), lambda c, r: (c, r)))
    return pl.pallas_call(
        functools.partial(_s5_prompt_kernel, tb=tb, nr=nr, n_cast=len(cast)),
        grid=(S5_CB, nr),
        in_specs=[
            pl.BlockSpec((rows, LANES), lambda c, r: (r, c)),
            pl.BlockSpec((None, S5_UW, S5_UW + 2 * S5_SPB), lambda c, r: (c, 0, 0)),
            pl.BlockSpec((None, 2 * S5_SPB, S5_UW), lambda c, r: (c, 0, 0)),
            pl.BlockSpec((None, 1, 2 * S5_SPB), lambda c, r: (c, 0, 0)),
            pl.BlockSpec((None, 1, LANES), lambda c, r: (c, 0, 0)),
        ] + cast_blocks,
        out_specs=[
            pl.BlockSpec((rows, LANES), lambda c, r: (r, c)),
            pl.BlockSpec((None, 1, 2 * S5_SPB), lambda c, r: (c, 0, 0)),
        ] + cast_blocks,
        out_shape=[
            jax.ShapeDtypeStruct((n, D_MODEL), F32),
            jax.ShapeDtypeStruct((S5_CB, 1, 2 * S5_SPB), F32),
        ] + [jax.ShapeDtypeStruct(w.shape, BF16) for w in cast],
        scratch_shapes=[pltpu.VMEM((1, 2 * S5_SPB), F32), pltpu.VMEM((tb, 2 * S5_SPB), F32),
                        pltpu.VMEM((tb, 2 * S5_SPB), F32)],
        compiler_params=_cparams("parallel", "arbitrary"),
        name="s5_prompt",
    )(x, mw, v, at, d, *cast)


def _s5_sample_kernel(u_ref, hre_ref, him_ref, mw_ref, v_ref, at_ref, d_ref, y_ref, ore_ref, oim_ref, *, nblk):
    d_row = jnp.concatenate([d_ref[...]] * S5_CHUNK, axis=1)
    h_re, h_im = hre_ref[...], him_ref[...]
    for k in range(nblk):
        u = u_ref[k]
        res = jnp.dot(u.astype(BF16), mw_ref[...], preferred_element_type=F32)
        h = jnp.concatenate([h_re, h_im], axis=1).astype(BF16)
        y_ref[k] = res[:, :S5_UW] + jnp.dot(h, v_ref[...], preferred_element_type=F32) + d_row * u
        h_re, h_im = _s5_step(at_ref, h_re, h_im, res[:, S5_UW:S5_UW + S5_SPB], res[:, S5_UW + S5_SPB:])
    ore_ref[...] = h_re
    oim_ref[...] = h_im


def _s5_sample(u, h_re, h_im, mw, v, at, d):
    _, nblk, bsz, _ = u.shape
    h_spec = pl.BlockSpec((bsz, S5_SPB), lambda c: (0, c))
    u_spec = pl.BlockSpec((None, nblk, bsz, S5_UW), lambda c: (c, 0, 0, 0))
    return pl.pallas_call(
        functools.partial(_s5_sample_kernel, nblk=nblk),
        grid=(S5_CB,),
        in_specs=[
            u_spec, h_spec, h_spec,
            pl.BlockSpec((None, S5_UW, S5_UW + 2 * S5_SPB), lambda c: (c, 0, 0)),
            pl.BlockSpec((None, 2 * S5_SPB, S5_UW), lambda c: (c, 0, 0)),
            pl.BlockSpec((None, 1, 2 * S5_SPB), lambda c: (c, 0, 0)),
            pl.BlockSpec((None, 1, LANES), lambda c: (c, 0, 0)),
        ],
        out_specs=[u_spec, h_spec, h_spec],
        out_shape=[
            jax.ShapeDtypeStruct(u.shape, F32),
            jax.ShapeDtypeStruct(h_re.shape, F32),
            jax.ShapeDtypeStruct(h_im.shape, F32),
        ],
        compiler_params=_cparams("parallel"),
        name="s5_sample",
    )(u, h_re, h_im, mw, v, at, d)


def _glu_kernel(x_ref, y_ref, w_ref, g_ref, b_ref, o_ref):
    z = jax.nn.gelu(y_ref[...]).astype(BF16)
    for n in range(D_MODEL // COL_TILE):
        ns = slice(n * COL_TILE, (n + 1) * COL_TILE)
        gs = slice(D_MODEL + n * COL_TILE, D_MODEL + (n + 1) * COL_TILE)
        za = jnp.dot(z, w_ref[:, ns], preferred_element_type=F32)
        zb = jnp.dot(z, w_ref[:, gs], preferred_element_type=F32)
        o_ref[:, ns] = DN_ALPHA * x_ref[:, ns] + za * jax.nn.sigmoid(zb)
    _norm_rows(o_ref, g_ref, b_ref)


def _glu_ln(x, y, w_glu, g, b, tm):
    n = x.shape[0]
    return pl.pallas_call(
        _glu_kernel,
        grid=(n // tm,),
        in_specs=[
            pl.BlockSpec((tm, D_MODEL), lambda i: (i, 0)),
            pl.BlockSpec((tm, D_MODEL), lambda i: (i, 0)),
            _resident((D_MODEL, 2 * D_MODEL), lambda i: (0, 0)),
            _resident((1, D_MODEL), lambda i: (0, 0)),
            _resident((1, D_MODEL), lambda i: (0, 0)),
        ],
        out_specs=pl.BlockSpec((tm, D_MODEL), lambda i: (i, 0)),
        out_shape=jax.ShapeDtypeStruct((n, D_MODEL), F32),
        compiler_params=_cparams("parallel"),
        name="glu_ln",
    )(x, y, w_glu, g, b)


def _proj_kernel(x_ref, w_ref, o_ref, xb_ref):
    @pl.when(pl.program_id(1) == 0)
    def _():
        xb_ref[...] = x_ref[...].astype(BF16)

    o_ref[...] = jnp.dot(xb_ref[...], w_ref[...], preferred_element_type=F32).astype(o_ref.dtype)


def _proj(x, w, col0, ncols, out_dtype, tm, tn, row0=0, nrows=None):
    nrows = x.shape[0] if nrows is None else nrows
    assert row0 % tm == 0 and col0 % tn == 0
    rb, cb = row0 // tm, col0 // tn
    return pl.pallas_call(
        _proj_kernel,
        grid=(nrows // tm, ncols // tn),
        in_specs=[
            pl.BlockSpec((tm, D_MODEL), lambda i, j: (i + rb, 0)),
            pl.BlockSpec((D_MODEL, tn), lambda i, j: (0, j + cb)),
        ],
        out_specs=pl.BlockSpec((tm, tn), lambda i, j: (i, j)),
        out_shape=jax.ShapeDtypeStruct((nrows, ncols), out_dtype),
        scratch_shapes=[pltpu.VMEM((tm, D_MODEL), BF16)],
        compiler_params=_cparams("parallel", "arbitrary"),
        name="proj",
    )(x, w)


def _wo_kernel(x_ref, a_ref, w_ref, g_ref, b_ref, o_ref):
    a = a_ref[...]
    for n in range(D_MODEL // COL_TILE):
        ns = slice(n * COL_TILE, (n + 1) * COL_TILE)
        o_ref[:, ns] = DN_ALPHA * x_ref[:, ns] + jnp.dot(a, w_ref[:, ns], preferred_element_type=F32)
    _norm_rows(o_ref, g_ref, b_ref)


def _wo_ln(x, a, w_o, g, b, tm):
    n = x.shape[0]
    return pl.pallas_call(
        _wo_kernel,
        grid=(n // tm,),
        in_specs=[
            pl.BlockSpec((tm, D_MODEL), lambda i: (i, 0)),
            pl.BlockSpec((tm, D_MODEL), lambda i: (i, 0)),
            _resident((D_MODEL, D_MODEL), lambda i: (0, 0)),
            _resident((1, D_MODEL), lambda i: (0, 0)),
            _resident((1, D_MODEL), lambda i: (0, 0)),
        ],
        out_specs=pl.BlockSpec((tm, D_MODEL), lambda i: (i, 0)),
        out_shape=jax.ShapeDtypeStruct((n, D_MODEL), F32),
        compiler_params=_cparams("parallel"),
        name="wo_ln",
    )(x, a, w_o, g, b)


def _bias_kernel(tab_ref, o_ref, base_ref, *, q_off, nq, nk, scale, band):
    h = pl.program_id(0)
    ntab, nw = tab_ref.shape[1], base_ref.shape[1]

    @pl.when(h == 0)
    def _():
        off = lax.broadcasted_iota(jnp.int32, (ntab, nw), 1)
        off = jnp.where(off < nk, off, off - nw)
        idx = jnp.clip(q_off - off, -REL_CLIP, REL_CLIP) + REL_CLIP
        onehot = (idx == lax.broadcasted_iota(jnp.int32, (ntab, nw), 0)).astype(F32)
        base_ref[...] = scale * jnp.dot(tab_ref[...], onehot, precision=lax.Precision.HIGHEST,
                                        preferred_element_type=F32)

    bias = pltpu.roll(jnp.broadcast_to(base_ref[pl.ds(h, 1), :], (nq, nw)), 0, 1, stride=1, stride_axis=0)
    if band:
        row = lax.broadcasted_iota(jnp.int32, (nq, nw), 0)
        col = lax.broadcasted_iota(jnp.int32, (nq, nw), 1)
        first = row & ~(CHUNK - 1)
        bias = jnp.where((col >= first) & (col < first + BAND), bias, NEG_INF)
    o_ref[...] = bias[:, :nk]


def _rel_bias(table, q_off, nq, nk, scale=1.0, band=False):
    assert nk % LANES == 0
    ntab = table.shape[1]
    ntab_pad = -(-ntab // LANES) * LANES
    nw = -(-(nq + nk) // LANES) * LANES
    table = jnp.pad(table, ((0, 0), (0, ntab_pad - ntab)))
    return pl.pallas_call(
        functools.partial(_bias_kernel, q_off=q_off, nq=nq, nk=nk, scale=scale, band=band),
        grid=(N_HEADS,),
        in_specs=[_resident((N_HEADS, ntab_pad), lambda h: (0, 0))],
        out_specs=pl.BlockSpec((None, nq, nk), lambda h: (h, 0, 0)),
        out_shape=jax.ShapeDtypeStruct((N_HEADS, nq, nk), F32),
        scratch_shapes=[pltpu.VMEM((N_HEADS, nw), F32)],
        compiler_params=_cparams("arbitrary"),
        name="rel_bias",
    )(table)


ATT_TQ = BAND_PAST
ATT_GQ = 4 * CHUNK
ATT_GK = ATT_GQ + BAND_PAST
ATT_HB = 4
LOG2E = 1.4426950408889634


def _attn_prompt_kernel(q_ref, kp_ref, kc_ref, vp_ref, vc_ref, bias_ref, o_ref, s_ref, p_ref, l_ref):
    i = pl.program_id(0)
    ng = ATT_TQ // ATT_GQ
    dims = (((1,), (1,)), ((), ()))
    units = [(hh, g) for hh in range(ATT_HB) for g in range(ng)]

    def split(g):
        lo = g * ATT_GQ
        return lo, ATT_TQ - lo

    for u, (hh, g) in enumerate(units):
        cols = slice(hh * HEAD_DIM, (hh + 1) * HEAD_DIM)
        lo, n_prev = split(g)
        q = q_ref[lo:lo + ATT_GQ, cols]
        c = ATTN_SCALE * LOG2E
        s_prev = lax.dot_general(q, kp_ref[lo:, cols], dims, preferred_element_type=F32)
        s_ref[u, :, :n_prev] = s_prev * c + bias_ref[hh, :, :n_prev]
        s_cur = lax.dot_general(q, kc_ref[:ATT_GK - n_prev, cols], dims, preferred_element_type=F32)
        s_ref[u, :, n_prev:] = s_cur * c + bias_ref[hh, :, n_prev:]

    @pl.when(i == 0)
    def _():
        for u, (hh, g) in enumerate(units):
            _, n_prev = split(g)
            s_ref[u, :, :n_prev] = jnp.full((ATT_GQ, n_prev), NEG_INF, F32)

    for u, (hh, g) in enumerate(units):
        x = s_ref[u]
        e = jnp.exp2(x - jnp.max(x, axis=-1, keepdims=True))
        l_ref[u] = jnp.sum(e, axis=-1, keepdims=True)
        p_ref[u] = e.astype(BF16)

    for u, (hh, g) in enumerate(units):
        cols = slice(hh * HEAD_DIM, (hh + 1) * HEAD_DIM)
        lo, n_prev = split(g)
        pv = (jnp.dot(p_ref[u, :, :n_prev], vp_ref[lo:, cols], preferred_element_type=F32)
              + jnp.dot(p_ref[u, :, n_prev:], vc_ref[:ATT_GK - n_prev, cols], preferred_element_type=F32))
        o_ref[lo:lo + ATT_GQ, cols] = (pv / l_ref[u]).astype(o_ref.dtype)


def _attn_prompt(qkv, bias):
    n = qkv.shape[0]
    width = ATT_HB * HEAD_DIM
    nhb = D_MODEL // width
    n_units = ATT_HB * (ATT_TQ // ATT_GQ)

    def prev(i):
        return jnp.maximum(i - 1, 0)

    return pl.pallas_call(
        _attn_prompt_kernel,
        grid=(n // ATT_TQ, nhb),
        in_specs=[
            pl.BlockSpec((ATT_TQ, width), lambda i, h: (i, h)),
            pl.BlockSpec((ATT_TQ, width), lambda i, h: (prev(i), nhb + h)),
            pl.BlockSpec((ATT_TQ, width), lambda i, h: (i, nhb + h)),
            pl.BlockSpec((ATT_TQ, width), lambda i, h: (prev(i), 2 * nhb + h)),
            pl.BlockSpec((ATT_TQ, width), lambda i, h: (i, 2 * nhb + h)),
            pl.BlockSpec((ATT_HB, ATT_GQ, ATT_GK), lambda i, h: (h, 0, 0)),
        ],
        out_specs=pl.BlockSpec((ATT_TQ, width), lambda i, h: (i, h)),
        out_shape=jax.ShapeDtypeStruct((n, D_MODEL), BF16),
        scratch_shapes=[pltpu.VMEM((n_units, ATT_GQ, ATT_GK), F32), pltpu.VMEM((n_units, ATT_GQ, ATT_GK), BF16),
                        pltpu.VMEM((n_units, ATT_GQ, 1), F32)],
        compiler_params=_cparams("parallel", "parallel"),
        name="attn_prompt",
    )(qkv, qkv, qkv, qkv, qkv, bias)


def _attn_sample_kernel(q_ref, kn_ref, vn_ref, ck_ref, cv_ref, bias_ref, o_ref, *, ncache):
    s_len = q_ref.shape[0]
    for h in range(N_HEADS):
        cols = slice(h * HEAD_DIM, (h + 1) * HEAD_DIM)
        q = q_ref[:, cols]
        k_old = ck_ref[pl.ds(h, ncache, stride=N_HEADS), :].astype(BF16)
        v_old = cv_ref[pl.ds(h, ncache, stride=N_HEADS), :].astype(BF16)
        k_new, v_new = kn_ref[:, cols], vn_ref[:, cols]
        dims = (((1,), (1,)), ((), ()))
        s_old = lax.dot_general(q, k_old, dims, preferred_element_type=F32) * ATTN_SCALE + bias_ref[h, :, :ncache]
        s_new = (lax.dot_general(q, k_new, dims, preferred_element_type=F32) * ATTN_SCALE
                 + bias_ref[h, :, ncache:ncache + s_len])
        m = jnp.maximum(jnp.max(s_old, axis=-1, keepdims=True), jnp.max(s_new, axis=-1, keepdims=True))
        e_old, e_new = jnp.exp(s_old - m), jnp.exp(s_new - m)
        den = jnp.sum(e_old, axis=-1, keepdims=True) + jnp.sum(e_new, axis=-1, keepdims=True)
        pv = (jnp.dot(e_old.astype(BF16), v_old, preferred_element_type=F32)
              + jnp.dot(e_new.astype(BF16), v_new, preferred_element_type=F32))
        o_ref[:, cols] = (pv / den).astype(o_ref.dtype)


def _attn_sample(qkv, cache_k, cache_v, bias, bsz, s_len):
    ncache = cache_k.shape[1] // N_HEADS
    return pl.pallas_call(
        functools.partial(_attn_sample_kernel, ncache=ncache),
        grid=(bsz,),
        in_specs=[
            pl.BlockSpec((s_len, D_MODEL), lambda b: (b, 0)),
            pl.BlockSpec((s_len, D_MODEL), lambda b: (b, 1)),
            pl.BlockSpec((s_len, D_MODEL), lambda b: (b, 2)),
            pl.BlockSpec((None, ncache * N_HEADS, HEAD_DIM), lambda b: (b, 0, 0)),
            pl.BlockSpec((None, ncache * N_HEADS, HEAD_DIM), lambda b: (b, 0, 0)),
            pl.BlockSpec(bias.shape, lambda b: (0, 0, 0)),
        ],
        out_specs=pl.BlockSpec((s_len, D_MODEL), lambda b: (b, 0)),
        out_shape=jax.ShapeDtypeStruct((bsz * s_len, D_MODEL), BF16),
        compiler_params=_cparams("parallel"),
        name="attn_sample",
    )(qkv, qkv, qkv, cache_k, cache_v, bias)


def kernel(x_prompt, x_sample, state_s5_re, state_s5_im, cache_k, cache_v, p_prompt, p_sample, ffn1_w_in, ffn1_w_out, ffn2_w_in, ffn2_w_out, ln_g, ln_b, ple_w_proj, ple_w_gate, s5_a_re, s5_a_im, s5_log_dt, s5_b_re, s5_b_im, s5_c_re, s5_c_im, s5_d, s5_w_glu, attn_w_qkv, attn_w_o, attn_rel_bias):
    bsz_p, seq, _ = x_prompt.shape
    bsz_s, s_len, _ = x_sample.shape
    assert bsz_p == 1 and s_len % S5_CHUNK == 0 and seq % (S5_CHUNK * S5_TB) == 0
    n_p, n_s = bsz_p * seq, bsz_s * s_len
    tm_glu, tm_s = 512, n_s
    tm_ffn, tm_proj, tn_wide = 1024, 1024, 1024

    xp = x_prompt.reshape(n_p, D_MODEL)
    xs = x_sample.reshape(n_s, D_MODEL)
    pp = p_prompt.reshape(DEPTH, n_p, PLE_DIM)
    ps = p_sample.reshape(DEPTH, n_s, PLE_DIM)
    mw, v, at, *w1 = _s5_prep(s5_a_re, s5_a_im, s5_log_dt, s5_b_re, s5_b_im, s5_c_re, s5_c_im,
                              ffn1_w_in, ffn1_w_out)

    def norm(i, slot):
        return ln_g[i, slot].reshape(1, D_MODEL), ln_b[i, slot].reshape(1, D_MODEL)

    outs = {}
    for i in range(DEPTH):
        xp, *w2 = _ffn_ln(xp, *w1, *norm(i, 0), tm_ffn, next_w=(ffn2_w_in, ffn2_w_out, i))
        xs = _ffn_ln(xs, *w1, *norm(i, 0), tm_s)
        if i % N_MIXERS == 0:
            d = s5_d.reshape(S5_CB, 1, LANES)
            later = (s5_w_glu, attn_w_qkv, attn_w_o, ple_w_gate.reshape(DEPTH * D_MODEL, D_MODEL),
                     ple_w_proj.reshape(DEPTH * PLE_DIM, D_MODEL))
            y, h_fin, w_glu, w_qkv, w_o, w_gate, w_proj = _s5_prompt(xp, mw, v, at, d, cast=later)
            w_gate = w_gate.reshape(DEPTH, D_MODEL, D_MODEL)
            w_proj = w_proj.reshape(DEPTH, PLE_DIM, D_MODEL)
            xp = _glu_ln(xp, y, w_glu, *norm(i, 1), tm_glu)
            outs["s5_p"] = (h_fin[:, 0, :S5_SPB].reshape(bsz_p, S5_GROUPS, S5_STATE),
                            h_fin[:, 0, S5_SPB:].reshape(bsz_p, S5_GROUPS, S5_STATE))
            nblk = s_len // S5_CHUNK
            u = (xs.reshape(bsz_s, nblk, S5_CHUNK, S5_CB, LANES).transpose(3, 1, 0, 2, 4)
                 .reshape(S5_CB, nblk, bsz_s, S5_UW))
            ys, hs_re, hs_im = _s5_sample(u, state_s5_re.reshape(bsz_s, -1), state_s5_im.reshape(bsz_s, -1),
                                          mw, v, at, d)
            ys = (ys.reshape(S5_CB, nblk, bsz_s, S5_CHUNK, LANES).transpose(2, 1, 3, 0, 4)
                  .reshape(n_s, D_MODEL))
            xs = _glu_ln(xs, ys, w_glu, *norm(i, 1), tm_s)
            outs["s5_s"] = (hs_re.reshape(bsz_s, S5_GROUPS, S5_STATE), hs_im.reshape(bsz_s, S5_GROUPS, S5_STATE))
        else:
            rows = min(BAND_PAST, seq)
            assert seq % ATT_TQ == 0
            qkv_p = _proj(xp, w_qkv, 0, 3 * D_MODEL, BF16, tm_proj, 2 * tn_wide)
            kv_tail = _proj(xp, w_qkv, D_MODEL, 2 * D_MODEL, F32, rows, tn_wide, row0=n_p - rows, nrows=rows)
            bias_p = _rel_bias(attn_rel_bias, BAND_PAST, ATT_GQ, ATT_GK, scale=LOG2E, band=True)
            att_p = _attn_prompt(qkv_p, bias_p)
            xp = _wo_ln(xp, att_p, w_o, *norm(i, 1), tm_ffn)
            outs["kv_p"] = (kv_tail[:, :D_MODEL].reshape(bsz_p, rows, N_HEADS, HEAD_DIM),
                            kv_tail[:, D_MODEL:].reshape(bsz_p, rows, N_HEADS, HEAD_DIM))

            ncache = cache_k.shape[1]
            qkv_s = _proj(xs, w_qkv, 0, 3 * D_MODEL, BF16, tm_s, tn_wide)
            kv_s = _proj(xs, w_qkv, D_MODEL, 2 * D_MODEL, F32, tm_s, tn_wide)
            nk_pad = -(-(ncache + s_len) // LANES) * LANES
            bias_s = _rel_bias(attn_rel_bias, ncache, s_len, nk_pad)
            att_s = _attn_sample(qkv_s, cache_k.reshape(bsz_s, ncache * N_HEADS, HEAD_DIM),
                                 cache_v.reshape(bsz_s, ncache * N_HEADS, HEAD_DIM), bias_s, bsz_s, s_len)
            xs = _wo_ln(xs, att_s, w_o, *norm(i, 1), tm_s)
            outs["kv_s"] = (kv_s[:, :D_MODEL].reshape(bsz_s, s_len, N_HEADS, HEAD_DIM),
                            kv_s[:, D_MODEL:].reshape(bsz_s, s_len, N_HEADS, HEAD_DIM))
        if i + 1 < DEPTH:
            xp, *w1 = _ffn_ln(xp, *w2, *norm(i, 2), tm_ffn, next_w=(ffn1_w_in, ffn1_w_out, i + 1))
        else:
            xp = _ffn_ln(xp, *w2, *norm(i, 2), tm_ffn)
        xs = _ffn_ln(xs, *w2, *norm(i, 2), tm_s)
        xp = _ple_ln(xp, pp, w_proj, w_gate, i, *norm(i, 3), tm_ffn)
        xs = _ple_ln(xs, ps, w_proj, w_gate, i, *norm(i, 3), tm_s)

    return (xp.reshape(bsz_p, seq, D_MODEL), xs.reshape(bsz_s, s_len, D_MODEL),
            *outs["s5_p"], *outs["kv_p"], *outs["s5_s"], *outs["kv_s"])
```

```python
import functools

import jax
import jax.numpy as jnp
from jax import lax
from jax.experimental import pallas as pl
from jax.experimental.pallas import tpu as pltpu

F32 = jnp.float32
BF16 = jnp.bfloat16

D_MODEL = 2048
DEPTH = 2
N_MIXERS = 2
CHUNK = 64
S5_GROUP = 16
S5_GROUPS = D_MODEL // S5_GROUP
S5_STATE = 64
N_HEADS = 16
HEAD_DIM = D_MODEL // N_HEADS
PAST_CHUNKS = 8
BAND_PAST = PAST_CHUNKS * CHUNK
BAND = BAND_PAST + CHUNK
REL_CLIP = 128
ATTN_SCALE = HEAD_DIM ** -0.5
NEG_INF = -1e30
D_FF = 5632
PLE_DIM = 256
DN_ALPHA = (2 * DEPTH) ** 0.25
LN_EPS = 1e-5

LANES = 128
MXU_DIM = 256
VMEM_LIMIT = 60 * 1024 * 1024

TF = 512
FFN_SUB = MXU_DIM
FFN_OUT = 512
LN_ROWS = 128
COL_TILE = 512

S5_CB = D_MODEL // LANES
S5_GPB = LANES // S5_GROUP
S5_SPB = S5_GPB * S5_STATE
S5_CHUNK = 8
S5_UW = S5_CHUNK * LANES
S5_TB = 512


def _cparams(*sem):
    return pltpu.CompilerParams(dimension_semantics=sem, vmem_limit_bytes=VMEM_LIMIT)


def _layer_norm(v, g, b, eps=LN_EPS):
    mu = jnp.mean(v, axis=-1, keepdims=True)
    d = v - mu
    var = jnp.mean(d * d, axis=-1, keepdims=True)
    return d * lax.rsqrt(var + eps) * g + b


def _norm_rows(o_ref, g_ref, b_ref, eps=LN_EPS):
    rows = min(LN_ROWS, o_ref.shape[0])
    for r in range(o_ref.shape[0] // rows):
        rs = slice(r * rows, (r + 1) * rows)
        o_ref[rs, :] = _layer_norm(o_ref[rs, :], g_ref[...], b_ref[...], eps)


def _resident(block_shape, index_map):
    return pl.BlockSpec(block_shape, index_map, pipeline_mode=pl.Buffered(1))


def _ffn_kernel(*refs, nj, cast_next):
    if cast_next:
        x_ref, wg_ref, wu_ref, wo_ref, g_ref, b_ref, nin_ref, nout_ref, o_ref, cin_ref, cout_ref, xb_ref = refs
    else:
        x_ref, wg_ref, wu_ref, wo_ref, g_ref, b_ref, o_ref, xb_ref = refs
    j = pl.program_id(1)

    @pl.when(j == 0)
    def _():
        x = x_ref[...]
        xb_ref[...] = x.astype(BF16)
        o_ref[...] = (2.0 * DN_ALPHA) * x

    if cast_next:
        cin_ref[...] = nin_ref[...].astype(BF16)
        cout_ref[...] = nout_ref[...].astype(BF16)
    xb = xb_ref[...]
    acts = []
    for h in range(TF // FFN_SUB):
        cs = slice(h * FFN_SUB, (h + 1) * FFN_SUB)
        gate = jnp.dot(xb, wg_ref[:, cs], preferred_element_type=F32)
        up = jnp.dot(xb, wu_ref[:, cs], preferred_element_type=F32)
        acts.append((gate * jax.nn.sigmoid(gate) * up).astype(BF16))
    act = jnp.concatenate(acts, axis=1)
    for n in range(D_MODEL // FFN_OUT):
        ns = slice(n * FFN_OUT, (n + 1) * FFN_OUT)
        o_ref[:, ns] += jnp.dot(act, wo_ref[:, ns], preferred_element_type=F32)

    @pl.when(j == nj - 1)
    def _():
        _norm_rows(o_ref, g_ref, b_ref, eps=4.0 * LN_EPS)


def _ffn_ln(x, w_in, w_out, g, b, tm, next_w=None):
    n = x.shape[0]
    ni, nj = n // tm, D_FF // TF
    in_specs = [
        pl.BlockSpec((tm, D_MODEL), lambda i, j: (i, 0)),
        pl.BlockSpec((D_MODEL, TF), lambda i, j: (0, j)),
        pl.BlockSpec((D_MODEL, TF), lambda i, j: (0, j + nj)),
        pl.BlockSpec((TF, D_MODEL), lambda i, j: (j, 0)),
        pl.BlockSpec((1, D_MODEL), lambda i, j: (0, 0)),
        pl.BlockSpec((1, D_MODEL), lambda i, j: (0, 0)),
    ]
    out_specs = [pl.BlockSpec((tm, D_MODEL), lambda i, j: (i, 0))]
    out_shape = [jax.ShapeDtypeStruct((n, D_MODEL), F32)]
    args = [x, w_in, w_in, w_out, g, b]
    if next_w is not None:
        nw_in, nw_out, layer = next_w
        assert D_MODEL % ni == 0 and (2 * D_FF) % nj == 0 and D_FF % nj == 0
        in_blk = (D_MODEL // ni, 2 * D_FF // nj)
        out_blk = (D_FF // nj, D_MODEL // ni)
        in_specs += [pl.BlockSpec((None,) + in_blk, lambda i, j: (layer, i, j)),
                     pl.BlockSpec((None,) + out_blk, lambda i, j: (layer, j, i))]
        out_specs += [pl.BlockSpec(in_blk, lambda i, j: (i, j)), pl.BlockSpec(out_blk, lambda i, j: (j, i))]
        out_shape += [jax.ShapeDtypeStruct(nw_in.shape[1:], BF16), jax.ShapeDtypeStruct(nw_out.shape[1:], BF16)]
        args += [nw_in, nw_out]
    res = pl.pallas_call(
        functools.partial(_ffn_kernel, nj=nj, cast_next=next_w is not None),
        grid=(ni, nj),
        in_specs=in_specs,
        out_specs=out_specs,
        out_shape=out_shape,
        scratch_shapes=[pltpu.VMEM((tm, D_MODEL), BF16)],
        compiler_params=_cparams("parallel", "arbitrary"),
        name="ffn_ln",
    )(*args)
    return res[0] if next_w is None else res


def _ple_kernel(x_ref, p_ref, wp_ref, wg_ref, g_ref, b_ref, o_ref):
    xb = x_ref[...].astype(BF16)
    pb = p_ref[...].astype(BF16)
    for n in range(D_MODEL // COL_TILE):
        ns = slice(n * COL_TILE, (n + 1) * COL_TILE)
        proj = jnp.dot(pb, wp_ref[:, ns], preferred_element_type=F32)
        gate = jnp.dot(xb, wg_ref[:, ns], preferred_element_type=F32)
        o_ref[:, ns] = DN_ALPHA * x_ref[:, ns] + proj * jax.nn.sigmoid(gate)
    _norm_rows(o_ref, g_ref, b_ref)


def _ple_ln(x, p, w_proj, w_gate, layer, g, b, tm):
    n = x.shape[0]
    return pl.pallas_call(
        _ple_kernel,
        grid=(n // tm,),
        in_specs=[
            pl.BlockSpec((tm, D_MODEL), lambda i: (i, 0)),
            pl.BlockSpec((None, tm, PLE_DIM), lambda i: (layer, i, 0)),
            _resident((None, PLE_DIM, D_MODEL), lambda i: (layer, 0, 0)),
            _resident((None, D_MODEL, D_MODEL), lambda i: (layer, 0, 0)),
            _resident((1, D_MODEL), lambda i: (0, 0)),
            _resident((1, D_MODEL), lambda i: (0, 0)),
        ],
        out_specs=pl.BlockSpec((tm, D_MODEL), lambda i: (i, 0)),
        out_shape=jax.ShapeDtypeStruct((n, D_MODEL), F32),
        compiler_params=_cparams("parallel"),
        name="ple_ln",
    )(x, p, w_proj, w_gate, g, b)


def _s5_discretize(ar, ai, log_dt):
    dt = jnp.exp(log_dt)
    mag = jnp.exp(ar * dt)
    ab_re, ab_im = mag * jnp.cos(ai * dt), mag * jnp.sin(ai * dt)
    den = ar * ar + ai * ai
    nr, ni = ab_re - 1.0, ab_im
    return ab_re, ab_im, (nr * ar + ni * ai) / den, (ni * ar - nr * ai) / den


def _s5_prep_kernel(ar_r, ai_r, ldt_r, ar_c, ai_c, ldt_c, bre_ref, bim_ref, cre_ref, cim_ref, fin_ref, fout_ref,
                    mw_ref, v_ref, at_ref, cin_ref, cout_ref):
    cin_ref[...] = fin_ref[...].astype(BF16)
    cout_ref[...] = fout_ref[...].astype(BF16)
    ab_re, ab_im, cf_re, cf_im = _s5_discretize(ar_r[...], ai_r[...], ldt_r[...])
    b_re, b_im = bre_ref[...], bim_ref[...]
    c_re, c_im = cre_ref[...], cim_ref[...]
    p_re = cf_re * b_re - cf_im * b_im
    p_im = cf_re * b_im + cf_im * b_re
    ak_re, ak_im = jnp.ones_like(ab_re), jnp.zeros_like(ab_im)
    pows = []
    for _ in range(S5_CHUNK):
        pows.append((p_re, p_im))
        p_re, p_im = ab_re * p_re - ab_im * p_im, ab_re * p_im + ab_im * p_re
        ak_re, ak_im = ab_re * ak_re - ab_im * ak_im, ab_re * ak_im + ab_im * ak_re
    at_ref[:, :S5_SPB] = ak_re
    at_ref[:, S5_SPB:] = ak_im

    taps = (jnp.dot(jnp.concatenate([p[0] for p in pows], axis=0), c_re,
                    precision=lax.Precision.HIGHEST, preferred_element_type=F32)
            - jnp.dot(jnp.concatenate([p[1] for p in pows], axis=0), c_im,
                      precision=lax.Precision.HIGHEST, preferred_element_type=F32)).astype(BF16)
    zero = jnp.zeros((LANES, LANES), BF16)
    for s in range(S5_CHUNK):
        rows = slice(s * LANES, (s + 1) * LANES)
        for t in range(S5_CHUNK):
            lag = slice((t - s) * LANES, (t - s + 1) * LANES)
            mw_ref[rows, t * LANES:(t + 1) * LANES] = taps[lag, :] if t >= s else zero
        w_re, w_im = pows[S5_CHUNK - 1 - s]
        mw_ref[rows, S5_UW:S5_UW + S5_SPB] = w_re.astype(BF16)
        mw_ref[rows, S5_UW + S5_SPB:] = w_im.astype(BF16)

    ac_re, ac_im, _, _ = _s5_discretize(ar_c[...], ai_c[...], ldt_c[...])
    ck_re, ck_im = ac_re, ac_im
    for t in range(S5_CHUNK):
        cols = slice(t * LANES, (t + 1) * LANES)
        v_ref[:S5_SPB, cols] = (ck_re * c_re - ck_im * c_im).astype(BF16)
        v_ref[S5_SPB:, cols] = (-(ck_im * c_re + ck_re * c_im)).astype(BF16)
        ck_re, ck_im = ck_re * ac_re - ck_im * ac_im, ck_re * ac_im + ck_im * ac_re


def _s5_prep(a_re, a_im, log_dt, b_re, b_im, c_re, c_im, ffn_w_in, ffn_w_out):
    fin_blk = (ffn_w_in.shape[1] // S5_CB, ffn_w_in.shape[2])
    fout_blk = (ffn_w_out.shape[1] // S5_CB, ffn_w_out.shape[2])
    assert fin_blk[0] % 16 == 0 and fout_blk[0] % 16 == 0
    eye = jnp.eye(S5_GPB, dtype=F32)

    def rows(v):
        return v.reshape(S5_CB, 1, S5_SPB)

    def cols(v):
        return v.reshape(S5_CB, S5_SPB, 1)

    def b_blockdiag(w):
        w4 = w.reshape(S5_CB, S5_GPB, S5_STATE, S5_GROUP).transpose(0, 1, 3, 2)
        return (w4[:, :, :, None, :] * eye[None, :, None, :, None]).reshape(S5_CB, LANES, S5_SPB)

    def c_blockdiag(w):
        w4 = w.reshape(S5_CB, S5_GPB, S5_GROUP, S5_STATE).transpose(0, 1, 3, 2)
        return (w4[:, :, :, None, :] * eye[None, :, None, :, None]).reshape(S5_CB, S5_SPB, LANES)

    ldt = jnp.repeat(log_dt, S5_STATE)
    row_spec = pl.BlockSpec((None, 1, S5_SPB), lambda c: (c, 0, 0))
    col_spec = pl.BlockSpec((None, S5_SPB, 1), lambda c: (c, 0, 0))
    b_spec = pl.BlockSpec((None, LANES, S5_SPB), lambda c: (c, 0, 0))
    c_spec = pl.BlockSpec((None, S5_SPB, LANES), lambda c: (c, 0, 0))
    return pl.pallas_call(
        _s5_prep_kernel,
        grid=(S5_CB,),
        in_specs=[row_spec, row_spec, row_spec, col_spec, col_spec, col_spec, b_spec, b_spec, c_spec, c_spec,
                  pl.BlockSpec((None,) + fin_blk, lambda c: (0, c, 0)),
                  pl.BlockSpec((None,) + fout_blk, lambda c: (0, c, 0))],
        out_specs=[
            pl.BlockSpec((None, S5_UW, S5_UW + 2 * S5_SPB), lambda c: (c, 0, 0)),
            pl.BlockSpec((None, 2 * S5_SPB, S5_UW), lambda c: (c, 0, 0)),
            pl.BlockSpec((None, 1, 2 * S5_SPB), lambda c: (c, 0, 0)),
            pl.BlockSpec(fin_blk, lambda c: (c, 0)),
            pl.BlockSpec(fout_blk, lambda c: (c, 0)),
        ],
        out_shape=[
            jax.ShapeDtypeStruct((S5_CB, S5_UW, S5_UW + 2 * S5_SPB), BF16),
            jax.ShapeDtypeStruct((S5_CB, 2 * S5_SPB, S5_UW), BF16),
            jax.ShapeDtypeStruct((S5_CB, 1, 2 * S5_SPB), F32),
            jax.ShapeDtypeStruct(ffn_w_in.shape[1:], BF16),
            jax.ShapeDtypeStruct(ffn_w_out.shape[1:], BF16),
        ],
        compiler_params=_cparams("parallel"),
        name="s5_prep",
    )(rows(a_re), rows(a_im), rows(ldt), cols(a_re), cols(a_im), cols(ldt),
      b_blockdiag(b_re), b_blockdiag(b_im), c_blockdiag(c_re), c_blockdiag(c_im), ffn_w_in, ffn_w_out)


def _s5_step(at_ref, h_re, h_im, s_re, s_im):
    a_re, a_im = at_ref[:, :S5_SPB], at_ref[:, S5_SPB:]
    return a_re * h_re - a_im * h_im + s_re, a_re * h_im + a_im * h_re + s_im


def _s5_prompt_kernel(*refs, tb, nr, n_cast):
    x_ref, mw_ref, v_ref, at_ref, d_ref = refs[:5]
    w32_refs = refs[5:5 + n_cast]
    y_ref, hfin_ref = refs[5 + n_cast:7 + n_cast]
    w16_refs = refs[7 + n_cast:7 + 2 * n_cast]
    hc_ref, s_ref, h_ref = refs[7 + 2 * n_cast:]
    r = pl.program_id(1)

    @pl.when(r == 0)
    def _():
        hc_ref[...] = jnp.zeros_like(hc_ref)

    for w32, w16 in zip(w32_refs, w16_refs):
        w16[...] = w32[...].astype(BF16)
    u = jnp.concatenate([x_ref[pl.ds(s, tb, stride=S5_CHUNK), :] for s in range(S5_CHUNK)], axis=1)
    ub = u.astype(BF16)
    s_ref[...] = jnp.dot(ub, mw_ref[:, S5_UW:], preferred_element_type=F32)

    def body(i, h):
        h_re, h_im = h
        row = pl.ds(i, 1)
        h_ref[row, :S5_SPB] = h_re
        h_ref[row, S5_SPB:] = h_im
        return _s5_step(at_ref, h_re, h_im, s_ref[row, :S5_SPB], s_ref[row, S5_SPB:])

    h_re, h_im = lax.fori_loop(0, tb, body, (hc_ref[:, :S5_SPB], hc_ref[:, S5_SPB:]), unroll=8)
    hc_ref[:, :S5_SPB] = h_re
    hc_ref[:, S5_SPB:] = h_im

    hb = h_ref[...].astype(BF16)
    steps = MXU_DIM // LANES
    for c in range(S5_UW // MXU_DIM):
        cols = slice(c * MXU_DIM, (c + 1) * MXU_DIM)
        k = (c + 1) * MXU_DIM
        y = (jnp.dot(ub[:, :k], mw_ref[:k, cols], preferred_element_type=F32)
             + jnp.dot(hb, v_ref[:, cols], preferred_element_type=F32)
             + jnp.concatenate([d_ref[...]] * steps, axis=1) * u[:, cols])
        for t in range(steps):
            y_ref[pl.ds(c * steps + t, tb, stride=S5_CHUNK), :] = y[:, t * LANES:(t + 1) * LANES]

    @pl.when(r == nr - 1)
    def _():
        hfin_ref[...] = hc_ref[...]


def _s5_prompt(x, mw, v, at, d, cast=()):
    n = x.shape[0]
    tb = S5_TB
    rows = tb * S5_CHUNK
    nr = n // rows
    cast_blocks = []
    for w in cast:
        assert w.shape[0] % (S5_CB * 16) == 0 and w.shape[1] % (nr * LANES) == 0
        cast_blocks.append(pl.BlockSpec((w.shape[0] // S5_CB, w.shape[1] // nr), lambda c, r: (c, r)))
    return pl.pallas_call(
        functools.partial(_s5_prompt_kernel, tb=tb, nr=nr, n_cast=len(cast)),
        grid=(S5_CB, nr),
        in_specs=[
            pl.BlockSpec((rows, LANES), lambda c, r: (r, c)),
            pl.BlockSpec((None, S5_UW, S5_UW + 2 * S5_SPB), lambda c, r: (c, 0, 0)),
            pl.BlockSpec((None, 2 * S5_SPB, S5_UW), lambda c, r: (c, 0, 0)),
            pl.BlockSpec((None, 1, 2 * S5_SPB), lambda c, r: (c, 0, 0)),
            pl.BlockSpec((None, 1, LANES), lambda c, r: (c, 0, 0)),
        ] + cast_blocks,
        out_specs=[
            pl.BlockSpec((rows, LANES), lambda c, r: (r, c)),
            pl.BlockSpec((None, 1, 2 * S5_SPB), lambda c, r: (c, 0, 0)),
        ] + cast_blocks,
        out_shape=[
            jax.ShapeDtypeStruct((n, D_MODEL), F32),
            jax.ShapeDtypeStruct((S5_CB, 1, 2 * S5_SPB), F32),
        ] + [jax.ShapeDtypeStruct(w.shape, BF16) for w in cast],
        scratch_shapes=[pltpu.VMEM((1, 2 * S5_SPB), F32), pltpu.VMEM((tb, 2 * S5_SPB), F32),
                        pltpu.VMEM((tb, 2 * S5_SPB), F32)],
        compiler_params=_cparams("parallel", "arbitrary"),
        name="s5_prompt",
    )(x, mw, v, at, d, *cast)


def _s5_sample_kernel(u_ref, hre_ref, him_ref, mw_ref, v_ref, at_ref, d_ref, y_ref, ore_ref, oim_ref, *, nblk):
    d_row = jnp.concatenate([d_ref[...]] * S5_CHUNK, axis=1)
    h_re, h_im = hre_ref[...], him_ref[...]
    for k in range(nblk):
        u = u_ref[k]
        res = jnp.dot(u.astype(BF16), mw_ref[...], preferred_element_type=F32)
        h = jnp.concatenate([h_re, h_im], axis=1).astype(BF16)
        y_ref[k] = res[:, :S5_UW] + jnp.dot(h, v_ref[...], preferred_element_type=F32) + d_row * u
        h_re, h_im = _s5_step(at_ref, h_re, h_im, res[:, S5_UW:S5_UW + S5_SPB], res[:, S5_UW + S5_SPB:])
    ore_ref[...] = h_re
    oim_ref[...] = h_im


def _s5_sample(u, h_re, h_im, mw, v, at, d):
    _, nblk, bsz, _ = u.shape
    h_spec = pl.BlockSpec((bsz, S5_SPB), lambda c: (0, c))
    u_spec = pl.BlockSpec((None, nblk, bsz, S5_UW), lambda c: (c, 0, 0, 0))
    return pl.pallas_call(
        functools.partial(_s5_sample_kernel, nblk=nblk),
        grid=(S5_CB,),
        in_specs=[
            u_spec, h_spec, h_spec,
            pl.BlockSpec((None, S5_UW, S5_UW + 2 * S5_SPB), lambda c: (c, 0, 0)),
            pl.BlockSpec((None, 2 * S5_SPB, S5_UW), lambda c: (c, 0, 0)),
            pl.BlockSpec((None, 1, 2 * S5_SPB), lambda c: (c, 0, 0)),
            pl.BlockSpec((None, 1, LANES), lambda c: (c, 0, 0)),
        ],
        out_specs=[u_spec, h_spec, h_spec],
        out_shape=[
            jax.ShapeDtypeStruct(u.shape, F32),
            jax.ShapeDtypeStruct(h_re.shape, F32),
            jax.ShapeDtypeStruct(h_im.shape, F32),
        ],
        compiler_params=_cparams("parallel"),
        name="s5_sample",
    )(u, h_re, h_im, mw, v, at, d)


def _glu_kernel(x_ref, y_ref, w_ref, g_ref, b_ref, o_ref):
    z = jax.nn.gelu(y_ref[...]).astype(BF16)
    for n in range(D_MODEL // COL_TILE):
        ns = slice(n * COL_TILE, (n + 1) * COL_TILE)
        gs = slice(D_MODEL + n * COL_TILE, D_MODEL + (n + 1) * COL_TILE)
        za = jnp.dot(z, w_ref[:, ns], preferred_element_type=F32)
        zb = jnp.dot(z, w_ref[:, gs], preferred_element_type=F32)
        o_ref[:, ns] = DN_ALPHA * x_ref[:, ns] + za * jax.nn.sigmoid(zb)
    _norm_rows(o_ref, g_ref, b_ref)


def _glu_ln(x, y, w_glu, g, b, tm):
    n = x.shape[0]
    return pl.pallas_call(
        _glu_kernel,
        grid=(n // tm,),
        in_specs=[
            pl.BlockSpec((tm, D_MODEL), lambda i: (i, 0)),
            pl.BlockSpec((tm, D_MODEL), lambda i: (i, 0)),
            _resident((D_MODEL, 2 * D_MODEL), lambda i: (0, 0)),
            _resident((1, D_MODEL), lambda i: (0, 0)),
            _resident((1, D_MODEL), lambda i: (0, 0)),
        ],
        out_specs=pl.BlockSpec((tm, D_MODEL), lambda i: (i, 0)),
        out_shape=jax.ShapeDtypeStruct((n, D_MODEL), F32),
        compiler_params=_cparams("parallel"),
        name="glu_ln",
    )(x, y, w_glu, g, b)


def _proj_kernel(x_ref, w_ref, o_ref, xb_ref):
    @pl.when(pl.program_id(1) == 0)
    def _():
        xb_ref[...] = x_ref[...].astype(BF16)

    o_ref[...] = jnp.dot(xb_ref[...], w_ref[...], preferred_element_type=F32).astype(o_ref.dtype)


def _proj(x, w, col0, ncols, out_dtype, tm, tn, row0=0, nrows=None):
    nrows = x.shape[0] if nrows is None else nrows
    assert row0 % tm == 0 and col0 % tn == 0
    rb, cb = row0 // tm, col0 // tn
    return pl.pallas_call(
        _proj_kernel,
        grid=(nrows // tm, ncols // tn),
        in_specs=[
            pl.BlockSpec((tm, D_MODEL), lambda i, j: (i + rb, 0)),
            pl.BlockSpec((D_MODEL, tn), lambda i, j: (0, j + cb)),
        ],
        out_specs=pl.BlockSpec((tm, tn), lambda i, j: (i, j)),
        out_shape=jax.ShapeDtypeStruct((nrows, ncols), out_dtype),
        scratch_shapes=[pltpu.VMEM((tm, D_MODEL), BF16)],
        compiler_params=_cparams("parallel", "arbitrary"),
        name="proj",
    )(x, w)


def _wo_kernel(x_ref, a_ref, w_ref, g_ref, b_ref, o_ref):
    a = a_ref[...]
    for n in range(D_MODEL // COL_TILE):
        ns = slice(n * COL_TILE, (n + 1) * COL_TILE)
        o_ref[:, ns] = DN_ALPHA * x_ref[:, ns] + jnp.dot(a, w_ref[:, ns], preferred_element_type=F32)
    _norm_rows(o_ref, g_ref, b_ref)


def _wo_ln(x, a, w_o, g, b, tm):
    n = x.shape[0]
    return pl.pallas_call(
        _wo_kernel,
        grid=(n // tm,),
        in_specs=[
            pl.BlockSpec((tm, D_MODEL), lambda i: (i, 0)),
            pl.BlockSpec((tm, D_MODEL), lambda i: (i, 0)),
            _resident((D_MODEL, D_MODEL), lambda i: (0, 0)),
            _resident((1, D_MODEL), lambda i: (0, 0)),
            _resident((1, D_MODEL), lambda i: (0, 0)),
        ],
        out_specs=pl.BlockSpec((tm, D_MODEL), lambda i: (i, 0)),
        out_shape=jax.ShapeDtypeStruct((n, D_MODEL), F32),
        compiler_params=_cparams("parallel"),
        name="wo_ln",
    )(x, a, w_o, g, b)


def _bias_kernel(tab_ref, o_ref, base_ref, *, q_off, nq, nk, scale, band):
    h = pl.program_id(0)
    ntab, nw = tab_ref.shape[1], base_ref.shape[1]

    @pl.when(h == 0)
    def _():
        off = lax.broadcasted_iota(jnp.int32, (ntab, nw), 1)
        off = jnp.where(off < nk, off, off - nw)
        idx = jnp.clip(q_off - off, -REL_CLIP, REL_CLIP) + REL_CLIP
        onehot = (idx == lax.broadcasted_iota(jnp.int32, (ntab, nw), 0)).astype(F32)
        base_ref[...] = scale * jnp.dot(tab_ref[...], onehot, precision=lax.Precision.HIGHEST,
                                        preferred_element_type=F32)

    bias = pltpu.roll(jnp.broadcast_to(base_ref[pl.ds(h, 1), :], (nq, nw)), 0, 1, stride=1, stride_axis=0)
    if band:
        row = lax.broadcasted_iota(jnp.int32, (nq, nw), 0)
        col = lax.broadcasted_iota(jnp.int32, (nq, nw), 1)
        first = row & ~(CHUNK - 1)
        bias = jnp.where((col >= first) & (col < first + BAND), bias, NEG_INF)
    o_ref[...] = bias[:, :nk]


def _rel_bias(table, q_off, nq, nk, scale=1.0, band=False):
    assert nk % LANES == 0
    ntab = table.shape[1]
    ntab_pad = -(-ntab // LANES) * LANES
    nw = -(-(nq + nk) // LANES) * LANES
    table = jnp.pad(table, ((0, 0), (0, ntab_pad - ntab)))
    return pl.pallas_call(
        functools.partial(_bias_kernel, q_off=q_off, nq=nq, nk=nk, scale=scale, band=band),
        grid=(N_HEADS,),
        in_specs=[_resident((N_HEADS, ntab_pad), lambda h: (0, 0))],
        out_specs=pl.BlockSpec((None, nq, nk), lambda h: (h, 0, 0)),
        out_shape=jax.ShapeDtypeStruct((N_HEADS, nq, nk), F32),
        scratch_shapes=[pltpu.VMEM((N_HEADS, nw), F32)],
        compiler_params=_cparams("arbitrary"),
        name="rel_bias",
    )(table)


ATT_TQ = BAND_PAST
ATT_GQ = 4 * CHUNK
ATT_GK = ATT_GQ + BAND_PAST
ATT_HB = 4
LOG2E = 1.4426950408889634


def _attn_prompt_kernel(q_ref, kp_ref, kc_ref, vp_ref, vc_ref, bias_ref, o_ref, s_ref, p_ref, l_ref):
    i = pl.program_id(0)
    ng = ATT_TQ // ATT_GQ
    dims = (((1,), (1,)), ((), ()))
    units = [(hh, g) for hh in range(ATT_HB) for g in range(ng)]

    def split(g):
        lo = g * ATT_GQ
        return lo, ATT_TQ - lo

    for u, (hh, g) in enumerate(units):
        cols = slice(hh * HEAD_DIM, (hh + 1) * HEAD_DIM)
        lo, n_prev = split(g)
        q = q_ref[lo:lo + ATT_GQ, cols]
        c = ATTN_SCALE * LOG2E
        s_prev = lax.dot_general(q, kp_ref[lo:, cols], dims, preferred_element_type=F32)
        s_ref[u, :, :n_prev] = s_prev * c + bias_ref[hh, :, :n_prev]
        s_cur = lax.dot_general(q, kc_ref[:ATT_GK - n_prev, cols], dims, preferred_element_type=F32)
        s_ref[u, :, n_prev:] = s_cur * c + bias_ref[hh, :, n_prev:]

    @pl.when(i == 0)
    def _():
        for u, (hh, g) in enumerate(units):
            _, n_prev = split(g)
            s_ref[u, :, :n_prev] = jnp.full((ATT_GQ, n_prev), NEG_INF, F32)

    for u, (hh, g) in enumerate(units):
        x = s_ref[u]
        e = jnp.exp2(x - jnp.max(x, axis=-1, keepdims=True))
        l_ref[u] = jnp.sum(e, axis=-1, keepdims=True)
        p_ref[u] = e.astype(BF16)

    for u, (hh, g) in enumerate(units):
        cols = slice(hh * HEAD_DIM, (hh + 1) * HEAD_DIM)
        lo, n_prev = split(g)
        pv = (jnp.dot(p_ref[u, :, :n_prev], vp_ref[lo:, cols], preferred_element_type=F32)
              + jnp.dot(p_ref[u, :, n_prev:], vc_ref[:ATT_GK - n_prev, cols], preferred_element_type=F32))
        o_ref[lo:lo + ATT_GQ, cols] = (pv / l_ref[u]).astype(o_ref.dtype)


def _attn_prompt(qkv, bias):
    n = qkv.shape[0]
    width = ATT_HB * HEAD_DIM
    nhb = D_MODEL // width
    n_units = ATT_HB * (ATT_TQ // ATT_GQ)

    def prev(i):
        return jnp.maximum(i - 1, 0)

    return pl.pallas_call(
        _attn_prompt_kernel,
        grid=(n // ATT_TQ, nhb),
        in_specs=[
            pl.BlockSpec((ATT_TQ, width), lambda i, h: (i, h)),
            pl.BlockSpec((ATT_TQ, width), lambda i, h: (prev(i), nhb + h)),
            pl.BlockSpec((ATT_TQ, width), lambda i, h: (i, nhb + h)),
            pl.BlockSpec((ATT_TQ, width), lambda i, h: (prev(i), 2 * nhb + h)),
            pl.BlockSpec((ATT_TQ, width), lambda i, h: (i, 2 * nhb + h)),
            pl.BlockSpec((ATT_HB, ATT_GQ, ATT_GK), lambda i, h: (h, 0, 0)),
        ],
        out_specs=pl.BlockSpec((ATT_TQ, width), lambda i, h: (i, h)),
        out_shape=jax.ShapeDtypeStruct((n, D_MODEL), BF16),
        scratch_shapes=[pltpu.VMEM((n_units, ATT_GQ, ATT_GK), F32), pltpu.VMEM((n_units, ATT_GQ, ATT_GK), BF16),
                        pltpu.VMEM((n_units, ATT_GQ, 1), F32)],
        compiler_params=_cparams("parallel", "parallel"),
        name="attn_prompt",
    )(qkv, qkv, qkv, qkv, qkv, bias)


def _attn_sample_kernel(q_ref, kn_ref, vn_ref, ck_ref, cv_ref, bias_ref, o_ref, *, ncache):
    s_len = q_ref.shape[0]
    for h in range(N_HEADS):
        cols = slice(h * HEAD_DIM, (h + 1) * HEAD_DIM)
        q = q_ref[:, cols].astype(BF16)
        k_old = ck_ref[pl.ds(h, ncache, stride=N_HEADS), :].astype(BF16)
        v_old = cv_ref[pl.ds(h, ncache, stride=N_HEADS), :].astype(BF16)
        k_new, v_new = kn_ref[:, cols].astype(BF16), vn_ref[:, cols].astype(BF16)
        dims = (((1,), (1,)), ((), ()))
        s_old = lax.dot_general(q, k_old, dims, preferred_element_type=F32) * ATTN_SCALE + bias_ref[h, :, :ncache]
        s_new = (lax.dot_general(q, k_new, dims, preferred_element_type=F32) * ATTN_SCALE
                 + bias_ref[h, :, ncache:ncache + s_len])
        m = jnp.maximum(jnp.max(s_old, axis=-1, keepdims=True), jnp.max(s_new, axis=-1, keepdims=True))
        e_old, e_new = jnp.exp(s_old - m), jnp.exp(s_new - m)
        den = jnp.sum(e_old, axis=-1, keepdims=True) + jnp.sum(e_new, axis=-1, keepdims=True)
        pv = (jnp.dot(e_old.astype(BF16), v_old, preferred_element_type=F32)
              + jnp.dot(e_new.astype(BF16), v_new, preferred_element_type=F32))
        o_ref[:, cols] = (pv / den).astype(o_ref.dtype)


def _attn_sample(qkv, cache_k, cache_v, bias, bsz, s_len):
    ncache = cache_k.shape[1] // N_HEADS
    return pl.pallas_call(
        functools.partial(_attn_sample_kernel, ncache=ncache),
        grid=(bsz,),
        in_specs=[
            pl.BlockSpec((s_len, D_MODEL), lambda b: (b, 0)),
            pl.BlockSpec((s_len, D_MODEL), lambda b: (b, 1)),
            pl.BlockSpec((s_len, D_MODEL), lambda b: (b, 2)),
            pl.BlockSpec((None, ncache * N_HEADS, HEAD_DIM), lambda b: (b, 0, 0)),
            pl.BlockSpec((None, ncache * N_HEADS, HEAD_DIM), lambda b: (b, 0, 0)),
            pl.BlockSpec(bias.shape, lambda b: (0, 0, 0)),
        ],
        out_specs=pl.BlockSpec((s_len, D_MODEL), lambda b: (b, 0)),
        out_shape=jax.ShapeDtypeStruct((bsz * s_len, D_MODEL), BF16),
        compiler_params=_cparams("parallel"),
        name="attn_sample",
    )(qkv, qkv, qkv, cache_k, cache_v, bias)


def kernel(x_prompt, x_sample, state_s5_re, state_s5_im, cache_k, cache_v, p_prompt, p_sample, ffn1_w_in, ffn1_w_out, ffn2_w_in, ffn2_w_out, ln_g, ln_b, ple_w_proj, ple_w_gate, s5_a_re, s5_a_im, s5_log_dt, s5_b_re, s5_b_im, s5_c_re, s5_c_im, s5_d, s5_w_glu, attn_w_qkv, attn_w_o, attn_rel_bias):
    bsz_p, seq, _ = x_prompt.shape
    bsz_s, s_len, _ = x_sample.shape
    assert bsz_p == 1 and s_len % S5_CHUNK == 0 and seq % (S5_CHUNK * S5_TB) == 0
    n_p, n_s = bsz_p * seq, bsz_s * s_len
    tm_glu, tm_s = 512, n_s
    tm_ffn, tm_proj, tn_wide = 1024, 1024, 1024

    xp = x_prompt.reshape(n_p, D_MODEL)
    xs = x_sample.reshape(n_s, D_MODEL)
    pp = p_prompt.reshape(DEPTH, n_p, PLE_DIM)
    ps = p_sample.reshape(DEPTH, n_s, PLE_DIM)
    mw, v, at, *w1 = _s5_prep(s5_a_re, s5_a_im, s5_log_dt, s5_b_re, s5_b_im, s5_c_re, s5_c_im,
                              ffn1_w_in, ffn1_w_out)

    def norm(i, slot):
        return ln_g[i, slot].reshape(1, D_MODEL), ln_b[i, slot].reshape(1, D_MODEL)

    outs = {}
    for i in range(DEPTH):
        xp, *w2 = _ffn_ln(xp, *w1, *norm(i, 0), tm_ffn, next_w=(ffn2_w_in, ffn2_w_out, i))
        xs = _ffn_ln(xs, *w1, *norm(i, 0), tm_s)
        if i % N_MIXERS == 0:
            d = s5_d.reshape(S5_CB, 1, LANES)
            later = (s5_w_glu, attn_w_qkv, attn_w_o, ple_w_gate.reshape(DEPTH * D_MODEL, D_MODEL),
                     ple_w_proj.reshape(DEPTH * PLE_DIM, D_MODEL))
            y, h_fin, w_glu, w_qkv, w_o, w_gate, w_proj = _s5_prompt(xp, mw, v, at, d, cast=later)
            w_gate = w_gate.reshape(DEPTH, D_MODEL, D_MODEL)
            w_proj = w_proj.reshape(DEPTH, PLE_DIM, D_MODEL)
            xp = _glu_ln(xp, y, w_glu, *norm(i, 1), tm_glu)
            outs["s5_p"] = (h_fin[:, 0, :S5_SPB].reshape(bsz_p, S5_GROUPS, S5_STATE),
                            h_fin[:, 0, S5_SPB:].reshape(bsz_p, S5_GROUPS, S5_STATE))
            nblk = s_len // S5_CHUNK
            u = (xs.reshape(bsz_s, nblk, S5_CHUNK, S5_CB, LANES).transpose(3, 1, 0, 2, 4)
                 .reshape(S5_CB, nblk, bsz_s, S5_UW))
            ys, hs_re, hs_im = _s5_sample(u, state_s5_re.reshape(bsz_s, -1), state_s5_im.reshape(bsz_s, -1),
                                          mw, v, at, d)
            ys = (ys.reshape(S5_CB, nblk, bsz_s, S5_CHUNK, LANES).transpose(2, 1, 3, 0, 4)
                  .reshape(n_s, D_MODEL))
            xs = _glu_ln(xs, ys, w_glu, *norm(i, 1), tm_s)
            outs["s5_s"] = (hs_re.reshape(bsz_s, S5_GROUPS, S5_STATE), hs_im.reshape(bsz_s, S5_GROUPS, S5_STATE))
        else:
            rows = min(BAND_PAST, seq)
            assert seq % ATT_TQ == 0
            qkv_p = _proj(xp, w_qkv, 0, 3 * D_MODEL, BF16, tm_proj, 2 * tn_wide)
            kv_tail = _proj(xp, w_qkv, D_MODEL, 2 * D_MODEL, F32, rows, tn_wide, row0=n_p - rows, nrows=rows)
            bias_p = _rel_bias(attn_rel_bias, BAND_PAST, ATT_GQ, ATT_GK, scale=LOG2E, band=True)
            att_p = _attn_prompt(qkv_p, bias_p)
            xp = _wo_ln(xp, att_p, w_o, *norm(i, 1), tm_ffn)
            outs["kv_p"] = (kv_tail[:, :D_MODEL].reshape(bsz_p, rows, N_HEADS, HEAD_DIM),
                            kv_tail[:, D_MODEL:].reshape(bsz_p, rows, N_HEADS, HEAD_DIM))

            ncache = cache_k.shape[1]
            qkv_s = _proj(xs, w_qkv, 0, 3 * D_MODEL, F32, tm_s, tn_wide)
            nk_pad = -(-(ncache + s_len) // LANES) * LANES
            bias_s = _rel_bias(attn_rel_bias, ncache, s_len, nk_pad)
            att_s = _attn_sample(qkv_s, cache_k.reshape(bsz_s, ncache * N_HEADS, HEAD_DIM),
                                 cache_v.reshape(bsz_s, ncache * N_HEADS, HEAD_DIM), bias_s, bsz_s, s_len)
            xs = _wo_ln(xs, att_s, w_o, *norm(i, 1), tm_s)
            outs["kv_s"] = (qkv_s[:, D_MODEL:2 * D_MODEL].reshape(bsz_s, s_len, N_HEADS, HEAD_DIM),
                            qkv_s[:, 2 * D_MODEL:].reshape(bsz_s, s_len, N_HEADS, HEAD_DIM))
        if i + 1 < DEPTH:
            xp, *w1 = _ffn_ln(xp, *w2, *norm(i, 2), tm_ffn, next_w=(ffn1_w_in, ffn1_w_out, i + 1))
        else:
            xp = _ffn_ln(xp, *w2, *norm(i, 2), tm_ffn)
        xs = _ffn_ln(xs, *w2, *norm(i, 2), tm_s)
        xp = _ple_ln(xp, pp, w_proj, w_gate, i, *norm(i, 3), tm_ffn)
        xs = _ple_ln(xs, ps, w_proj, w_gate, i, *norm(i, 3), tm_s)

    return (xp.reshape(bsz_p, seq, D_MODEL), xs.reshape(bsz_s, s_len, D_MODEL),
            *outs["s5_p"], *outs["kv_p"], *outs["s5_s"], *outs["kv_s"])
```

```python
import functools

import jax
import jax.numpy as jnp
from jax import lax
from jax.experimental import pallas as pl
from jax.experimental.pallas import tpu as pltpu

F32 = jnp.float32
BF16 = jnp.bfloat16

D_MODEL = 2048
DEPTH = 2
N_MIXERS = 2
CHUNK = 64
S5_GROUP = 16
S5_GROUPS = D_MODEL // S5_GROUP
S5_STATE = 64
N_HEADS = 16
HEAD_DIM = D_MODEL // N_HEADS
PAST_CHUNKS = 8
BAND_PAST = PAST_CHUNKS * CHUNK
BAND = BAND_PAST + CHUNK
REL_CLIP = 128
ATTN_SCALE = HEAD_DIM ** -0.5
NEG_INF = -1e30
D_FF = 5632
PLE_DIM = 256
DN_ALPHA = (2 * DEPTH) ** 0.25
LN_EPS = 1e-5

LANES = 128
MXU_DIM = 256
VMEM_LIMIT = 60 * 1024 * 1024

TF = 512
FFN_SUB = MXU_DIM
FFN_OUT = 512
LN_ROWS = 128
COL_TILE = 512

S5_CB = D_MODEL // LANES
S5_GPB = LANES // S5_GROUP
S5_SPB = S5_GPB * S5_STATE
S5_CHUNK = 8
S5_UW = S5_CHUNK * LANES
S5_TB = 512


def _cparams(*sem):
    return pltpu.CompilerParams(dimension_semantics=sem, vmem_limit_bytes=VMEM_LIMIT)


def _layer_norm(v, g, b, eps=LN_EPS):
    mu = jnp.mean(v, axis=-1, keepdims=True)
    d = v - mu
    var = jnp.mean(d * d, axis=-1, keepdims=True)
    return d * lax.rsqrt(var + eps) * g + b


def _norm_rows(o_ref, g_ref, b_ref, eps=LN_EPS):
    rows = min(LN_ROWS, o_ref.shape[0])
    for r in range(o_ref.shape[0] // rows):
        rs = slice(r * rows, (r + 1) * rows)
        o_ref[rs, :] = _layer_norm(o_ref[rs, :], g_ref[...], b_ref[...], eps)


def _resident(block_shape, index_map):
    return pl.BlockSpec(block_shape, index_map, pipeline_mode=pl.Buffered(1))


def _ffn_kernel(*refs, nj, cast_next):
    if cast_next:
        x_ref, wg_ref, wu_ref, wo_ref, g_ref, b_ref, nin_ref, nout_ref, o_ref, cin_ref, cout_ref, xb_ref = refs
    else:
        x_ref, wg_ref, wu_ref, wo_ref, g_ref, b_ref, o_ref, xb_ref = refs
    j = pl.program_id(1)

    @pl.when(j == 0)
    def _():
        x = x_ref[...]
        xb_ref[...] = x.astype(BF16)
        o_ref[...] = (2.0 * DN_ALPHA) * x

    if cast_next:
        cin_ref[...] = nin_ref[...].astype(BF16)
        cout_ref[...] = nout_ref[...].astype(BF16)
    xb = xb_ref[...]
    acts = []
    for h in range(TF // FFN_SUB):
        cs = slice(h * FFN_SUB, (h + 1) * FFN_SUB)
        gate = jnp.dot(xb, wg_ref[:, cs], preferred_element_type=F32)
        up = jnp.dot(xb, wu_ref[:, cs], preferred_element_type=F32)
        acts.append((gate * jax.nn.sigmoid(gate) * up).astype(BF16))
    act = jnp.concatenate(acts, axis=1)
    for n in range(D_MODEL // FFN_OUT):
        ns = slice(n * FFN_OUT, (n + 1) * FFN_OUT)
        o_ref[:, ns] += jnp.dot(act, wo_ref[:, ns], preferred_element_type=F32)

    @pl.when(j == nj - 1)
    def _():
        _norm_rows(o_ref, g_ref, b_ref, eps=4.0 * LN_EPS)


def _ffn_ln(x, w_in, w_out, g, b, tm, next_w=None):
    n = x.shape[0]
    ni, nj = n // tm, D_FF // TF
    in_specs = [
        pl.BlockSpec((tm, D_MODEL), lambda i, j: (i, 0)),
        pl.BlockSpec((D_MODEL, TF), lambda i, j: (0, j)),
        pl.BlockSpec((D_MODEL, TF), lambda i, j: (0, j + nj)),
        pl.BlockSpec((TF, D_MODEL), lambda i, j: (j, 0)),
        pl.BlockSpec((1, D_MODEL), lambda i, j: (0, 0)),
        pl.BlockSpec((1, D_MODEL), lambda i, j: (0, 0)),
    ]
    out_specs = [pl.BlockSpec((tm, D_MODEL), lambda i, j: (i, 0))]
    out_shape = [jax.ShapeDtypeStruct((n, D_MODEL), F32)]
    args = [x, w_in, w_in, w_out, g, b]
    if next_w is not None:
        nw_in, nw_out, layer = next_w
        assert D_MODEL % ni == 0 and (2 * D_FF) % nj == 0 and D_FF % nj == 0
        in_blk = (D_MODEL // ni, 2 * D_FF // nj)
        out_blk = (D_FF // nj, D_MODEL // ni)
        in_specs += [pl.BlockSpec((None,) + in_blk, lambda i, j: (layer, i, j)),
                     pl.BlockSpec((None,) + out_blk, lambda i, j: (layer, j, i))]
        out_specs += [pl.BlockSpec(in_blk, lambda i, j: (i, j)), pl.BlockSpec(out_blk, lambda i, j: (j, i))]
        out_shape += [jax.ShapeDtypeStruct(nw_in.shape[1:], BF16), jax.ShapeDtypeStruct(nw_out.shape[1:], BF16)]
        args += [nw_in, nw_out]
    res = pl.pallas_call(
        functools.partial(_ffn_kernel, nj=nj, cast_next=next_w is not None),
        grid=(ni, nj),
        in_specs=in_specs,
        out_specs=out_specs,
        out_shape=out_shape,
        scratch_shapes=[pltpu.VMEM((tm, D_MODEL), BF16)],
        compiler_params=_cparams("parallel", "arbitrary"),
        name="ffn_ln",
    )(*args)
    return res[0] if next_w is None else res


def _ple_kernel(x_ref, p_ref, wp_ref, wg_ref, g_ref, b_ref, o_ref):
    xb = x_ref[...].astype(BF16)
    pb = p_ref[...].astype(BF16)
    for n in range(D_MODEL // COL_TILE):
        ns = slice(n * COL_TILE, (n + 1) * COL_TILE)
        proj = jnp.dot(pb, wp_ref[:, ns], preferred_element_type=F32)
        gate = jnp.dot(xb, wg_ref[:, ns], preferred_element_type=F32)
        o_ref[:, ns] = DN_ALPHA * x_ref[:, ns] + proj * jax.nn.sigmoid(gate)
    _norm_rows(o_ref, g_ref, b_ref)


def _ple_ln(x, p, w_proj, w_gate, layer, g, b, tm):
    n = x.shape[0]
    return pl.pallas_call(
        _ple_kernel,
        grid=(n // tm,),
        in_specs=[
            pl.BlockSpec((tm, D_MODEL), lambda i: (i, 0)),
            pl.BlockSpec((None, tm, PLE_DIM), lambda i: (layer, i, 0)),
            _resident((None, PLE_DIM, D_MODEL), lambda i: (layer, 0, 0)),
            _resident((None, D_MODEL, D_MODEL), lambda i: (layer, 0, 0)),
            _resident((1, D_MODEL), lambda i: (0, 0)),
            _resident((1, D_MODEL), lambda i: (0, 0)),
        ],
        out_specs=pl.BlockSpec((tm, D_MODEL), lambda i: (i, 0)),
        out_shape=jax.ShapeDtypeStruct((n, D_MODEL), F32),
        compiler_params=_cparams("parallel"),
        name="ple_ln",
    )(x, p, w_proj, w_gate, g, b)


def _s5_discretize(ar, ai, log_dt):
    dt = jnp.exp(log_dt)
    mag = jnp.exp(ar * dt)
    ab_re, ab_im = mag * jnp.cos(ai * dt), mag * jnp.sin(ai * dt)
    den = ar * ar + ai * ai
    nr, ni = ab_re - 1.0, ab_im
    return ab_re, ab_im, (nr * ar + ni * ai) / den, (ni * ar - nr * ai) / den


def _s5_prep_kernel(ar_r, ai_r, ldt_r, ar_c, ai_c, ldt_c, bre_ref, bim_ref, cre_ref, cim_ref, fin_ref, fout_ref,
                    mw_ref, v_ref, at_ref, cin_ref, cout_ref):
    cin_ref[...] = fin_ref[...].astype(BF16)
    cout_ref[...] = fout_ref[...].astype(BF16)
    ab_re, ab_im, cf_re, cf_im = _s5_discretize(ar_r[...], ai_r[...], ldt_r[...])
    b_re, b_im = bre_ref[...], bim_ref[...]
    c_re, c_im = cre_ref[...], cim_ref[...]
    p_re = cf_re * b_re - cf_im * b_im
    p_im = cf_re * b_im + cf_im * b_re
    ak_re, ak_im = jnp.ones_like(ab_re), jnp.zeros_like(ab_im)
    pows = []
    for _ in range(S5_CHUNK):
        pows.append((p_re, p_im))
        p_re, p_im = ab_re * p_re - ab_im * p_im, ab_re * p_im + ab_im * p_re
        ak_re, ak_im = ab_re * ak_re - ab_im * ak_im, ab_re * ak_im + ab_im * ak_re
    at_ref[:, :S5_SPB] = ak_re
    at_ref[:, S5_SPB:] = ak_im

    def split(m):
        head = m.astype(BF16)
        return head, (m - head.astype(F32)).astype(BF16)

    def dot3(lhs, rhs):
        l_hi, l_lo = split(lhs)
        r_hi, r_lo = split(rhs)
        return (jnp.dot(l_hi, r_hi, preferred_element_type=F32)
                + (jnp.dot(l_hi, r_lo, preferred_element_type=F32) + jnp.dot(l_lo, r_hi, preferred_element_type=F32)))

    taps = (dot3(jnp.concatenate([p[0] for p in pows], axis=0), c_re)
            - dot3(jnp.concatenate([p[1] for p in pows], axis=0), c_im)).astype(BF16)
    zero = jnp.zeros((LANES, LANES), BF16)
    for s in range(S5_CHUNK):
        rows = slice(s * LANES, (s + 1) * LANES)
        for t in range(S5_CHUNK):
            lag = slice((t - s) * LANES, (t - s + 1) * LANES)
            mw_ref[rows, t * LANES:(t + 1) * LANES] = taps[lag, :] if t >= s else zero
        w_re, w_im = pows[S5_CHUNK - 1 - s]
        mw_ref[rows, S5_UW:S5_UW + S5_SPB] = w_re.astype(BF16)
        mw_ref[rows, S5_UW + S5_SPB:] = w_im.astype(BF16)

    ac_re, ac_im, _, _ = _s5_discretize(ar_c[...], ai_c[...], ldt_c[...])
    ck_re, ck_im = ac_re, ac_im
    for t in range(S5_CHUNK):
        cols = slice(t * LANES, (t + 1) * LANES)
        v_ref[:S5_SPB, cols] = (ck_re * c_re - ck_im * c_im).astype(BF16)
        v_ref[S5_SPB:, cols] = (-(ck_im * c_re + ck_re * c_im)).astype(BF16)
        ck_re, ck_im = ck_re * ac_re - ck_im * ac_im, ck_re * ac_im + ck_im * ac_re


def _s5_prep(a_re, a_im, log_dt, b_re, b_im, c_re, c_im, ffn_w_in, ffn_w_out):
    fin_blk = (ffn_w_in.shape[1] // S5_CB, ffn_w_in.shape[2])
    fout_blk = (ffn_w_out.shape[1] // S5_CB, ffn_w_out.shape[2])
    assert fin_blk[0] % 16 == 0 and fout_blk[0] % 16 == 0
    eye = jnp.eye(S5_GPB, dtype=F32)

    def rows(v):
        return v.reshape(S5_CB, 1, S5_SPB)

    def cols(v):
        return v.reshape(S5_CB, S5_SPB, 1)

    def b_blockdiag(w):
        w4 = w.reshape(S5_CB, S5_GPB, S5_STATE, S5_GROUP).transpose(0, 1, 3, 2)
        return (w4[:, :, :, None, :] * eye[None, :, None, :, None]).reshape(S5_CB, LANES, S5_SPB)

    def c_blockdiag(w):
        w4 = w.reshape(S5_CB, S5_GPB, S5_GROUP, S5_STATE).transpose(0, 1, 3, 2)
        return (w4[:, :, :, None, :] * eye[None, :, None, :, None]).reshape(S5_CB, S5_SPB, LANES)

    ldt = jnp.repeat(log_dt, S5_STATE)
    row_spec = pl.BlockSpec((None, 1, S5_SPB), lambda c: (c, 0, 0))
    col_spec = pl.BlockSpec((None, S5_SPB, 1), lambda c: (c, 0, 0))
    b_spec = pl.BlockSpec((None, LANES, S5_SPB), lambda c: (c, 0, 0))
    c_spec = pl.BlockSpec((None, S5_SPB, LANES), lambda c: (c, 0, 0))
    return pl.pallas_call(
        _s5_prep_kernel,
        grid=(S5_CB,),
        in_specs=[row_spec, row_spec, row_spec, col_spec, col_spec, col_spec, b_spec, b_spec, c_spec, c_spec,
                  pl.BlockSpec((None,) + fin_blk, lambda c: (0, c, 0)),
                  pl.BlockSpec((None,) + fout_blk, lambda c: (0, c, 0))],
        out_specs=[
            pl.BlockSpec((None, S5_UW, S5_UW + 2 * S5_SPB), lambda c: (c, 0, 0)),
            pl.BlockSpec((None, 2 * S5_SPB, S5_UW), lambda c: (c, 0, 0)),
            pl.BlockSpec((None, 1, 2 * S5_SPB), lambda c: (c, 0, 0)),
            pl.BlockSpec(fin_blk, lambda c: (c, 0)),
            pl.BlockSpec(fout_blk, lambda c: (c, 0)),
        ],
        out_shape=[
            jax.ShapeDtypeStruct((S5_CB, S5_UW, S5_UW + 2 * S5_SPB), BF16),
            jax.ShapeDtypeStruct((S5_CB, 2 * S5_SPB, S5_UW), BF16),
            jax.ShapeDtypeStruct((S5_CB, 1, 2 * S5_SPB), F32),
            jax.ShapeDtypeStruct(ffn_w_in.shape[1:], BF16),
            jax.ShapeDtypeStruct(ffn_w_out.shape[1:], BF16),
        ],
        compiler_params=_cparams("parallel"),
        name="s5_prep",
    )(rows(a_re), rows(a_im), rows(ldt), cols(a_re), cols(a_im), cols(ldt),
      b_blockdiag(b_re), b_blockdiag(b_im), c_blockdiag(c_re), c_blockdiag(c_im), ffn_w_in, ffn_w_out)


def _s5_step(at_ref, h_re, h_im, s_re, s_im):
    a_re, a_im = at_ref[:, :S5_SPB], at_ref[:, S5_SPB:]
    return a_re * h_re - a_im * h_im + s_re, a_re * h_im + a_im * h_re + s_im


def _s5_prompt_kernel(*refs, tb, nr, n_cast):
    x_ref, mw_ref, v_ref, at_ref, d_ref = refs[:5]
    w32_refs = refs[5:5 + n_cast]
    y_ref, hfin_ref = refs[5 + n_cast:7 + n_cast]
    w16_refs = refs[7 + n_cast:7 + 2 * n_cast]
    hc_ref, s_ref, h_ref = refs[7 + 2 * n_cast:]
    r = pl.program_id(1)

    @pl.when(r == 0)
    def _():
        hc_ref[...] = jnp.zeros_like(hc_ref)

    for w32, w16 in zip(w32_refs, w16_refs):
        w16[...] = w32[...].astype(BF16)
    u = jnp.concatenate([x_ref[pl.ds(s, tb, stride=S5_CHUNK), :] for s in range(S5_CHUNK)], axis=1)
    ub = u.astype(BF16)
    s_ref[...] = jnp.dot(ub, mw_ref[:, S5_UW:], preferred_element_type=F32)

    def body(i, h):
        h_re, h_im = h
        row = pl.ds(i, 1)
        h_ref[row, :S5_SPB] = h_re
        h_ref[row, S5_SPB:] = h_im
        return _s5_step(at_ref, h_re, h_im, s_ref[row, :S5_SPB], s_ref[row, S5_SPB:])

    h_re, h_im = lax.fori_loop(0, tb, body, (hc_ref[:, :S5_SPB], hc_ref[:, S5_SPB:]), unroll=8)
    hc_ref[:, :S5_SPB] = h_re
    hc_ref[:, S5_SPB:] = h_im

    hb = h_ref[...].astype(BF16)
    steps = MXU_DIM // LANES
    for c in range(S5_UW // MXU_DIM):
        cols = slice(c * MXU_DIM, (c + 1) * MXU_DIM)
        k = (c + 1) * MXU_DIM
        y = (jnp.dot(ub[:, :k], mw_ref[:k, cols], preferred_element_type=F32)
             + jnp.dot(hb, v_ref[:, cols], preferred_element_type=F32)
             + jnp.concatenate([d_ref[...]] * steps, axis=1) * u[:, cols])
        for t in range(steps):
            y_ref[pl.ds(c * steps + t, tb, stride=S5_CHUNK), :] = y[:, t * LANES:(t + 1) * LANES]

    @pl.when(r == nr - 1)
    def _():
        hfin_ref[...] = hc_ref[...]


def _s5_prompt(x, mw, v, at, d, cast=()):
    n = x.shape[0]
    tb = S5_TB
    rows = tb * S5_CHUNK
    nr = n // rows
    cast_blocks = []
    for w in cast:
        assert w.shape[0] % (S5_CB * 16) == 0 and w.shape[1] % (nr * LANES) == 0
        cast_blocks.append(pl.BlockSpec((w.shape[0] // S5_CB, w.shape[1] // nr), lambda c, r: (c, r)))
    return pl.pallas_call(
        functools.partial(_s5_prompt_kernel, tb=tb, nr=nr, n_cast=len(cast)),
        grid=(S5_CB, nr),
        in_specs=[
            pl.BlockSpec((rows, LANES), lambda c, r: (r, c)),
            pl.BlockSpec((None, S5_UW, S5_UW + 2 * S5_SPB), lambda c, r: (c, 0, 0)),
            pl.BlockSpec((None, 2 * S5_SPB, S5_UW), lambda c, r: (c, 0, 0)),
            pl.BlockSpec((None, 1, 2 * S5_SPB), lambda c, r: (c, 0, 0)),
            pl.BlockSpec((None, 1, LANES), lambda c, r: (c, 0, 0)),
        ] + cast_blocks,
        out_specs=[
            pl.BlockSpec((rows, LANES), lambda c, r: (r, c)),
            pl.BlockSpec((None, 1, 2 * S5_SPB), lambda c, r: (c, 0, 0)),
        ] + cast_blocks,
        out_shape=[
            jax.ShapeDtypeStruct((n, D_MODEL), F32),
            jax.ShapeDtypeStruct((S5_CB, 1, 2 * S5_SPB), F32),
        ] + [jax.ShapeDtypeStruct(w.shape, BF16) for w in cast],
        scratch_shapes=[pltpu.VMEM((1, 2 * S5_SPB), F32), pltpu.VMEM((tb, 2 * S5_SPB), F32),
                        pltpu.VMEM((tb, 2 * S5_SPB), F32)],
        compiler_params=_cparams("parallel", "arbitrary"),
        name="s5_prompt",
    )(x, mw, v, at, d, *cast)


def _s5_sample_kernel(u_ref, hre_ref, him_ref, mw_ref, v_ref, at_ref, d_ref, y_ref, ore_ref, oim_ref, *, nblk):
    d_row = jnp.concatenate([d_ref[...]] * S5_CHUNK, axis=1)
    h_re, h_im = hre_ref[...], him_ref[...]
    for k in range(nblk):
        u = u_ref[k]
        res = jnp.dot(u.astype(BF16), mw_ref[...], preferred_element_type=F32)
        h = jnp.concatenate([h_re, h_im], axis=1).astype(BF16)
        y_ref[k] = res[:, :S5_UW] + jnp.dot(h, v_ref[...], preferred_element_type=F32) + d_row * u
        h_re, h_im = _s5_step(at_ref, h_re, h_im, res[:, S5_UW:S5_UW + S5_SPB], res[:, S5_UW + S5_SPB:])
    ore_ref[...] = h_re
    oim_ref[...] = h_im


def _s5_sample(u, h_re, h_im, mw, v, at, d):
    _, nblk, bsz, _ = u.shape
    h_spec = pl.BlockSpec((bsz, S5_SPB), lambda c: (0, c))
    u_spec = pl.BlockSpec((None, nblk, bsz, S5_UW), lambda c: (c, 0, 0, 0))
    return pl.pallas_call(
        functools.partial(_s5_sample_kernel, nblk=nblk),
        grid=(S5_CB,),
        in_specs=[
            u_spec, h_spec, h_spec,
            pl.BlockSpec((None, S5_UW, S5_UW + 2 * S5_SPB), lambda c: (c, 0, 0)),
            pl.BlockSpec((None, 2 * S5_SPB, S5_UW), lambda c: (c, 0, 0)),
            pl.BlockSpec((None, 1, 2 * S5_SPB), lambda c: (c, 0, 0)),
            pl.BlockSpec((None, 1, LANES), lambda c: (c, 0, 0)),
        ],
        out_specs=[u_spec, h_spec, h_spec],
        out_shape=[
            jax.ShapeDtypeStruct(u.shape, F32),
            jax.ShapeDtypeStruct(h_re.shape, F32),
            jax.ShapeDtypeStruct(h_im.shape, F32),
        ],
        compiler_params=_cparams("parallel"),
        name="s5_sample",
    )(u, h_re, h_im, mw, v, at, d)


def _glu_kernel(x_ref, y_ref, w_ref, g_ref, b_ref, o_ref):
    z = jax.nn.gelu(y_ref[...]).astype(BF16)
    for n in range(D_MODEL // COL_TILE):
        ns = slice(n * COL_TILE, (n + 1) * COL_TILE)
        gs = slice(D_MODEL + n * COL_TILE, D_MODEL + (n + 1) * COL_TILE)
        za = jnp.dot(z, w_ref[:, ns], preferred_element_type=F32)
        zb = jnp.dot(z, w_ref[:, gs], preferred_element_type=F32)
        o_ref[:, ns] = DN_ALPHA * x_ref[:, ns] + za * jax.nn.sigmoid(zb)
    _norm_rows(o_ref, g_ref, b_ref)


def _glu_ln(x, y, w_glu, g, b, tm):
    n = x.shape[0]
    return pl.pallas_call(
        _glu_kernel,
        grid=(n // tm,),
        in_specs=[
            pl.BlockSpec((tm, D_MODEL), lambda i: (i, 0)),
            pl.BlockSpec((tm, D_MODEL), lambda i: (i, 0)),
            _resident((D_MODEL, 2 * D_MODEL), lambda i: (0, 0)),
            _resident((1, D_MODEL), lambda i: (0, 0)),
            _resident((1, D_MODEL), lambda i: (0, 0)),
        ],
        out_specs=pl.BlockSpec((tm, D_MODEL), lambda i: (i, 0)),
        out_shape=jax.ShapeDtypeStruct((n, D_MODEL), F32),
        compiler_params=_cparams("parallel"),
        name="glu_ln",
    )(x, y, w_glu, g, b)


def _proj_kernel(x_ref, w_ref, o_ref, xb_ref):
    @pl.when(pl.program_id(1) == 0)
    def _():
        xb_ref[...] = x_ref[...].astype(BF16)

    o_ref[...] = jnp.dot(xb_ref[...], w_ref[...], preferred_element_type=F32).astype(o_ref.dtype)


def _proj(x, w, col0, ncols, out_dtype, tm, tn, row0=0, nrows=None):
    nrows = x.shape[0] if nrows is None else nrows
    assert row0 % tm == 0 and col0 % tn == 0
    rb, cb = row0 // tm, col0 // tn
    return pl.pallas_call(
        _proj_kernel,
        grid=(nrows // tm, ncols // tn),
        in_specs=[
            pl.BlockSpec((tm, D_MODEL), lambda i, j: (i + rb, 0)),
            pl.BlockSpec((D_MODEL, tn), lambda i, j: (0, j + cb)),
        ],
        out_specs=pl.BlockSpec((tm, tn), lambda i, j: (i, j)),
        out_shape=jax.ShapeDtypeStruct((nrows, ncols), out_dtype),
        scratch_shapes=[pltpu.VMEM((tm, D_MODEL), BF16)],
        compiler_params=_cparams("parallel", "arbitrary"),
        name="proj",
    )(x, w)


def _wo_kernel(x_ref, a_ref, w_ref, g_ref, b_ref, o_ref):
    a = a_ref[...]
    for n in range(D_MODEL // COL_TILE):
        ns = slice(n * COL_TILE, (n + 1) * COL_TILE)
        o_ref[:, ns] = DN_ALPHA * x_ref[:, ns] + jnp.dot(a, w_ref[:, ns], preferred_element_type=F32)
    _norm_rows(o_ref, g_ref, b_ref)


def _wo_ln(x, a, w_o, g, b, tm):
    n = x.shape[0]
    return pl.pallas_call(
        _wo_kernel,
        grid=(n // tm,),
        in_specs=[
            pl.BlockSpec((tm, D_MODEL), lambda i: (i, 0)),
            pl.BlockSpec((tm, D_MODEL), lambda i: (i, 0)),
            _resident((D_MODEL, D_MODEL), lambda i: (0, 0)),
            _resident((1, D_MODEL), lambda i: (0, 0)),
            _resident((1, D_MODEL), lambda i: (0, 0)),
        ],
        out_specs=pl.BlockSpec((tm, D_MODEL), lambda i: (i, 0)),
        out_shape=jax.ShapeDtypeStruct((n, D_MODEL), F32),
        compiler_params=_cparams("parallel"),
        name="wo_ln",
    )(x, a, w_o, g, b)


def _bias_kernel(tab_ref, o_ref, base_ref, *, q_off, nq, nk, scale, band):
    h = pl.program_id(0)
    ntab, nw = tab_ref.shape[1], base_ref.shape[1]

    @pl.when(h == 0)
    def _():
        off = lax.broadcasted_iota(jnp.int32, (ntab, nw), 1)
        off = jnp.where(off < nk, off, off - nw)
        idx = jnp.clip(q_off - off, -REL_CLIP, REL_CLIP) + REL_CLIP
        onehot = (idx == lax.broadcasted_iota(jnp.int32, (ntab, nw), 0)).astype(F32)
        base_ref[...] = scale * jnp.dot(tab_ref[...], onehot, precision=lax.Precision.HIGHEST,
                                        preferred_element_type=F32)

    bias = pltpu.roll(jnp.broadcast_to(base_ref[pl.ds(h, 1), :], (nq, nw)), 0, 1, stride=1, stride_axis=0)
    if band:
        row = lax.broadcasted_iota(jnp.int32, (nq, nw), 0)
        col = lax.broadcasted_iota(jnp.int32, (nq, nw), 1)
        first = row & ~(CHUNK - 1)
        bias = jnp.where((col >= first) & (col < first + BAND), bias, NEG_INF)
    o_ref[...] = bias[:, :nk]


def _rel_bias(table, q_off, nq, nk, scale=1.0, band=False):
    assert nk % LANES == 0
    ntab = table.shape[1]
    ntab_pad = -(-ntab // LANES) * LANES
    nw = -(-(nq + nk) // LANES) * LANES
    table = jnp.pad(table, ((0, 0), (0, ntab_pad - ntab)))
    return pl.pallas_call(
        functools.partial(_bias_kernel, q_off=q_off, nq=nq, nk=nk, scale=scale, band=band),
        grid=(N_HEADS,),
        in_specs=[_resident((N_HEADS, ntab_pad), lambda h: (0, 0))],
        out_specs=pl.BlockSpec((None, nq, nk), lambda h: (h, 0, 0)),
        out_shape=jax.ShapeDtypeStruct((N_HEADS, nq, nk), F32),
        scratch_shapes=[pltpu.VMEM((N_HEADS, nw), F32)],
        compiler_params=_cparams("arbitrary"),
        name="rel_bias",
    )(table)


ATT_TQ = BAND_PAST
ATT_GQ = 4 * CHUNK
ATT_GK = ATT_GQ + BAND_PAST
ATT_HB = 4
LOG2E = 1.4426950408889634


def _attn_prompt_kernel(q_ref, kp_ref, kc_ref, vp_ref, vc_ref, bias_ref, o_ref, s_ref, p_ref, l_ref):
    i = pl.program_id(0)
    ng = ATT_TQ // ATT_GQ
    dims = (((1,), (1,)), ((), ()))
    units = [(hh, g) for hh in range(ATT_HB) for g in range(ng)]

    def split(g):
        lo = g * ATT_GQ
        return lo, ATT_TQ - lo

    for u, (hh, g) in enumerate(units):
        cols = slice(hh * HEAD_DIM, (hh + 1) * HEAD_DIM)
        lo, n_prev = split(g)
        q = q_ref[lo:lo + ATT_GQ, cols]
        c = ATTN_SCALE * LOG2E
        s_prev = lax.dot_general(q, kp_ref[lo:, cols], dims, preferred_element_type=F32)
        s_ref[u, :, :n_prev] = s_prev * c + bias_ref[hh, :, :n_prev]
        s_cur = lax.dot_general(q, kc_ref[:ATT_GK - n_prev, cols], dims, preferred_element_type=F32)
        s_ref[u, :, n_prev:] = s_cur * c + bias_ref[hh, :, n_prev:]

    @pl.when(i == 0)
    def _():
        for u, (hh, g) in enumerate(units):
            _, n_prev = split(g)
            s_ref[u, :, :n_prev] = jnp.full((ATT_GQ, n_prev), NEG_INF, F32)

    for u, (hh, g) in enumerate(units):
        x = s_ref[u]
        e = jnp.exp2(x - jnp.max(x, axis=-1, keepdims=True))
        l_ref[u] = jnp.sum(e, axis=-1, keepdims=True)
        p_ref[u] = e.astype(BF16)

    for u, (hh, g) in enumerate(units):
        cols = slice(hh * HEAD_DIM, (hh + 1) * HEAD_DIM)
        lo, n_prev = split(g)
        pv = (jnp.dot(p_ref[u, :, :n_prev], vp_ref[lo:, cols], preferred_element_type=F32)
              + jnp.dot(p_ref[u, :, n_prev:], vc_ref[:ATT_GK - n_prev, cols], preferred_element_type=F32))
        o_ref[lo:lo + ATT_GQ, cols] = (pv / l_ref[u]).astype(o_ref.dtype)


def _attn_prompt(qkv, bias):
    n = qkv.shape[0]
    width = ATT_HB * HEAD_DIM
    nhb = D_MODEL // width
    n_units = ATT_HB * (ATT_TQ // ATT_GQ)

    def prev(i):
        return jnp.maximum(i - 1, 0)

    return pl.pallas_call(
        _attn_prompt_kernel,
        grid=(n // ATT_TQ, nhb),
        in_specs=[
            pl.BlockSpec((ATT_TQ, width), lambda i, h: (i, h)),
            pl.BlockSpec((ATT_TQ, width), lambda i, h: (prev(i), nhb + h)),
            pl.BlockSpec((ATT_TQ, width), lambda i, h: (i, nhb + h)),
            pl.BlockSpec((ATT_TQ, width), lambda i, h: (prev(i), 2 * nhb + h)),
            pl.BlockSpec((ATT_TQ, width), lambda i, h: (i, 2 * nhb + h)),
            pl.BlockSpec((ATT_HB, ATT_GQ, ATT_GK), lambda i, h: (h, 0, 0)),
        ],
        out_specs=pl.BlockSpec((ATT_TQ, width), lambda i, h: (i, h)),
        out_shape=jax.ShapeDtypeStruct((n, D_MODEL), BF16),
        scratch_shapes=[pltpu.VMEM((n_units, ATT_GQ, ATT_GK), F32), pltpu.VMEM((n_units, ATT_GQ, ATT_GK), BF16),
                        pltpu.VMEM((n_units, ATT_GQ, 1), F32)],
        compiler_params=_cparams("parallel", "parallel"),
        name="attn_prompt",
    )(qkv, qkv, qkv, qkv, qkv, bias)


def _attn_sample_kernel(q_ref, kn_ref, vn_ref, ck_ref, cv_ref, bias_ref, o_ref, *, ncache):
    s_len = q_ref.shape[0]
    for h in range(N_HEADS):
        cols = slice(h * HEAD_DIM, (h + 1) * HEAD_DIM)
        q = q_ref[:, cols].astype(BF16)
        k_old = ck_ref[pl.ds(h, ncache, stride=N_HEADS), :].astype(BF16)
        v_old = cv_ref[pl.ds(h, ncache, stride=N_HEADS), :].astype(BF16)
        k_new, v_new = kn_ref[:, cols].astype(BF16), vn_ref[:, cols].astype(BF16)
        dims = (((1,), (1,)), ((), ()))
        s_old = lax.dot_general(q, k_old, dims, preferred_element_type=F32) * ATTN_SCALE + bias_ref[h, :, :ncache]
        s_new = (lax.dot_general(q, k_new, dims, preferred_element_type=F32) * ATTN_SCALE
                 + bias_ref[h, :, ncache:ncache + s_len])
        m = jnp.maximum(jnp.max(s_old, axis=-1, keepdims=True), jnp.max(s_new, axis=-1, keepdims=True))
        e_old, e_new = jnp.exp(s_old - m), jnp.exp(s_new - m)
        den = jnp.sum(e_old, axis=-1, keepdims=True) + jnp.sum(e_new, axis=-1, keepdims=True)
        pv = (jnp.dot(e_old.astype(BF16), v_old, preferred_element_type=F32)
              + jnp.dot(e_new.astype(BF16), v_new, preferred_element_type=F32))
        o_ref[:, cols] = (pv / den).astype(o_ref.dtype)


def _attn_sample(qkv, cache_k, cache_v, bias, bsz, s_len):
    ncache = cache_k.shape[1] // N_HEADS
    return pl.pallas_call(
        functools.partial(_attn_sample_kernel, ncache=ncache),
        grid=(bsz,),
        in_specs=[
            pl.BlockSpec((s_len, D_MODEL), lambda b: (b, 0)),
            pl.BlockSpec((s_len, D_MODEL), lambda b: (b, 1)),
            pl.BlockSpec((s_len, D_MODEL), lambda b: (b, 2)),
            pl.BlockSpec((None, ncache * N_HEADS, HEAD_DIM), lambda b: (b, 0, 0)),
            pl.BlockSpec((None, ncache * N_HEADS, HEAD_DIM), lambda b: (b, 0, 0)),
            pl.BlockSpec(bias.shape, lambda b: (0, 0, 0)),
        ],
        out_specs=pl.BlockSpec((s_len, D_MODEL), lambda b: (b, 0)),
        out_shape=jax.ShapeDtypeStruct((bsz * s_len, D_MODEL), BF16),
        compiler_params=_cparams("parallel"),
        name="attn_sample",
    )(qkv, qkv, qkv, cache_k, cache_v, bias)


def kernel(x_prompt, x_sample, state_s5_re, state_s5_im, cache_k, cache_v, p_prompt, p_sample, ffn1_w_in, ffn1_w_out, ffn2_w_in, ffn2_w_out, ln_g, ln_b, ple_w_proj, ple_w_gate, s5_a_re, s5_a_im, s5_log_dt, s5_b_re, s5_b_im, s5_c_re, s5_c_im, s5_d, s5_w_glu, attn_w_qkv, attn_w_o, attn_rel_bias):
    bsz_p, seq, _ = x_prompt.shape
    bsz_s, s_len, _ = x_sample.shape
    assert bsz_p == 1 and s_len % S5_CHUNK == 0 and seq % (S5_CHUNK * S5_TB) == 0
    n_p, n_s = bsz_p * seq, bsz_s * s_len
    tm_glu, tm_s = 512, n_s
    tm_ffn, tm_proj, tn_wide = 1024, 1024, 1024

    xp = x_prompt.reshape(n_p, D_MODEL)
    xs = x_sample.reshape(n_s, D_MODEL)
    pp = p_prompt.reshape(DEPTH, n_p, PLE_DIM)
    ps = p_sample.reshape(DEPTH, n_s, PLE_DIM)
    mw, v, at, *w1 = _s5_prep(s5_a_re, s5_a_im, s5_log_dt, s5_b_re, s5_b_im, s5_c_re, s5_c_im,
                              ffn1_w_in, ffn1_w_out)

    def norm(i, slot):
        return ln_g[i, slot].reshape(1, D_MODEL), ln_b[i, slot].reshape(1, D_MODEL)

    outs = {}
    for i in range(DEPTH):
        xp, *w2 = _ffn_ln(xp, *w1, *norm(i, 0), tm_ffn, next_w=(ffn2_w_in, ffn2_w_out, i))
        xs = _ffn_ln(xs, *w1, *norm(i, 0), tm_s)
        if i % N_MIXERS == 0:
            d = s5_d.reshape(S5_CB, 1, LANES)
            later = (s5_w_glu, attn_w_qkv, attn_w_o, ple_w_gate.reshape(DEPTH * D_MODEL, D_MODEL),
                     ple_w_proj.reshape(DEPTH * PLE_DIM, D_MODEL))
            y, h_fin, w_glu, w_qkv, w_o, w_gate, w_proj = _s5_prompt(xp, mw, v, at, d, cast=later)
            w_gate = w_gate.reshape(DEPTH, D_MODEL, D_MODEL)
            w_proj = w_proj.reshape(DEPTH, PLE_DIM, D_MODEL)
            xp = _glu_ln(xp, y, w_glu, *norm(i, 1), tm_glu)
            outs["s5_p"] = (h_fin[:, 0, :S5_SPB].reshape(bsz_p, S5_GROUPS, S5_STATE),
                            h_fin[:, 0, S5_SPB:].reshape(bsz_p, S5_GROUPS, S5_STATE))
            nblk = s_len // S5_CHUNK
            u = (xs.reshape(bsz_s, nblk, S5_CHUNK, S5_CB, LANES).transpose(3, 1, 0, 2, 4)
                 .reshape(S5_CB, nblk, bsz_s, S5_UW))
            ys, hs_re, hs_im = _s5_sample(u, state_s5_re.reshape(bsz_s, -1), state_s5_im.reshape(bsz_s, -1),
                                          mw, v, at, d)
            ys = (ys.reshape(S5_CB, nblk, bsz_s, S5_CHUNK, LANES).transpose(2, 1, 3, 0, 4)
                  .reshape(n_s, D_MODEL))
            xs = _glu_ln(xs, ys, w_glu, *norm(i, 1), tm_s)
            outs["s5_s"] = (hs_re.reshape(bsz_s, S5_GROUPS, S5_STATE), hs_im.reshape(bsz_s, S5_GROUPS, S5_STATE))
        else:
            rows = min(BAND_PAST, seq)
            assert seq % ATT_TQ == 0
            qkv_p = _proj(xp, w_qkv, 0, 3 * D_MODEL, BF16, tm_proj, 2 * tn_wide)
            kv_tail = _proj(xp, w_qkv, D_MODEL, 2 * D_MODEL, F32, rows, tn_wide, row0=n_p - rows, nrows=rows)
            bias_p = _rel_bias(attn_rel_bias, BAND_PAST, ATT_GQ, ATT_GK, scale=LOG2E, band=True)
            att_p = _attn_prompt(qkv_p, bias_p)
            xp = _wo_ln(xp, att_p, w_o, *norm(i, 1), tm_ffn)
            outs["kv_p"] = (kv_tail[:, :D_MODEL].reshape(bsz_p, rows, N_HEADS, HEAD_DIM),
                            kv_tail[:, D_MODEL:].reshape(bsz_p, rows, N_HEADS, HEAD_DIM))

            ncache = cache_k.shape[1]
            qkv_s = _proj(xs, w_qkv, 0, 3 * D_MODEL, F32, tm_s, tn_wide)
            nk_pad = -(-(ncache + s_len) // LANES) * LANES
            bias_s = _rel_bias(attn_rel_bias, ncache, s_len, nk_pad)
            att_s = _attn_sample(qkv_s, cache_k.reshape(bsz_s, ncache * N_HEADS, HEAD_DIM),
                                 cache_v.reshape(bsz_s, ncache * N_HEADS, HEAD_DIM), bias_s, bsz_s, s_len)
            xs = _wo_ln(xs, att_s, w_o, *norm(i, 1), tm_s)
            outs["kv_s"] = (qkv_s[:, D_MODEL:2 * D_MODEL].reshape(bsz_s, s_len, N_HEADS, HEAD_DIM),
                            qkv_s[:, 2 * D_MODEL:].reshape(bsz_s, s_len, N_HEADS, HEAD_DIM))
        if i + 1 < DEPTH:
            xp, *w1 = _ffn_ln(xp, *w2, *norm(i, 2), tm_ffn, next_w=(ffn1_w_in, ffn1_w_out, i + 1))
        else:
            xp = _ffn_ln(xp, *w2, *norm(i, 2), tm_ffn)
        xs = _ffn_ln(xs, *w2, *norm(i, 2), tm_s)
        xp = _ple_ln(xp, pp, w_proj, w_gate, i, *norm(i, 3), tm_ffn)
        xs = _ple_ln(xs, ps, w_proj, w_gate, i, *norm(i, 3), tm_s)

    return (xp.reshape(bsz_p, seq, D_MODEL), xs.reshape(bsz_s, s_len, D_MODEL),
            *outs["s5_p"], *outs["kv_p"], *outs["s5_s"], *outs["kv_s"])
```
